```python
import math
import jax, jax.numpy as jnp
from jax import lax
import numpy as np


D_MODEL = 2048
BATCH = 2
SEQ = 4096
DEPTH = 2

GRID_W = 64
A_HEAD_DIM = 128
A_WIDTH = D_MODEL // 2
A_HEADS = A_WIDTH // A_HEAD_DIM
HGRN_CHUNK = 64
B_WIDTH = D_MODEL // 2
B_GROUP_DIM = 128
B_GROUPS = B_WIDTH // B_GROUP_DIM
B_CHUNK = 128
IN0_COLS = 5 * A_WIDTH + 2 * B_WIDTH
MIX0_WIDTH = A_WIDTH + B_WIDTH
C_HEAD_DIM = 128
C_Q_HEADS = D_MODEL // C_HEAD_DIM
C_KV_HEADS = C_Q_HEADS // 4
C_GROUP = C_Q_HEADS // C_KV_HEADS
C_Q_BLOCK = 128
ROPE_THETA = 10000.0
IN1_COLS = (C_Q_HEADS + 2 * C_KV_HEADS) * C_HEAD_DIM
MIX1_WIDTH = C_Q_HEADS * C_HEAD_DIM
D_FF = 5632
CONV_W = 3
N_EVEN = (DEPTH + 1) // 2
N_ODD = DEPTH // 2
ALPHA = (2.0 * DEPTH) ** 0.25
BETA = (8.0 * DEPTH) ** -0.25
LN_EPS = 1e-5
RMS_EPS = 1e-6

kernel_name = 'hybrid_hgrn2_gmlp_axialgqa_convffn_deepnorm'


def layer_norm(x, g, b):
    xf = x.astype(jnp.float32)
    mu = jnp.mean(xf, axis=-1, keepdims=True)
    var = jnp.mean(jnp.square(xf - mu), axis=-1, keepdims=True)
    return ((xf - mu) * lax.rsqrt(var + LN_EPS) * g.astype(jnp.float32) + b.astype(jnp.float32)).astype(x.dtype)


def rms_norm(x, g):
    xf = x.astype(jnp.float32)
    y = xf * lax.rsqrt(jnp.mean(xf * xf, axis=-1, keepdims=True) + RMS_EPS)
    return (y * g.astype(jnp.float32)).astype(x.dtype)


def hgrn2_scan(q, k, v, log_f):
    Bn, H, T, DK = q.shape
    DV = v.shape[-1]
    C = HGRN_CHUNK
    N = T // C
    q, k, log_f = [a.astype(jnp.float32).reshape(Bn, H, N, C, DK) for a in (q, k, log_f)]
    v = v.astype(jnp.float32).reshape(Bn, H, N, C, DV)
    b = jnp.cumsum(log_f, axis=3)
    b_last = b[:, :, :, C - 1:C, :]
    b_mid = b[:, :, :, C // 2 - 1:C // 2, :]
    scores = jnp.einsum('bhntd,bhnsd->bhnts', q * jnp.exp(b - b_mid), k * jnp.exp(b_mid - b))
    mask = jnp.tril(jnp.ones((C, C), dtype=bool))
    scores = jnp.where(mask, scores, 0.0)
    o_intra = jnp.einsum('bhnts,bhnsv->bhntv', scores, v)
    kv = jnp.einsum('bhnsd,bhnsv->bhndv', k * jnp.exp(b_last - b), v)
    decay = jnp.exp(b_last[:, :, :, 0, :])

    def step(S, inp):
        kv_n, dec_n = inp
        return dec_n[..., None] * S + kv_n, S

    S0 = jnp.zeros((Bn, H, DK, DV), jnp.float32)
    _, S_prev = lax.scan(step, S0, (jnp.moveaxis(kv, 2, 0), jnp.moveaxis(decay, 2, 0)))
    S_prev = jnp.moveaxis(S_prev, 0, 2)
    o_inter = jnp.einsum('bhntd,bhndv->bhntv', q * jnp.exp(b), S_prev)
    return (o_intra + o_inter).reshape(Bn, H, T, DV)


def hgrn2_bidirectional(q, i_in, f_fw_pre, f_bw_pre, lb):
    Bn, T, _ = q.shape

    def heads(a):
        return a.reshape(Bn, T, A_HEADS, A_HEAD_DIM).transpose(0, 2, 1, 3)

    def gate(pre):
        return lb + (1.0 - lb) * jax.nn.sigmoid(pre.astype(jnp.float32))

    f_fw = gate(f_fw_pre)
    f_bw = gate(f_bw_pre)
    qh, ih = heads(q), heads(i_in)
    o_fw = hgrn2_scan(qh, heads(1.0 - f_fw), ih, heads(jnp.log(f_fw)))

    def flip(a):
        return jnp.flip(a, axis=2)

    o_bw = flip(hgrn2_scan(flip(qh), flip(heads(1.0 - f_bw)), flip(ih), flip(heads(jnp.log(f_bw)))))
    return (o_fw + o_bw).transpose(0, 2, 1, 3)


def even_mixer(x, w_in, lb, a_norm_g, b_ln_g, b_ln_b, b_ws, b_bias, w_out):
    Bn, T, _ = x.shape
    h = x @ w_in
    cuts = [A_WIDTH, 2 * A_WIDTH, 3 * A_WIDTH, 4 * A_WIDTH, 5 * A_WIDTH, 5 * A_WIDTH + B_WIDTH]
    q, f_fw, f_bw, i_in, g, u, v = jnp.split(h, cuts, axis=-1)
    o = hgrn2_bidirectional(q, i_in, f_fw, f_bw, lb).astype(x.dtype)
    o = rms_norm(o, a_norm_g.reshape(A_HEADS, A_HEAD_DIM)).reshape(Bn, T, A_WIDTH)
    o_a = o * jax.nn.silu(g)
    v = layer_norm(v, b_ln_g, b_ln_b)
    N = T // B_CHUNK
    v = v.reshape(Bn, N, B_CHUNK, B_GROUPS, B_GROUP_DIM)
    s = jnp.einsum('gts,bnsgc->bntgc', b_ws, v) + b_bias.T[None, None, :, :, None]
    o_b = (u.reshape(Bn, N, B_CHUNK, B_GROUPS, B_GROUP_DIM) * s).reshape(Bn, T, B_WIDTH)
    return jnp.concatenate([o_a, o_b], axis=-1) @ w_out


def axial_rope_angles(T):
    rows = T // GRID_W
    r = jnp.repeat(jnp.arange(rows, dtype=jnp.float32), GRID_W)
    c = jnp.tile(jnp.arange(GRID_W, dtype=jnp.float32), rows)
    half = C_HEAD_DIM // 2
    inv_freq = jnp.exp(-math.log(ROPE_THETA) * jnp.arange(0, half, 2, dtype=jnp.float32) / half)
    return r[:, None] * inv_freq, c[:, None] * inv_freq


def rope_rotate(x, ang):
    n2 = x.shape[-1] // 2
    cos = jnp.cos(ang)[:, None, :].astype(x.dtype)
    sin = jnp.sin(ang)[:, None, :].astype(x.dtype)
    x1, x2 = x[..., :n2], x[..., n2:]
    return jnp.concatenate([x1 * cos - x2 * sin, x2 * cos + x1 * sin], axis=-1)


def apply_axial_rope(x, ang_r, ang_c):
    half = C_HEAD_DIM // 2
    return jnp.concatenate([rope_rotate(x[..., :half], ang_r), rope_rotate(x[..., half:], ang_c)], axis=-1)


def odd_mixer(x, w_in, q_g, k_g, w_out):
    Bn, T, _ = x.shape
    h = x @ w_in
    qd, kd = C_Q_HEADS * C_HEAD_DIM, C_KV_HEADS * C_HEAD_DIM
    q, k, v = jnp.split(h, [qd, qd + kd], axis=-1)
    q = q.reshape(Bn, T, C_Q_HEADS, C_HEAD_DIM)
    k = k.reshape(Bn, T, C_KV_HEADS, C_HEAD_DIM)
    v = v.reshape(Bn, T, C_KV_HEADS, C_HEAD_DIM)
    ang_r, ang_c = axial_rope_angles(T)
    q = apply_axial_rope(rms_norm(q, q_g), ang_r, ang_c)
    k = apply_axial_rope(rms_norm(k, k_g), ang_r, ang_c)
    kk = k.transpose(0, 2, 1, 3)
    vv = v.transpose(0, 2, 1, 3)
    NB = T // C_Q_BLOCK
    qb = q.reshape(Bn, T, C_KV_HEADS, C_GROUP, C_HEAD_DIM).transpose(0, 2, 3, 1, 4)
    qb = jnp.moveaxis(qb.reshape(Bn, C_KV_HEADS, C_GROUP, NB, C_Q_BLOCK, C_HEAD_DIM), 3, 0)
    scale = C_HEAD_DIM ** -0.5

    def attend(q_blk):
        s = jnp.einsum('bkgqd,bksd->bkgqs', q_blk, kk).astype(jnp.float32) * scale
        p = jax.nn.softmax(s, axis=-1)
        return jnp.einsum('bkgqs,bksd->bkgqd', p.astype(vv.dtype), vv)

    out = lax.map(attend, qb)
    out = out.transpose(1, 0, 4, 2, 3, 5).reshape(Bn, T, MIX1_WIDTH)
    return out @ w_out


def conv_ffn(x, w_up, conv_w, conv_b, w_down):
    T = x.shape[1]
    h = x @ w_up
    pad = CONV_W // 2
    hp = jnp.pad(h, ((0, 0), (pad, pad), (0, 0)))
    h = sum(hp[:, j:j + T, :] * conv_w[j] for j in range(CONV_W)) + conv_b
    gate, val = jnp.split(h, 2, axis=-1)
    return (jax.nn.silu(gate) * val) @ w_down


def setup_inputs(seed: int = 0) -> dict:
    key = jax.random.key(seed)
    ks = jax.random.split(key, 24)
    f32 = jnp.float32

    def nrm(k, shape, scale):
        return jax.random.normal(k, shape, f32) * scale

    return {
        'x': nrm(ks[0], (BATCH, SEQ, D_MODEL), 1.0),
        'w_in_ab': nrm(ks[1], (N_EVEN, D_MODEL, IN0_COLS), D_MODEL ** -0.5),
        'hgrn_lb_table': nrm(ks[2], (DEPTH + 1, A_WIDTH), 0.1),
        'hgrn_norm_g': 1.0 + nrm(ks[3], (N_EVEN, A_WIDTH), 0.02),
        'gmlp_ln_g': 1.0 + nrm(ks[4], (N_EVEN, B_WIDTH), 0.02),
        'gmlp_ln_b': nrm(ks[5], (N_EVEN, B_WIDTH), 0.02),
        'gmlp_ws': nrm(ks[6], (N_EVEN, B_GROUPS, B_CHUNK, B_CHUNK), B_CHUNK ** -0.5),
        'gmlp_bias': 1.0 + nrm(ks[7], (N_EVEN, B_GROUPS, B_CHUNK), 0.02),
        'w_out_ab': nrm(ks[8], (N_EVEN, MIX0_WIDTH, D_MODEL), BETA * MIX0_WIDTH ** -0.5),
        'w_in_attn': nrm(ks[9], (N_ODD, D_MODEL, IN1_COLS), D_MODEL ** -0.5),
        'q_norm_g': 1.0 + nrm(ks[10], (N_ODD, C_HEAD_DIM), 0.02),
        'k_norm_g': 1.0 + nrm(ks[11], (N_ODD, C_HEAD_DIM), 0.02),
        'w_out_attn': nrm(ks[12], (N_ODD, MIX1_WIDTH, D_MODEL), BETA * MIX1_WIDTH ** -0.5),
        'ffn_up': nrm(ks[13], (DEPTH, D_MODEL, 2 * D_FF), D_MODEL ** -0.5),
        'ffn_conv_w': nrm(ks[14], (DEPTH, CONV_W, 2 * D_FF), CONV_W ** -0.5),
        'ffn_conv_b': nrm(ks[15], (DEPTH, 2 * D_FF), 0.02),
        'ffn_down': nrm(ks[16], (DEPTH, D_FF, D_MODEL), BETA * D_FF ** -0.5),
        'ln1_g': 1.0 + nrm(ks[17], (DEPTH, D_MODEL), 0.02),
        'ln1_b': nrm(ks[18], (DEPTH, D_MODEL), 0.02),
        'ln2_g': 1.0 + nrm(ks[19], (DEPTH, D_MODEL), 0.02),
        'ln2_b': nrm(ks[20], (DEPTH, D_MODEL), 0.02),
    }


def reference(x, w_in_ab, hgrn_lb_table, hgrn_norm_g, gmlp_ln_g, gmlp_ln_b, gmlp_ws, gmlp_bias,
              w_out_ab, w_in_attn, q_norm_g, k_norm_g, w_out_attn, ffn_up, ffn_conv_w, ffn_conv_b,
              ffn_down, ln1_g, ln1_b, ln2_g, ln2_b):
    lb_all = jnp.cumsum(jax.nn.softmax(hgrn_lb_table.astype(jnp.float32), axis=0), axis=0)
    for layer in range(DEPTH):
        j = layer // 2
        if layer % 2 == 0:
            mix = even_mixer(x, w_in_ab[j], lb_all[layer], hgrn_norm_g[j], gmlp_ln_g[j], gmlp_ln_b[j],
                             gmlp_ws[j], gmlp_bias[j], w_out_ab[j])
        else:
            mix = odd_mixer(x, w_in_attn[j], q_norm_g[j], k_norm_g[j], w_out_attn[j])
        x = layer_norm(ALPHA * x + mix, ln1_g[layer], ln1_b[layer])
        x = layer_norm(ALPHA * x + conv_ffn(x, ffn_up[layer], ffn_conv_w[layer], ffn_conv_b[layer], ffn_down[layer]),
                       ln2_g[layer], ln2_b[layer])
    return x
```

```python
import functools
import math

import jax
import jax.numpy as jnp
from jax import lax
from jax.experimental import pallas as pl
from jax.experimental.pallas import tpu as pltpu

F32 = jnp.float32
BF16 = jnp.bfloat16

D_MODEL = 2048
GRID_W = 64
A_HEAD_DIM = 128
A_WIDTH = D_MODEL // 2
A_HEADS = A_WIDTH // A_HEAD_DIM
HGRN_CHUNK = 64
B_WIDTH = D_MODEL // 2
B_GROUP_DIM = 128
B_GROUPS = B_WIDTH // B_GROUP_DIM
B_CHUNK = 128
C_HEAD_DIM = 128
C_Q_HEADS = D_MODEL // C_HEAD_DIM
C_KV_HEADS = C_Q_HEADS // 4
C_GROUP = C_Q_HEADS // C_KV_HEADS
ROPE_THETA = 10000.0
D_FF = 5632
DEPTH = 2
ALPHA = (2.0 * DEPTH) ** 0.25
LN_EPS = 1e-5
RMS_EPS = 1e-6

V7X_VMEM_LIMIT_BYTES = 56 * 1024 * 1024
BF16_SUBLANES = 16

NT_DIMS = (((1,), (1,)), ((), ()))


def _cparams(*sem):
    return pltpu.CompilerParams(dimension_semantics=sem, vmem_limit_bytes=V7X_VMEM_LIMIT_BYTES)


def _layer_norm(y, g, b):
    mu = jnp.mean(y, axis=-1, keepdims=True)
    yc = y - mu
    var = jnp.mean(yc * yc, axis=-1, keepdims=True)
    return yc * lax.rsqrt(var + LN_EPS) * g + b


def _mm_kernel(x_ref, w_ref, o_ref):
    o_ref[...] = jnp.dot(x_ref[...], w_ref[...], preferred_element_type=F32).astype(o_ref.dtype)


def _matmul(x, w, tm, tn, out_dtype, name):
    M, K = x.shape
    N = w.shape[1]
    return pl.pallas_call(
        _mm_kernel,
        grid=(M // tm, N // tn),
        in_specs=[pl.BlockSpec((tm, K), lambda i, j: (i, 0)),
                  pl.BlockSpec((K, tn), lambda i, j: (0, j))],
        out_specs=pl.BlockSpec((tm, tn), lambda i, j: (i, j)),
        out_shape=jax.ShapeDtypeStruct((M, N), out_dtype),
        compiler_params=_cparams("parallel", "parallel"),
        name=name,
    )(x, w)


def _cumsum_rows(tri, x):
    hi = x.astype(BF16)
    r1 = x - hi.astype(F32)
    mid = r1.astype(BF16)
    lo = (r1 - mid.astype(F32)).astype(BF16)
    return (jnp.dot(tri, hi, preferred_element_type=F32)
            + jnp.dot(tri, mid, preferred_element_type=F32)
            + jnp.dot(tri, lo, preferred_element_type=F32))


def _hgrn_kernel(tab_ref, q_ref, ff_ref, fb_ref, i_ref, g_ref, ng_ref, o_ref, ofw_ref, obw_ref,
                 *, layer, seq):
    C = HGRN_CHUNK
    n_chunks = seq // C
    tab = tab_ref[...]
    e = jnp.exp(tab - jnp.max(tab, axis=0, keepdims=True))
    sm = e / jnp.sum(e, axis=0, keepdims=True)
    lb = jnp.sum(sm[:layer + 1], axis=0, keepdims=True)

    row = lax.broadcasted_iota(jnp.int32, (C, C), 0)
    col = lax.broadcasted_iota(jnp.int32, (C, C), 1)
    lower = col <= row
    upper = col >= row
    tril = lower.astype(BF16)
    triu = upper.astype(BF16)

    def one_chunk(start, f_ref, tri, mask, mid_row, last_row, st, out_ref):
        rows = pl.ds(start, C)
        f = lb + (1.0 - lb) * jax.nn.sigmoid(f_ref[rows, :])
        k = 1.0 - f
        b = _cumsum_rows(tri, jnp.log(f))
        b_mid = b[mid_row:mid_row + 1]
        b_last = b[last_row:last_row + 1]
        q = q_ref[rows, :]
        v = i_ref[rows, :]
        qe = (q * jnp.exp(b - b_mid)).astype(BF16)
        ke = (k * jnp.exp(b_mid - b)).astype(BF16)
        sc = lax.dot_general(qe, ke, NT_DIMS, preferred_element_type=F32)
        sc = jnp.where(mask, sc, 0.0)
        vb = v.astype(BF16)
        o_intra = jnp.dot(sc.astype(BF16), vb, preferred_element_type=F32)
        qb = (q * jnp.exp(b)).astype(BF16)
        o_inter = lax.dot_general(qb, st.astype(BF16), NT_DIMS, preferred_element_type=F32)
        out_ref[rows, :] = o_intra + o_inter
        k2 = (k * jnp.exp(b_last - b)).astype(BF16)
        kv_t = jnp.dot(v.T.astype(BF16), k2, preferred_element_type=F32)
        return st * jnp.exp(b_last) + kv_t

    def body(n, carry):
        st_f, st_b = carry
        s_f = pl.multiple_of(n * C, C)
        s_b = pl.multiple_of((n_chunks - 1 - n) * C, C)
        st_f = one_chunk(s_f, ff_ref, tril, lower, C // 2 - 1, C - 1, st_f, ofw_ref)
        st_b = one_chunk(s_b, fb_ref, triu, upper, C // 2, 0, st_b, obw_ref)
        return st_f, st_b

    zero = jnp.zeros((A_HEAD_DIM, A_HEAD_DIM), F32)
    lax.fori_loop(0, n_chunks, body, (zero, zero))

    ng = ng_ref[...]
    R = 256

    def fin(r, _):
        rows = pl.ds(pl.multiple_of(r * R, R), R)
        o = ofw_ref[rows, :] + obw_ref[rows, :]
        y = o * lax.rsqrt(jnp.mean(o * o, axis=-1, keepdims=True) + RMS_EPS) * ng
        g = g_ref[rows, :]
        o_ref[rows, :] = (y * (g * jax.nn.sigmoid(g))).astype(o_ref.dtype)
        return 0

    lax.fori_loop(0, seq // R, fin, 0)


def _hgrn(h0, lb_table, norm_g, layer, batch, seq):
    hd = A_HEAD_DIM

    def col(part):
        return pl.BlockSpec((None, seq, hd), lambda b, h: (b, 0, part * A_HEADS + h))

    n_tab = lb_table.shape[0]
    return pl.pallas_call(
        functools.partial(_hgrn_kernel, layer=layer, seq=seq),
        grid=(batch, A_HEADS),
        in_specs=[pl.BlockSpec((n_tab, hd), lambda b, h: (0, h)),
                  col(0), col(1), col(2), col(3), col(4),
                  pl.BlockSpec((1, hd), lambda b, h: (0, h))],
        out_specs=pl.BlockSpec((None, seq, hd), lambda b, h: (b, 0, h)),
        out_shape=jax.ShapeDtypeStruct((batch, seq, A_WIDTH), BF16),
        scratch_shapes=[pltpu.VMEM((seq, hd), F32), pltpu.VMEM((seq, hd), F32)],
        compiler_params=_cparams("parallel", "parallel"),
        name="hgrn2",
    )(lb_table, h0, h0, h0, h0, h0, norm_g)


def _gmlp_kernel(u_ref, v_ref, lg_ref, lbias_ref, ws_ref, bias_ref, o_ref, *, rows):
    vln = _layer_norm(v_ref[...], lg_ref[...], lbias_ref[...]).astype(BF16)
    for c in range(rows // B_CHUNK):
        r = slice(c * B_CHUNK, (c + 1) * B_CHUNK)
        for g in range(B_GROUPS):
            cs = slice(g * B_GROUP_DIM, (g + 1) * B_GROUP_DIM)
            s = jnp.dot(ws_ref[g], vln[r, cs], preferred_element_type=F32) + bias_ref[g]
            o_ref[r, cs] = (u_ref[r, cs] * s).astype(o_ref.dtype)


def _gmlp(h0, ln_g, ln_b, ws, bias, rows):
    M = h0.shape[0]
    u_blk = 5 * A_WIDTH // B_WIDTH
    bias_b = jnp.broadcast_to(bias[:, :, None], (B_GROUPS, B_CHUNK, B_GROUP_DIM))
    return pl.pallas_call(
        functools.partial(_gmlp_kernel, rows=rows),
        grid=(M // rows,),
        in_specs=[pl.BlockSpec((rows, B_WIDTH), lambda i: (i, u_blk)),
                  pl.BlockSpec((rows, B_WIDTH), lambda i: (i, u_blk + 1)),
                  pl.BlockSpec((1, B_WIDTH), lambda i: (0, 0)),
                  pl.BlockSpec((1, B_WIDTH), lambda i: (0, 0)),
                  pl.BlockSpec((B_GROUPS, B_CHUNK, B_CHUNK), lambda i: (0, 0, 0)),
                  pl.BlockSpec((B_GROUPS, B_CHUNK, B_GROUP_DIM), lambda i: (0, 0, 0))],
        out_specs=pl.BlockSpec((rows, B_WIDTH), lambda i: (i, 0)),
        out_shape=jax.ShapeDtypeStruct((M, B_WIDTH), BF16),
        compiler_params=_cparams("parallel"),
        name="gmlp",
    )(h0, h0, ln_g.reshape(1, -1), ln_b.reshape(1, -1), ws.astype(BF16), bias_b)


def _outproj_ln_kernel(*refs, n_in):
    ins = refs[:n_in]
    w_ref, x_ref, g_ref, b_ref, of_ref, ob_ref = refs[n_in:]
    acc = None
    off = 0
    for r in ins:
        kk = r.shape[1]
        d = jnp.dot(r[...], w_ref[off:off + kk, :], preferred_element_type=F32)
        acc = d if acc is None else acc + d
        off += kk
    out = _layer_norm(ALPHA * x_ref[...] + acc, g_ref[...], b_ref[...])
    of_ref[...] = out
    ob_ref[...] = out.astype(BF16)


def _outproj_ln(parts, w, x, g, b, tm):
    M, D = x.shape
    K = w.shape[0]
    in_specs = [pl.BlockSpec((tm, p.shape[1]), lambda i: (i, 0)) for p in parts]
    in_specs += [pl.BlockSpec((K, D), lambda i: (0, 0)),
                 pl.BlockSpec((tm, D), lambda i: (i, 0)),
                 pl.BlockSpec((1, D), lambda i: (0, 0)),
                 pl.BlockSpec((1, D), lambda i: (0, 0))]
    return pl.pallas_call(
        functools.partial(_outproj_ln_kernel, n_in=len(parts)),
        grid=(M // tm,),
        in_specs=in_specs,
        out_specs=[pl.BlockSpec((tm, D), lambda i: (i, 0)), pl.BlockSpec((tm, D), lambda i: (i, 0))],
        out_shape=[jax.ShapeDtypeStruct((M, D), F32), jax.ShapeDtypeStruct((M, D), BF16)],
        compiler_params=_cparams("parallel"),
        name="outproj_ln",
    )(*parts, w, x, g.reshape(1, -1), b.reshape(1, -1))


def _ffn_up_kernel(xm_ref, xp_ref, xn_ref, wg_ref, wv_ref, cwg_ref, cwv_ref, cbg_ref, cbv_ref,
                   o_ref, lhs_ref, *, tm, seq):
    H = BF16_SUBLANES
    i = pl.program_id(0)

    @pl.when(pl.program_id(1) == 0)
    def _():
        t0 = (i * tm) % seq
        zero = jnp.zeros((H, xm_ref.shape[1]), BF16)
        lhs_ref[0:H, :] = jnp.where(t0 != 0, xp_ref[...], zero)
        lhs_ref[H:H + tm, :] = xm_ref[...]
        lhs_ref[H + tm:2 * H + tm, :] = jnp.where(t0 + tm != seq, xn_ref[...], zero)

    lhs = lhs_ref[...]

    def conv(w_ref, cw_ref, cb_ref):
        h = jnp.dot(lhs, w_ref[...], preferred_element_type=F32)
        cw = cw_ref[...]
        return (h[H - 1:H - 1 + tm] * cw[0:1] + h[H:H + tm] * cw[1:2]
                + h[H + 1:H + 1 + tm] * cw[2:3] + cb_ref[...])

    gate = conv(wg_ref, cwg_ref, cbg_ref)
    val = conv(wv_ref, cwv_ref, cbv_ref)
    o_ref[...] = (gate * jax.nn.sigmoid(gate) * val).astype(o_ref.dtype)


def _ffn_up(xb, w_up, conv_w, conv_b, seq, tm, tn):
    M, D = xb.shape
    H = BF16_SUBLANES
    nj = D_FF // tn
    hb = tm // H
    n_hblk = M // H
    conv_b = conv_b.reshape(1, -1)
    return pl.pallas_call(
        functools.partial(_ffn_up_kernel, tm=tm, seq=seq),
        grid=(M // tm, nj),
        in_specs=[pl.BlockSpec((tm, D), lambda i, j: (i, 0)),
                  pl.BlockSpec((H, D), lambda i, j: (jnp.maximum(i * hb - 1, 0), 0)),
                  pl.BlockSpec((H, D), lambda i, j: (jnp.minimum((i + 1) * hb, n_hblk - 1), 0)),
                  pl.BlockSpec((D, tn), lambda i, j: (0, j)),
                  pl.BlockSpec((D, tn), lambda i, j: (0, j + nj)),
                  pl.BlockSpec((3, tn), lambda i, j: (0, j)),
                  pl.BlockSpec((3, tn), lambda i, j: (0, j + nj)),
                  pl.BlockSpec((1, tn), lambda i, j: (0, j)),
                  pl.BlockSpec((1, tn), lambda i, j: (0, j + nj))],
        out_specs=pl.BlockSpec((tm, tn), lambda i, j: (i, j)),
        out_shape=jax.ShapeDtypeStruct((M, D_FF), BF16),
        scratch_shapes=[pltpu.VMEM((tm + 2 * H, D), BF16)],
        compiler_params=_cparams("parallel", "arbitrary"),
        name="ffn_up_conv",
    )(xb, xb, xb, w_up, w_up, conv_w, conv_w, conv_b, conv_b)


def _ffn_down_ln_kernel(a_ref, w_ref, x_ref, g_ref, b_ref, of_ref, ob_ref, acc_ref):
    kk = pl.program_id(1)

    @pl.when(kk == 0)
    def _():
        acc_ref[...] = jnp.zeros_like(acc_ref)

    acc_ref[...] += jnp.dot(a_ref[...], w_ref[...], preferred_element_type=F32)

    @pl.when(kk == pl.num_programs(1) - 1)
    def _():
        out = _layer_norm(ALPHA * x_ref[...] + acc_ref[...], g_ref[...], b_ref[...])
        of_ref[...] = out
        ob_ref[...] = out.astype(BF16)


def _ffn_down_ln(act, w, x, g, b, tm, tk):
    M, D = x.shape
    K = act.shape[1]
    return pl.pallas_call(
        _ffn_down_ln_kernel,
        grid=(M // tm, K // tk),
        in_specs=[pl.BlockSpec((tm, tk), lambda i, k: (i, k)),
                  pl.BlockSpec((tk, D), lambda i, k: (k, 0)),
                  pl.BlockSpec((tm, D), lambda i, k: (i, 0)),
                  pl.BlockSpec((1, D), lambda i, k: (0, 0)),
                  pl.BlockSpec((1, D), lambda i, k: (0, 0))],
        out_specs=[pl.BlockSpec((tm, D), lambda i, k: (i, 0)), pl.BlockSpec((tm, D), lambda i, k: (i, 0))],
        out_shape=[jax.ShapeDtypeStruct((M, D), F32), jax.ShapeDtypeStruct((M, D), BF16)],
        scratch_shapes=[pltpu.VMEM((tm, D), F32)],
        compiler_params=_cparams("parallel", "arbitrary"),
        name="ffn_down_ln",
    )(act, w, x, g.reshape(1, -1), b.reshape(1, -1))


def _rope_tables(seq):
    t = jnp.arange(seq, dtype=jnp.int32)
    r = (t // GRID_W).astype(F32)
    c = (t % GRID_W).astype(F32)
    half = C_HEAD_DIM // 2
    inv_freq = jnp.exp(-math.log(ROPE_THETA) * jnp.arange(0, half, 2, dtype=F32) / half)
    ang_r = r[:, None] * inv_freq
    ang_c = c[:, None] * inv_freq
    ang = jnp.concatenate([ang_r, ang_r, ang_c, ang_c], axis=-1)
    sign = jnp.tile(jnp.concatenate([-jnp.ones(half // 2, F32), jnp.ones(half // 2, F32)]), 2)
    return jnp.cos(ang), jnp.sin(ang) * sign


def _qkv_kernel(x_ref, w_ref, gain_ref, cos_ref, sin_ref, o_ref, *, n_norm_tiles):
    acc = jnp.dot(x_ref[...], w_ref[...], preferred_element_type=F32)
    hd = C_HEAD_DIM
    j = pl.program_id(1)

    @pl.when(j < n_norm_tiles)
    def _():
        cos = cos_ref[...]
        sin = sin_ref[...]
        lane = lax.broadcasted_iota(jnp.int32, (1, hd), 1)
        low = (lane % (hd // 2)) < (hd // 4)
        for h in range(acc.shape[1] // hd):
            cs = slice(h * hd, (h + 1) * hd)
            a = acc[:, cs]
            y = a * lax.rsqrt(jnp.mean(a * a, axis=-1, keepdims=True) + RMS_EPS) * gain_ref[:, cs]
            partner = jnp.where(low, pltpu.roll(y, hd - hd // 4, 1), pltpu.roll(y, hd // 4, 1))
            o_ref[:, cs] = (y * cos + partner * sin).astype(o_ref.dtype)

    @pl.when(j >= n_norm_tiles)
    def _():
        o_ref[...] = acc.astype(o_ref.dtype)


def _qkv_proj(xb, w, gain, cos, sin, seq, tm, tn):
    M, D = xb.shape
    N = w.shape[1]
    n_norm_tiles = gain.shape[1] // tn
    n_pos_blk = seq // tm
    return pl.pallas_call(
        functools.partial(_qkv_kernel, n_norm_tiles=n_norm_tiles),
        grid=(M // tm, N // tn),
        in_specs=[pl.BlockSpec((tm, D), lambda i, j: (i, 0)),
                  pl.BlockSpec((D, tn), lambda i, j: (0, j)),
                  pl.BlockSpec((1, tn), lambda i, j: (0, jnp.minimum(j, n_norm_tiles - 1))),
                  pl.BlockSpec((tm, C_HEAD_DIM), lambda i, j: (i % n_pos_blk, 0)),
                  pl.BlockSpec((tm, C_HEAD_DIM), lambda i, j: (i % n_pos_blk, 0))],
        out_specs=pl.BlockSpec((tm, tn), lambda i, j: (i, j)),
        out_shape=jax.ShapeDtypeStruct((M, N), BF16),
        compiler_params=_cparams("parallel", "parallel"),
        name="qkv_norm_rope",
    )(xb, w, gain, cos, sin)


def _attn_kernel(q_ref, k_ref, v_ref, o_ref, *, tq):
    hd = C_HEAD_DIM
    q = jnp.concatenate([q_ref[:, g * hd:(g + 1) * hd] for g in range(C_GROUP)], axis=0)
    s = lax.dot_general(q, k_ref[...], NT_DIMS, preferred_element_type=F32)
    p = jnp.exp(s - jnp.max(s, axis=-1, keepdims=True))
    l = jnp.sum(p, axis=-1, keepdims=True)
    o = jnp.dot(p.astype(BF16), v_ref[...], preferred_element_type=F32) * (1.0 / l)
    for g in range(C_GROUP):
        o_ref[:, g * hd:(g + 1) * hd] = o[g * tq:(g + 1) * tq].astype(o_ref.dtype)


def _attention(qkv, batch, seq, tq):
    hd = C_HEAD_DIM
    gw = C_GROUP * hd
    return pl.pallas_call(
        functools.partial(_attn_kernel, tq=tq),
        grid=(batch, C_KV_HEADS, seq // tq),
        in_specs=[pl.BlockSpec((None, tq, gw), lambda b, h, i: (b, i, h)),
                  pl.BlockSpec((None, seq, hd), lambda b, h, i: (b, 0, C_Q_HEADS + h)),
                  pl.BlockSpec((None, seq, hd), lambda b, h, i: (b, 0, C_Q_HEADS + C_KV_HEADS + h))],
        out_specs=pl.BlockSpec((None, tq, gw), lambda b, h, i: (b, i, h)),
        out_shape=jax.ShapeDtypeStruct((batch, seq, C_Q_HEADS * hd), BF16),
        compiler_params=_cparams("parallel", "parallel", "parallel"),
        name="gqa_attention",
    )(qkv, qkv, qkv)


def _conv_ffn_ln(x, xb, w_up, conv_w, conv_b, w_down, g, b, seq):
    act = _ffn_up(xb, w_up.astype(BF16), conv_w, conv_b, seq, tm=1024, tn=512)
    return _ffn_down_ln(act, w_down.astype(BF16), x, g, b, tm=512, tk=512)


def kernel(x, w_in_ab, hgrn_lb_table, hgrn_norm_g, gmlp_ln_g, gmlp_ln_b, gmlp_ws, gmlp_bias,
           w_out_ab, w_in_attn, q_norm_g, k_norm_g, w_out_attn, ffn_up, ffn_conv_w, ffn_conv_b,
           ffn_down, ln1_g, ln1_b, ln2_g, ln2_b):
    batch, seq, d = x.shape
    M = batch * seq
    x = x.reshape(M, d)
    xb = x.astype(BF16)

    h0 = _matmul(xb, w_in_ab[0].astype(BF16), 1024, 1024, F32, "inproj_ab")
    o_a = _hgrn(h0.reshape(batch, seq, -1), hgrn_lb_table, hgrn_norm_g[0].reshape(1, -1), 0, batch, seq)
    o_b = _gmlp(h0, gmlp_ln_g[0], gmlp_ln_b[0], gmlp_ws[0], gmlp_bias[0], rows=512)
    x, xb = _outproj_ln([o_a.reshape(M, -1), o_b], w_out_ab[0].astype(BF16), x, ln1_g[0], ln1_b[0], tm=512)
    x, xb = _conv_ffn_ln(x, xb, ffn_up[0], ffn_conv_w[0], ffn_conv_b[0], ffn_down[0], ln2_g[0], ln2_b[0], seq)

    scale = C_HEAD_DIM ** -0.5
    gain = jnp.concatenate([jnp.tile(q_norm_g[0] * scale, C_Q_HEADS), jnp.tile(k_norm_g[0], C_KV_HEADS)])
    cos, sin = _rope_tables(seq)
    qkv = _qkv_proj(xb, w_in_attn[0].astype(BF16), gain.reshape(1, -1), cos, sin, seq, tm=1024, tn=512)
    att = _attention(qkv.reshape(batch, seq, -1), batch, seq, tq=128)
    x, xb = _outproj_ln([att.reshape(M, -1)], w_out_attn[0].astype(BF16), x, ln1_g[1], ln1_b[1], tm=512)
    x, _ = _conv_ffn_ln(x, xb, ffn_up[1], ffn_conv_w[1], ffn_conv_b[1], ffn_down[1], ln2_g[1], ln2_b[1], seq)
    return x.reshape(batch, seq, d)
```

```python
import functools
import math

import jax
import jax.numpy as jnp
from jax import lax
from jax.experimental import pallas as pl
from jax.experimental.pallas import tpu as pltpu

F32 = jnp.float32
BF16 = jnp.bfloat16

D_MODEL = 2048
GRID_W = 64
A_HEAD_DIM = 128
A_WIDTH = D_MODEL // 2
A_HEADS = A_WIDTH // A_HEAD_DIM
HGRN_CHUNK = 64
B_WIDTH = D_MODEL // 2
B_GROUP_DIM = 128
B_GROUPS = B_WIDTH // B_GROUP_DIM
B_CHUNK = 128
C_HEAD_DIM = 128
C_Q_HEADS = D_MODEL // C_HEAD_DIM
C_KV_HEADS = C_Q_HEADS // 4
C_GROUP = C_Q_HEADS // C_KV_HEADS
ROPE_THETA = 10000.0
D_FF = 5632
DEPTH = 2
ALPHA = (2.0 * DEPTH) ** 0.25
LN_EPS = 1e-5
RMS_EPS = 1e-6

V7X_VMEM_LIMIT_BYTES = 56 * 1024 * 1024
BF16_SUBLANES = 16

NT_DIMS = (((1,), (1,)), ((), ()))


def _cparams(*sem):
    return pltpu.CompilerParams(dimension_semantics=sem, vmem_limit_bytes=V7X_VMEM_LIMIT_BYTES)


def _layer_norm(y, g, b):
    mu = jnp.mean(y, axis=-1, keepdims=True)
    yc = y - mu
    var = jnp.mean(yc * yc, axis=-1, keepdims=True)
    return yc * lax.rsqrt(var + LN_EPS) * g + b


def _mm_kernel(x_ref, w_ref, o_ref):
    o_ref[...] = jnp.dot(x_ref[...], w_ref[...], preferred_element_type=F32).astype(o_ref.dtype)


def _matmul(x, w, tm, tn, out_dtype, name):
    M, K = x.shape
    N = w.shape[1]
    return pl.pallas_call(
        _mm_kernel,
        grid=(M // tm, N // tn),
        in_specs=[pl.BlockSpec((tm, K), lambda i, j: (i, 0)),
                  pl.BlockSpec((K, tn), lambda i, j: (0, j))],
        out_specs=pl.BlockSpec((tm, tn), lambda i, j: (i, j)),
        out_shape=jax.ShapeDtypeStruct((M, N), out_dtype),
        compiler_params=_cparams("parallel", "parallel"),
        name=name,
    )(x, w)


def _cumsum_rows(tri, x):
    hi = x.astype(BF16)
    r1 = x - hi.astype(F32)
    mid = r1.astype(BF16)
    lo = (r1 - mid.astype(F32)).astype(BF16)
    return (jnp.dot(tri, hi, preferred_element_type=F32)
            + jnp.dot(tri, mid, preferred_element_type=F32)
            + jnp.dot(tri, lo, preferred_element_type=F32))


def _hgrn_kernel(tab_ref, q_ref, ff_ref, fb_ref, i_ref, g_ref, ng_ref, o_ref, ofw_ref, obw_ref,
                 *, layer, seq):
    C = HGRN_CHUNK
    n_chunks = seq // C
    tab = tab_ref[...]
    e = jnp.exp(tab - jnp.max(tab, axis=0, keepdims=True))
    sm = e / jnp.sum(e, axis=0, keepdims=True)
    lb = jnp.sum(sm[:layer + 1], axis=0, keepdims=True)

    row = lax.broadcasted_iota(jnp.int32, (C, C), 0)
    col = lax.broadcasted_iota(jnp.int32, (C, C), 1)
    lower = col <= row
    upper = col >= row
    tril = lower.astype(BF16)
    triu = upper.astype(BF16)

    def one_chunk(start, f_ref, tri, mask, mid_row, last_row, st, out_ref):
        rows = pl.ds(start, C)
        f = lb + (1.0 - lb) * jax.nn.sigmoid(f_ref[rows, :])
        k = 1.0 - f
        b = _cumsum_rows(tri, jnp.log(f))
        b_mid = b[mid_row:mid_row + 1]
        b_last = b[last_row:last_row + 1]
        q = q_ref[rows, :]
        v = i_ref[rows, :]
        qe = (q * jnp.exp(b - b_mid)).astype(BF16)
        ke = (k * jnp.exp(b_mid - b)).astype(BF16)
        sc = lax.dot_general(qe, ke, NT_DIMS, preferred_element_type=F32)
        sc = jnp.where(mask, sc, 0.0)
        vb = v.astype(BF16)
        o_intra = jnp.dot(sc.astype(BF16), vb, preferred_element_type=F32)
        qb = (q * jnp.exp(b)).astype(BF16)
        o_inter = lax.dot_general(qb, st.astype(BF16), NT_DIMS, preferred_element_type=F32)
        out_ref[rows, :] = o_intra + o_inter
        k2 = (k * jnp.exp(b_last - b)).astype(BF16)
        kv_t = jnp.dot(v.T.astype(BF16), k2, preferred_element_type=F32)
        return st * jnp.exp(b_last) + kv_t

    def body(n, carry):
        st_f, st_b = carry
        s_f = pl.multiple_of(n * C, C)
        s_b = pl.multiple_of((n_chunks - 1 - n) * C, C)
        st_f = one_chunk(s_f, ff_ref, tril, lower, C // 2 - 1, C - 1, st_f, ofw_ref)
        st_b = one_chunk(s_b, fb_ref, triu, upper, C // 2, 0, st_b, obw_ref)
        return st_f, st_b

    zero = jnp.zeros((A_HEAD_DIM, A_HEAD_DIM), F32)
    lax.fori_loop(0, n_chunks, body, (zero, zero))

    ng = ng_ref[...]
    R = 256

    def fin(r, _):
        rows = pl.ds(pl.multiple_of(r * R, R), R)
        o = ofw_ref[rows, :] + obw_ref[rows, :]
        y = o * lax.rsqrt(jnp.mean(o * o, axis=-1, keepdims=True) + RMS_EPS) * ng
        g = g_ref[rows, :]
        o_ref[rows, :] = (y * (g * jax.nn.sigmoid(g))).astype(o_ref.dtype)
        return 0

    lax.fori_loop(0, seq // R, fin, 0)


def _hgrn(h0, lb_table, norm_g, layer, batch, seq):
    hd = A_HEAD_DIM

    def col(part):
        return pl.BlockSpec((None, seq, hd), lambda b, h: (b, 0, part * A_HEADS + h))

    n_tab = lb_table.shape[0]
    return pl.pallas_call(
        functools.partial(_hgrn_kernel, layer=layer, seq=seq),
        grid=(batch, A_HEADS),
        in_specs=[pl.BlockSpec((n_tab, hd), lambda b, h: (0, h)),
                  col(0), col(1), col(2), col(3), col(4),
                  pl.BlockSpec((1, hd), lambda b, h: (0, h))],
        out_specs=pl.BlockSpec((None, seq, hd), lambda b, h: (b, 0, h)),
        out_shape=jax.ShapeDtypeStruct((batch, seq, A_WIDTH), BF16),
        scratch_shapes=[pltpu.VMEM((seq, hd), F32), pltpu.VMEM((seq, hd), F32)],
        compiler_params=_cparams("parallel", "parallel"),
        name="hgrn2",
    )(lb_table, h0, h0, h0, h0, h0, norm_g)


def _gmlp_kernel(u_ref, v_ref, lg_ref, lbias_ref, ws_ref, bias_ref, o_ref, *, rows):
    vln = _layer_norm(v_ref[...], lg_ref[...], lbias_ref[...]).astype(BF16)
    for c in range(rows // B_CHUNK):
        r = slice(c * B_CHUNK, (c + 1) * B_CHUNK)
        for g in range(B_GROUPS):
            cs = slice(g * B_GROUP_DIM, (g + 1) * B_GROUP_DIM)
            s = jnp.dot(ws_ref[g], vln[r, cs], preferred_element_type=F32) + bias_ref[g]
            o_ref[r, cs] = (u_ref[r, cs] * s).astype(o_ref.dtype)


def _gmlp(h0, ln_g, ln_b, ws, bias, rows):
    M = h0.shape[0]
    u_blk = 5 * A_WIDTH // B_WIDTH
    bias_b = jnp.broadcast_to(bias[:, :, None], (B_GROUPS, B_CHUNK, B_GROUP_DIM))
    return pl.pallas_call(
        functools.partial(_gmlp_kernel, rows=rows),
        grid=(M // rows,),
        in_specs=[pl.BlockSpec((rows, B_WIDTH), lambda i: (i, u_blk)),
                  pl.BlockSpec((rows, B_WIDTH), lambda i: (i, u_blk + 1)),
                  pl.BlockSpec((1, B_WIDTH), lambda i: (0, 0)),
                  pl.BlockSpec((1, B_WIDTH), lambda i: (0, 0)),
                  pl.BlockSpec((B_GROUPS, B_CHUNK, B_CHUNK), lambda i: (0, 0, 0)),
                  pl.BlockSpec((B_GROUPS, B_CHUNK, B_GROUP_DIM), lambda i: (0, 0, 0))],
        out_specs=pl.BlockSpec((rows, B_WIDTH), lambda i: (i, 0)),
        out_shape=jax.ShapeDtypeStruct((M, B_WIDTH), BF16),
        compiler_params=_cparams("parallel"),
        name="gmlp",
    )(h0, h0, ln_g.reshape(1, -1), ln_b.reshape(1, -1), ws.astype(BF16), bias_b)


def _outproj_ln_kernel(*refs, n_in):
    ins = refs[:n_in]
    w_ref, x_ref, g_ref, b_ref, of_ref, ob_ref = refs[n_in:]
    acc = None
    off = 0
    for r in ins:
        kk = r.shape[1]
        d = jnp.dot(r[...], w_ref[off:off + kk, :], preferred_element_type=F32)
        acc = d if acc is None else acc + d
        off += kk
    out = _layer_norm(ALPHA * x_ref[...] + acc, g_ref[...], b_ref[...])
    of_ref[...] = out
    ob_ref[...] = out.astype(BF16)


def _outproj_ln(parts, w, x, g, b, tm):
    M, D = x.shape
    K = w.shape[0]
    in_specs = [pl.BlockSpec((tm, p.shape[1]), lambda i: (i, 0)) for p in parts]
    in_specs += [pl.BlockSpec((K, D), lambda i: (0, 0)),
                 pl.BlockSpec((tm, D), lambda i: (i, 0)),
                 pl.BlockSpec((1, D), lambda i: (0, 0)),
                 pl.BlockSpec((1, D), lambda i: (0, 0))]
    return pl.pallas_call(
        functools.partial(_outproj_ln_kernel, n_in=len(parts)),
        grid=(M // tm,),
        in_specs=in_specs,
        out_specs=[pl.BlockSpec((tm, D), lambda i: (i, 0)), pl.BlockSpec((tm, D), lambda i: (i, 0))],
        out_shape=[jax.ShapeDtypeStruct((M, D), F32), jax.ShapeDtypeStruct((M, D), BF16)],
        compiler_params=_cparams("parallel"),
        name="outproj_ln",
    )(*parts, w, x, g.reshape(1, -1), b.reshape(1, -1))


def _ffn_up_kernel(xm_ref, xp_ref, xn_ref, wg_ref, wv_ref, cwg_ref, cwv_ref, cbg_ref, cbv_ref,
                   o_ref, lhs_ref, *, tm, seq):
    H = BF16_SUBLANES
    i = pl.program_id(0)

    @pl.when(pl.program_id(1) == 0)
    def _():
        t0 = (i * tm) % seq
        zero = jnp.zeros((H, xm_ref.shape[1]), BF16)
        lhs_ref[0:H, :] = jnp.where(t0 != 0, xp_ref[...], zero)
        lhs_ref[H:H + tm, :] = xm_ref[...]
        lhs_ref[H + tm:2 * H + tm, :] = jnp.where(t0 + tm != seq, xn_ref[...], zero)

    lhs = lhs_ref[...]

    def conv(w_ref, cw_ref, cb_ref):
        h = jnp.dot(lhs, w_ref[...], preferred_element_type=F32)
        cw = cw_ref[...]
        return (h[H - 1:H - 1 + tm] * cw[0:1] + h[H:H + tm] * cw[1:2]
                + h[H + 1:H + 1 + tm] * cw[2:3] + cb_ref[...])

    gate = conv(wg_ref, cwg_ref, cbg_ref)
    val = conv(wv_ref, cwv_ref, cbv_ref)
    o_ref[...] = (gate * jax.nn.sigmoid(gate) * val).astype(o_ref.dtype)


def _ffn_up(xb, w_up, conv_w, conv_b, seq, tm, tn):
    M, D = xb.shape
    H = BF16_SUBLANES
    nj = D_FF // tn
    hb = tm // H
    n_hblk = M // H
    conv_b = conv_b.reshape(1, -1)
    return pl.pallas_call(
        functools.partial(_ffn_up_kernel, tm=tm, seq=seq),
        grid=(M // tm, nj),
        in_specs=[pl.BlockSpec((tm, D), lambda i, j: (i, 0)),
                  pl.BlockSpec((H, D), lambda i, j: (jnp.maximum(i * hb - 1, 0), 0)),
                  pl.BlockSpec((H, D), lambda i, j: (jnp.minimum((i + 1) * hb, n_hblk - 1), 0)),
                  pl.BlockSpec((D, tn), lambda i, j: (0, j)),
                  pl.BlockSpec((D, tn), lambda i, j: (0, j + nj)),
                  pl.BlockSpec((3, tn), lambda i, j: (0, j)),
                  pl.BlockSpec((3, tn), lambda i, j: (0, j + nj)),
                  pl.BlockSpec((1, tn), lambda i, j: (0, j)),
                  pl.BlockSpec((1, tn), lambda i, j: (0, j + nj))],
        out_specs=pl.BlockSpec((tm, tn), lambda i, j: (i, j)),
        out_shape=jax.ShapeDtypeStruct((M, D_FF), BF16),
        scratch_shapes=[pltpu.VMEM((tm + 2 * H, D), BF16)],
        compiler_params=_cparams("parallel", "arbitrary"),
        name="ffn_up_conv",
    )(xb, xb, xb, w_up, w_up, conv_w, conv_w, conv_b, conv_b)


def _ffn_down_ln_kernel(a_ref, w_ref, x_ref, g_ref, b_ref, of_ref, ob_ref, acc_ref):
    kk = pl.program_id(1)

    @pl.when(kk == 0)
    def _():
        acc_ref[...] = jnp.zeros_like(acc_ref)

    acc_ref[...] += jnp.dot(a_ref[...], w_ref[...], preferred_element_type=F32)

    @pl.when(kk == pl.num_programs(1) - 1)
    def _():
        out = _layer_norm(ALPHA * x_ref[...] + acc_ref[...], g_ref[...], b_ref[...])
        of_ref[...] = out
        ob_ref[...] = out.astype(BF16)


def _ffn_down_ln(act, w, x, g, b, tm, tk):
    M, D = x.shape
    K = act.shape[1]
    return pl.pallas_call(
        _ffn_down_ln_kernel,
        grid=(M // tm, K // tk),
        in_specs=[pl.BlockSpec((tm, tk), lambda i, k: (i, k)),
                  pl.BlockSpec((tk, D), lambda i, k: (k, 0)),
                  pl.BlockSpec((tm, D), lambda i, k: (i, 0)),
                  pl.BlockSpec((1, D), lambda i, k: (0, 0)),
                  pl.BlockSpec((1, D), lambda i, k: (0, 0))],
        out_specs=[pl.BlockSpec((tm, D), lambda i, k: (i, 0)), pl.BlockSpec((tm, D), lambda i, k: (i, 0))],
        out_shape=[jax.ShapeDtypeStruct((M, D), F32), jax.ShapeDtypeStruct((M, D), BF16)],
        scratch_shapes=[pltpu.VMEM((tm, D), F32)],
        compiler_params=_cparams("parallel", "arbitrary"),
        name="ffn_down_ln",
    )(act, w, x, g.reshape(1, -1), b.reshape(1, -1))


def _rope_tables(seq):
    t = jnp.arange(seq, dtype=jnp.int32)
    r = (t // GRID_W).astype(F32)
    c = (t % GRID_W).astype(F32)
    half = C_HEAD_DIM // 2
    inv_freq = jnp.exp(-math.log(ROPE_THETA) * jnp.arange(0, half, 2, dtype=F32) / half)
    ang_r = r[:, None] * inv_freq
    ang_c = c[:, None] * inv_freq
    ang = jnp.concatenate([ang_r, ang_r, ang_c, ang_c], axis=-1)
    sign = jnp.tile(jnp.concatenate([-jnp.ones(half // 2, F32), jnp.ones(half // 2, F32)]), 2)
    return jnp.cos(ang), jnp.sin(ang) * sign


def _qkv_kernel(x_ref, w_ref, gain_ref, cos_ref, sin_ref, o_ref, *, n_norm_tiles):
    acc = jnp.dot(x_ref[...], w_ref[...], preferred_element_type=F32)
    hd = C_HEAD_DIM
    j = pl.program_id(1)

    @pl.when(j < n_norm_tiles)
    def _():
        cos = cos_ref[...]
        sin = sin_ref[...]
        lane = lax.broadcasted_iota(jnp.int32, (1, hd), 1)
        low = (lane % (hd // 2)) < (hd // 4)
        for h in range(acc.shape[1] // hd):
            cs = slice(h * hd, (h + 1) * hd)
            a = acc[:, cs]
            y = a * lax.rsqrt(jnp.mean(a * a, axis=-1, keepdims=True) + RMS_EPS) * gain_ref[:, cs]
            partner = jnp.where(low, pltpu.roll(y, hd - hd // 4, 1), pltpu.roll(y, hd // 4, 1))
            o_ref[:, cs] = (y * cos + partner * sin).astype(o_ref.dtype)

    @pl.when(j >= n_norm_tiles)
    def _():
        o_ref[...] = acc.astype(o_ref.dtype)


def _qkv_proj(xb, w, gain, cos, sin, seq, tm, tn):
    M, D = xb.shape
    N = w.shape[1]
    n_norm_tiles = gain.shape[1] // tn
    n_pos_blk = seq // tm
    return pl.pallas_call(
        functools.partial(_qkv_kernel, n_norm_tiles=n_norm_tiles),
        grid=(M // tm, N // tn),
        in_specs=[pl.BlockSpec((tm, D), lambda i, j: (i, 0)),
                  pl.BlockSpec((D, tn), lambda i, j: (0, j)),
                  pl.BlockSpec((1, tn), lambda i, j: (0, jnp.minimum(j, n_norm_tiles - 1))),
                  pl.BlockSpec((tm, C_HEAD_DIM), lambda i, j: (i % n_pos_blk, 0)),
                  pl.BlockSpec((tm, C_HEAD_DIM), lambda i, j: (i % n_pos_blk, 0))],
        out_specs=pl.BlockSpec((tm, tn), lambda i, j: (i, j)),
        out_shape=jax.ShapeDtypeStruct((M, N), BF16),
        compiler_params=_cparams("parallel", "parallel"),
        name="qkv_norm_rope",
    )(xb, w, gain, cos, sin)


def _attn_kernel(q_ref, k_ref, v_ref, o_ref, vt_ref, st_ref, m_ref, *, tq, blocks_per_head, n_chunks):
    hd = C_HEAD_DIM
    s = pl.program_id(0)
    seq = k_ref.shape[0]
    ck = seq // n_chunks

    @pl.when(s == 0)
    def _():
        st_ref[1] = jnp.zeros(st_ref.shape[1:], F32)
        m_ref[1] = jnp.zeros(m_ref.shape[1:], F32)

    @pl.when(jnp.maximum(s - 1, 0) % blocks_per_head == 0)
    def _():
        vt_ref[0:hd, :] = v_ref[...].astype(F32).T.astype(BF16)
        vt_ref[hd:, :] = jnp.ones((vt_ref.shape[0] - hd, seq), BF16)

    def step(fill, drain):
        q = jnp.concatenate([q_ref[:, g * hd:(g + 1) * hd] for g in range(C_GROUP)], axis=0)
        m_drain = m_ref[drain]
        m_fill = None
        acc = None
        for c in range(n_chunks):
            rows = slice(c * ck, (c + 1) * ck)
            sc = lax.dot_general(k_ref[rows, :], q, NT_DIMS, preferred_element_type=F32)
            st_ref[fill, rows, :] = sc
            mc = jnp.max(sc, axis=0, keepdims=True)
            m_fill = mc if c == 0 else jnp.maximum(m_fill, mc)
            p = jnp.exp2(st_ref[drain, rows, :] - m_drain).astype(BF16)
            pv = jnp.dot(vt_ref[:, rows], p, preferred_element_type=F32)
            acc = pv if c == 0 else acc + pv
        m_ref[fill] = m_fill
        o = acc[0:hd] * (1.0 / acc[hd:hd + 1])
        for g in range(C_GROUP):
            o_ref[:, g * hd:(g + 1) * hd] = o[:, g * tq:(g + 1) * tq].T.astype(o_ref.dtype)

    @pl.when(s % 2 == 0)
    def _():
        step(0, 1)

    @pl.when(s % 2 == 1)
    def _():
        step(1, 0)


def _attention(qkv, batch, seq, tq, n_chunks):
    hd = C_HEAD_DIM
    gw = C_GROUP * hd
    bph = seq // tq
    n_blocks = batch * C_KV_HEADS * bph

    def coords(blk):
        return blk // (C_KV_HEADS * bph), (blk // bph) % C_KV_HEADS, blk % bph

    def q_map(s):
        b, h, i = coords(jnp.minimum(s, n_blocks - 1))
        return b, i, h

    def k_map(s):
        b, h, _ = coords(jnp.minimum(s, n_blocks - 1))
        return b, 0, C_Q_HEADS + h

    def v_map(s):
        b, h, _ = coords(jnp.maximum(s - 1, 0))
        return b, 0, C_Q_HEADS + C_KV_HEADS + h

    def o_map(s):
        b, h, i = coords(jnp.maximum(s - 1, 0))
        return b, i, h

    return pl.pallas_call(
        functools.partial(_attn_kernel, tq=tq, blocks_per_head=bph, n_chunks=n_chunks),
        grid=(n_blocks + 1,),
        in_specs=[pl.BlockSpec((None, tq, gw), q_map),
                  pl.BlockSpec((None, seq, hd), k_map),
                  pl.BlockSpec((None, seq, hd), v_map)],
        out_specs=pl.BlockSpec((None, tq, gw), o_map),
        out_shape=jax.ShapeDtypeStruct((batch, seq, C_Q_HEADS * hd), BF16),
        scratch_shapes=[pltpu.VMEM((hd + BF16_SUBLANES, seq), BF16),
                        pltpu.VMEM((2, seq, C_GROUP * tq), F32),
                        pltpu.VMEM((2, 1, C_GROUP * tq), F32)],
        compiler_params=_cparams("arbitrary"),
        name="gqa_attention",
    )(qkv, qkv, qkv)


def _conv_ffn_ln(x, xb, w_up, conv_w, conv_b, w_down, g, b, seq):
    act = _ffn_up(xb, w_up.astype(BF16), conv_w, conv_b, seq, tm=1024, tn=512)
    return _ffn_down_ln(act, w_down.astype(BF16), x, g, b, tm=512, tk=512)


def kernel(x, w_in_ab, hgrn_lb_table, hgrn_norm_g, gmlp_ln_g, gmlp_ln_b, gmlp_ws, gmlp_bias,
           w_out_ab, w_in_attn, q_norm_g, k_norm_g, w_out_attn, ffn_up, ffn_conv_w, ffn_conv_b,
           ffn_down, ln1_g, ln1_b, ln2_g, ln2_b):
    batch, seq, d = x.shape
    M = batch * seq
    x = x.reshape(M, d)
    xb = x.astype(BF16)

    h0 = _matmul(xb, w_in_ab[0].astype(BF16), 1024, 1024, F32, "inproj_ab")
    o_a = _hgrn(h0.reshape(batch, seq, -1), hgrn_lb_table, hgrn_norm_g[0].reshape(1, -1), 0, batch, seq)
    o_b = _gmlp(h0, gmlp_ln_g[0], gmlp_ln_b[0], gmlp_ws[0], gmlp_bias[0], rows=512)
    x, xb = _outproj_ln([o_a.reshape(M, -1), o_b], w_out_ab[0].astype(BF16), x, ln1_g[0], ln1_b[0], tm=512)
    x, xb = _conv_ffn_ln(x, xb, ffn_up[0], ffn_conv_w[0], ffn_conv_b[0], ffn_down[0], ln2_g[0], ln2_b[0], seq)

    scale = C_HEAD_DIM ** -0.5 * math.log2(math.e)
    gain = jnp.concatenate([jnp.tile(q_norm_g[0] * scale, C_Q_HEADS), jnp.tile(k_norm_g[0], C_KV_HEADS)])
    cos, sin = _rope_tables(seq)
    qkv = _qkv_proj(xb, w_in_attn[0].astype(BF16), gain.reshape(1, -1), cos, sin, seq, tm=1024, tn=512)
    att = _attention(qkv.reshape(batch, seq, -1), batch, seq, tq=128, n_chunks=8)
    x, xb = _outproj_ln([att.reshape(M, -1)], w_out_attn[0].astype(BF16), x, ln1_g[1], ln1_b[1], tm=512)
    x, _ = _conv_ffn_ln(x, xb, ffn_up[1], ffn_conv_w[1], ffn_conv_b[1], ffn_down[1], ln2_g[1], ln2_b[1], seq)
    return x.reshape(batch, seq, d)
```

```python
import functools
import math

import jax
import jax.numpy as jnp
from jax import lax
from jax.experimental import pallas as pl
from jax.experimental.pallas import tpu as pltpu

F32 = jnp.float32
BF16 = jnp.bfloat16

D_MODEL = 2048
GRID_W = 64
A_HEAD_DIM = 128
A_WIDTH = D_MODEL // 2
A_HEADS = A_WIDTH // A_HEAD_DIM
HGRN_CHUNK = 64
B_WIDTH = D_MODEL // 2
B_GROUP_DIM = 128
B_GROUPS = B_WIDTH // B_GROUP_DIM
B_CHUNK = 128
C_HEAD_DIM = 128
C_Q_HEADS = D_MODEL // C_HEAD_DIM
C_KV_HEADS = C_Q_HEADS // 4
C_GROUP = C_Q_HEADS // C_KV_HEADS
ROPE_THETA = 10000.0
D_FF = 5632
DEPTH = 2
ALPHA = (2.0 * DEPTH) ** 0.25
LN_EPS = 1e-5
RMS_EPS = 1e-6

V7X_VMEM_LIMIT_BYTES = 56 * 1024 * 1024
BF16_SUBLANES = 16

NT_DIMS = (((1,), (1,)), ((), ()))


def _cparams(*sem):
    return pltpu.CompilerParams(dimension_semantics=sem, vmem_limit_bytes=V7X_VMEM_LIMIT_BYTES)


def _layer_norm(y, g, b):
    mu = jnp.mean(y, axis=-1, keepdims=True)
    yc = y - mu
    var = jnp.mean(yc * yc, axis=-1, keepdims=True)
    return yc * lax.rsqrt(var + LN_EPS) * g + b


def _mm_kernel(x_ref, w_ref, o_ref):
    o_ref[...] = jnp.dot(x_ref[...], w_ref[...], preferred_element_type=F32).astype(o_ref.dtype)


def _matmul(x, w, tm, tn, out_dtype, name):
    M, K = x.shape
    N = w.shape[1]
    return pl.pallas_call(
        _mm_kernel,
        grid=(M // tm, N // tn),
        in_specs=[pl.BlockSpec((tm, K), lambda i, j: (i, 0)),
                  pl.BlockSpec((K, tn), lambda i, j: (0, j))],
        out_specs=pl.BlockSpec((tm, tn), lambda i, j: (i, j)),
        out_shape=jax.ShapeDtypeStruct((M, N), out_dtype),
        compiler_params=_cparams("parallel", "parallel"),
        name=name,
    )(x, w)


HGRN_GROUP = 4
HGRN_ROWS = HGRN_GROUP * HGRN_CHUNK


def _hgrn_kernel(tab_ref, q_ref, ff_ref, fb_ref, i_ref, g_ref, ng_ref, o_ref,
                 ofw_ref, obw_ref, k2f_ref, qbf_ref, k2b_ref, qbb_ref, *, layer, seq):
    C, G, R, hd = HGRN_CHUNK, HGRN_GROUP, HGRN_ROWS, A_HEAD_DIM
    n_groups = seq // R
    tab = tab_ref[...]
    e = jnp.exp(tab - jnp.max(tab, axis=0, keepdims=True))
    sm = e / jnp.sum(e, axis=0, keepdims=True)
    lb = jnp.sum(sm[:layer + 1], axis=0, keepdims=True)

    row = lax.broadcasted_iota(jnp.int32, (R, R), 0)
    col = lax.broadcasted_iota(jnp.int32, (R, R), 1)
    same = (row // C) == (col // C)
    lower = same & (col <= row)
    upper = same & (col >= row)
    X = BF16_SUBLANES
    jrow = lax.broadcasted_iota(jnp.int32, (X, R), 0)
    jcol = lax.broadcasted_iota(jnp.int32, (X, R), 1)
    in_chunk = (jcol // C) == jrow
    last_rows = in_chunk.astype(BF16)
    mid_f = (in_chunk & (jcol % C <= C // 2 - 1)).astype(BF16)
    mid_b = (in_chunk & (jcol % C >= C // 2)).astype(BF16)
    cum_f = jnp.concatenate([lower.astype(BF16), mid_f, last_rows], axis=0)
    cum_b = jnp.concatenate([upper.astype(BF16), mid_b, last_rows], axis=0)

    for ref in (k2f_ref, qbf_ref, k2b_ref, qbb_ref):
        ref[...] = jnp.zeros(ref.shape, BF16)

    def per_chunk(extra, j0):
        return jnp.concatenate(
            [jnp.broadcast_to(extra[j0 + c:j0 + c + 1], (C, hd)) for c in range(G)], axis=0)

    def one_group(start, f_ref, cum, mask, st, out_ref, k2_ref, qb_ref, reverse):
        rows = pl.ds(start, R)
        f = lb + (1.0 - lb) * jax.nn.sigmoid(f_ref[rows, :])
        k = 1.0 - f
        lf = jnp.log(f)
        hi = lf.astype(BF16)
        r1 = lf - hi.astype(F32)
        mid = r1.astype(BF16)
        lo = (r1 - mid.astype(F32)).astype(BF16)
        ball = jnp.dot(cum, jnp.concatenate([hi, mid, lo], axis=1), preferred_element_type=F32)
        ball = ball[:, 0:hd] + ball[:, hd:2 * hd] + ball[:, 2 * hd:3 * hd]
        b = ball[0:R]
        b_mid = per_chunk(ball, R)
        b_last = per_chunk(ball, R + X)
        q = q_ref[rows, :]
        v = i_ref[rows, :]
        qe = (q * jnp.exp(b - b_mid)).astype(BF16)
        ke = (k * jnp.exp(b_mid - b)).astype(BF16)
        sc = lax.dot_general(qe, ke, NT_DIMS, preferred_element_type=F32)
        sc = jnp.where(mask, sc, 0.0)
        o = jnp.dot(sc.astype(BF16), v.astype(BF16), preferred_element_type=F32)
        qb = (q * jnp.exp(b)).astype(BF16)
        k2 = (k * jnp.exp(b_last - b)).astype(BF16)
        for c in range(G):
            k2_ref[c * C:(c + 1) * C, c * hd:(c + 1) * hd] = k2[c * C:(c + 1) * C]
            qb_ref[c * C:(c + 1) * C, c * hd:(c + 1) * hd] = qb[c * C:(c + 1) * C]
        kv_t = jnp.dot(v.T.astype(BF16), k2_ref[...], preferred_element_type=F32)
        entering = [None] * G
        for c in (reversed(range(G)) if reverse else range(G)):
            entering[c] = st.astype(BF16)
            st = st * jnp.exp(ball[R + X + c:R + X + c + 1]) + kv_t[:, c * hd:(c + 1) * hd]
        o = o + lax.dot_general(qb_ref[...], jnp.concatenate(entering, axis=1), NT_DIMS,
                                preferred_element_type=F32)
        out_ref[rows, :] = o
        return st

    def body(n, carry):
        st_f, st_b = carry
        s_f = pl.multiple_of(n * R, R)
        s_b = pl.multiple_of((n_groups - 1 - n) * R, R)
        st_f = one_group(s_f, ff_ref, cum_f, lower, st_f, ofw_ref, k2f_ref, qbf_ref, False)
        st_b = one_group(s_b, fb_ref, cum_b, upper, st_b, obw_ref, k2b_ref, qbb_ref, True)
        return st_f, st_b

    zero = jnp.zeros((hd, hd), F32)
    lax.fori_loop(0, n_groups, body, (zero, zero))

    ng = ng_ref[...]
    R = 256

    def fin(r, _):
        rows = pl.ds(pl.multiple_of(r * R, R), R)
        o = ofw_ref[rows, :] + obw_ref[rows, :]
        y = o * lax.rsqrt(jnp.mean(o * o, axis=-1, keepdims=True) + RMS_EPS) * ng
        g = g_ref[rows, :]
        o_ref[rows, :] = (y * (g * jax.nn.sigmoid(g))).astype(o_ref.dtype)
        return 0

    lax.fori_loop(0, seq // R, fin, 0)


def _hgrn(h0, lb_table, norm_g, layer, batch, seq):
    hd = A_HEAD_DIM

    def col(part):
        return pl.BlockSpec((None, seq, hd), lambda b, h: (b, 0, part * A_HEADS + h))

    n_tab = lb_table.shape[0]
    return pl.pallas_call(
        functools.partial(_hgrn_kernel, layer=layer, seq=seq),
        grid=(batch, A_HEADS),
        in_specs=[pl.BlockSpec((n_tab, hd), lambda b, h: (0, h)),
                  col(0), col(1), col(2), col(3), col(4),
                  pl.BlockSpec((1, hd), lambda b, h: (0, h))],
        out_specs=pl.BlockSpec((None, seq, hd), lambda b, h: (b, 0, h)),
        out_shape=jax.ShapeDtypeStruct((batch, seq, A_WIDTH), BF16),
        scratch_shapes=[pltpu.VMEM((seq, hd), F32), pltpu.VMEM((seq, hd), F32)]
        + [pltpu.VMEM((HGRN_ROWS, HGRN_GROUP * hd), BF16)] * 4,
        compiler_params=_cparams("parallel", "parallel"),
        name="hgrn2",
    )(lb_table, h0, h0, h0, h0, h0, norm_g)


def _gmlp_kernel(u_ref, v_ref, lg_ref, lbias_ref, ws_ref, bias_ref, o_ref, *, rows):
    vln = _layer_norm(v_ref[...], lg_ref[...], lbias_ref[...]).astype(BF16)
    for c in range(rows // B_CHUNK):
        r = slice(c * B_CHUNK, (c + 1) * B_CHUNK)
        for g in range(B_GROUPS):
            cs = slice(g * B_GROUP_DIM, (g + 1) * B_GROUP_DIM)
            s = jnp.dot(ws_ref[g], vln[r, cs], preferred_element_type=F32) + bias_ref[g]
            o_ref[r, cs] = (u_ref[r, cs] * s).astype(o_ref.dtype)


def _gmlp(h0, ln_g, ln_b, ws, bias, rows):
    M = h0.shape[0]
    u_blk = 5 * A_WIDTH // B_WIDTH
    bias_b = jnp.broadcast_to(bias[:, :, None], (B_GROUPS, B_CHUNK, B_GROUP_DIM))
    return pl.pallas_call(
        functools.partial(_gmlp_kernel, rows=rows),
        grid=(M // rows,),
        in_specs=[pl.BlockSpec((rows, B_WIDTH), lambda i: (i, u_blk)),
                  pl.BlockSpec((rows, B_WIDTH), lambda i: (i, u_blk + 1)),
                  pl.BlockSpec((1, B_WIDTH), lambda i: (0, 0)),
                  pl.BlockSpec((1, B_WIDTH), lambda i: (0, 0)),
                  pl.BlockSpec((B_GROUPS, B_CHUNK, B_CHUNK), lambda i: (0, 0, 0)),
                  pl.BlockSpec((B_GROUPS, B_CHUNK, B_GROUP_DIM), lambda i: (0, 0, 0))],
        out_specs=pl.BlockSpec((rows, B_WIDTH), lambda i: (i, 0)),
        out_shape=jax.ShapeDtypeStruct((M, B_WIDTH), BF16),
        compiler_params=_cparams("parallel"),
        name="gmlp",
    )(h0, h0, ln_g.reshape(1, -1), ln_b.reshape(1, -1), ws.astype(BF16), bias_b)


def _proj_ln_kernel(*refs, n_in):
    ins = refs[:n_in]
    w_ref, x_ref, g_ref, b_ref = refs[n_in:n_in + 4]
    out_refs = refs[n_in + 4:]
    acc = None
    off = 0
    for r in ins:
        kk = r.shape[1]
        d = jnp.dot(r[...], w_ref[off:off + kk, :], preferred_element_type=F32)
        acc = d if acc is None else acc + d
        off += kk
    out = _layer_norm(ALPHA * x_ref[...] + acc, g_ref[...], b_ref[...])
    for o_ref in out_refs:
        o_ref[...] = out.astype(o_ref.dtype)


def _proj_ln(parts, w, x, g, b, tm, out_dtypes, name):
    M, D = x.shape
    K = w.shape[0]
    in_specs = [pl.BlockSpec((tm, p.shape[1]), lambda i: (i, 0)) for p in parts]
    in_specs += [pl.BlockSpec((K, D), lambda i: (0, 0), pipeline_mode=pl.Buffered(1)),
                 pl.BlockSpec((tm, D), lambda i: (i, 0)),
                 pl.BlockSpec((1, D), lambda i: (0, 0)),
                 pl.BlockSpec((1, D), lambda i: (0, 0))]
    return pl.pallas_call(
        functools.partial(_proj_ln_kernel, n_in=len(parts)),
        grid=(M // tm,),
        in_specs=in_specs,
        out_specs=[pl.BlockSpec((tm, D), lambda i: (i, 0)) for _ in out_dtypes],
        out_shape=[jax.ShapeDtypeStruct((M, D), dt) for dt in out_dtypes],
        compiler_params=_cparams("parallel"),
        name=name,
    )(*parts, w, x, g.reshape(1, -1), b.reshape(1, -1))


def _ffn_up_kernel(xm_ref, xp_ref, xn_ref, wg_ref, wv_ref, cwg_ref, cwv_ref, cbg_ref, cbv_ref,
                   o_ref, lhs_ref, *, tm, seq, n_sub):
    H = BF16_SUBLANES
    i = pl.program_id(0)
    sub = o_ref.shape[1] // n_sub

    @pl.when(pl.program_id(1) == 0)
    def _():
        t0 = (i * tm) % seq
        zero = jnp.zeros((H, xm_ref.shape[1]), BF16)
        lhs_ref[0:H, :] = jnp.where(t0 != 0, xp_ref[...], zero)
        lhs_ref[H:H + tm, :] = xm_ref[...]
        lhs_ref[H + tm:2 * H + tm, :] = jnp.where(t0 + tm != seq, xn_ref[...], zero)

    lhs = lhs_ref[...]

    def conv(w_ref, cw_ref, cb_ref, cols):
        h = jnp.dot(lhs, w_ref[:, cols], preferred_element_type=F32)
        cw = cw_ref[:, cols]
        return (h[H - 1:H - 1 + tm] * cw[0:1] + h[H:H + tm] * cw[1:2]
                + h[H + 1:H + 1 + tm] * cw[2:3] + cb_ref[:, cols])

    for c in range(n_sub):
        cols = slice(c * sub, (c + 1) * sub)
        gate = conv(wg_ref, cwg_ref, cbg_ref, cols)
        val = conv(wv_ref, cwv_ref, cbv_ref, cols)
        o_ref[:, cols] = (gate * jax.nn.sigmoid(gate) * val).astype(o_ref.dtype)


def _ffn_up(xb, w_up, conv_w, conv_b, seq, tm, tn, n_sub):
    M, D = xb.shape
    H = BF16_SUBLANES
    nj = D_FF // tn
    hb = tm // H
    n_hblk = M // H
    conv_b = conv_b.reshape(1, -1)
    return pl.pallas_call(
        functools.partial(_ffn_up_kernel, tm=tm, seq=seq, n_sub=n_sub),
        grid=(M // tm, nj),
        in_specs=[pl.BlockSpec((tm, D), lambda i, j: (i, 0)),
                  pl.BlockSpec((H, D), lambda i, j: (jnp.maximum(i * hb - 1, 0), 0)),
                  pl.BlockSpec((H, D), lambda i, j: (jnp.minimum((i + 1) * hb, n_hblk - 1), 0)),
                  pl.BlockSpec((D, tn), lambda i, j: (0, j)),
                  pl.BlockSpec((D, tn), lambda i, j: (0, j + nj)),
                  pl.BlockSpec((3, tn), lambda i, j: (0, j)),
                  pl.BlockSpec((3, tn), lambda i, j: (0, j + nj)),
                  pl.BlockSpec((1, tn), lambda i, j: (0, j)),
                  pl.BlockSpec((1, tn), lambda i, j: (0, j + nj))],
        out_specs=pl.BlockSpec((tm, tn), lambda i, j: (i, j)),
        out_shape=jax.ShapeDtypeStruct((M, D_FF), BF16),
        scratch_shapes=[pltpu.VMEM((tm + 2 * H, D), BF16)],
        compiler_params=_cparams("parallel", "arbitrary"),
        name="ffn_up_conv",
    )(xb, xb, xb, w_up, w_up, conv_w, conv_w, conv_b, conv_b)


def _rope_tables(seq):
    t = jnp.arange(seq, dtype=jnp.int32)
    r = (t // GRID_W).astype(F32)
    c = (t % GRID_W).astype(F32)
    half = C_HEAD_DIM // 2
    inv_freq = jnp.exp(-math.log(ROPE_THETA) * jnp.arange(0, half, 2, dtype=F32) / half)
    ang_r = r[:, None] * inv_freq
    ang_c = c[:, None] * inv_freq
    ang = jnp.concatenate([ang_r, ang_r, ang_c, ang_c], axis=-1)
    sign = jnp.tile(jnp.concatenate([-jnp.ones(half // 2, F32), jnp.ones(half // 2, F32)]), 2)
    return jnp.cos(ang), jnp.sin(ang) * sign


def _qkv_kernel(x_ref, w_ref, gain_ref, cos_ref, sin_ref, o_ref, *, n_norm_tiles):
    acc = jnp.dot(x_ref[...], w_ref[...], preferred_element_type=F32)
    hd = C_HEAD_DIM
    j = pl.program_id(1)

    @pl.when(j < n_norm_tiles)
    def _():
        cos = cos_ref[...]
        sin = sin_ref[...]
        lane = lax.broadcasted_iota(jnp.int32, (1, hd), 1)
        low = (lane % (hd // 2)) < (hd // 4)
        for h in range(acc.shape[1] // hd):
            cs = slice(h * hd, (h + 1) * hd)
            a = acc[:, cs]
            y = a * lax.rsqrt(jnp.mean(a * a, axis=-1, keepdims=True) + RMS_EPS) * gain_ref[:, cs]
            partner = jnp.where(low, pltpu.roll(y, hd - hd // 4, 1), pltpu.roll(y, hd // 4, 1))
            o_ref[:, cs] = (y * cos + partner * sin).astype(o_ref.dtype)

    @pl.when(j >= n_norm_tiles)
    def _():
        o_ref[...] = acc.astype(o_ref.dtype)


def _qkv_proj(xb, w, gain, cos, sin, seq, tm, tn):
    M, D = xb.shape
    N = w.shape[1]
    n_norm_tiles = gain.shape[1] // tn
    n_pos_blk = seq // tm
    return pl.pallas_call(
        functools.partial(_qkv_kernel, n_norm_tiles=n_norm_tiles),
        grid=(M // tm, N // tn),
        in_specs=[pl.BlockSpec((tm, D), lambda i, j: (i, 0)),
                  pl.BlockSpec((D, tn), lambda i, j: (0, j)),
                  pl.BlockSpec((1, tn), lambda i, j: (0, jnp.minimum(j, n_norm_tiles - 1))),
                  pl.BlockSpec((tm, C_HEAD_DIM), lambda i, j: (i % n_pos_blk, 0)),
                  pl.BlockSpec((tm, C_HEAD_DIM), lambda i, j: (i % n_pos_blk, 0))],
        out_specs=pl.BlockSpec((tm, tn), lambda i, j: (i, j)),
        out_shape=jax.ShapeDtypeStruct((M, N), BF16),
        compiler_params=_cparams("parallel", "parallel"),
        name="qkv_norm_rope",
    )(xb, w, gain, cos, sin)


def _attn_kernel(q_ref, k_ref, v_ref, o_ref, vt_ref, st_ref, m_ref, *, tq, blocks_per_head, n_chunks):
    hd = C_HEAD_DIM
    s = pl.program_id(0)
    seq = k_ref.shape[0]
    ck = seq // n_chunks

    @pl.when(s == 0)
    def _():
        st_ref[1] = jnp.zeros(st_ref.shape[1:], F32)
        m_ref[1] = jnp.zeros(m_ref.shape[1:], F32)

    @pl.when(jnp.maximum(s - 1, 0) % blocks_per_head == 0)
    def _():
        vt_ref[0:hd, :] = v_ref[...].astype(F32).T.astype(BF16)
        vt_ref[hd:, :] = jnp.ones((vt_ref.shape[0] - hd, seq), BF16)

    def step(fill, drain):
        q = jnp.concatenate([q_ref[:, g * hd:(g + 1) * hd] for g in range(C_GROUP)], axis=0)
        m_drain = m_ref[drain]
        m_fill = None
        acc = None
        for c in range(n_chunks):
            rows = slice(c * ck, (c + 1) * ck)
            sc = lax.dot_general(k_ref[rows, :], q, NT_DIMS, preferred_element_type=F32)
            st_ref[fill, rows, :] = sc
            mc = jnp.max(sc, axis=0, keepdims=True)
            m_fill = mc if c == 0 else jnp.maximum(m_fill, mc)
            p = jnp.exp2(st_ref[drain, rows, :] - m_drain).astype(BF16)
            pv = jnp.dot(vt_ref[:, rows], p, preferred_element_type=F32)
            acc = pv if c == 0 else acc + pv
        m_ref[fill] = m_fill
        o = acc[0:hd] * (1.0 / acc[hd:hd + 1])
        for g in range(C_GROUP):
            o_ref[:, g * hd:(g + 1) * hd] = o[:, g * tq:(g + 1) * tq].T.astype(o_ref.dtype)

    @pl.when(s % 2 == 0)
    def _():
        step(0, 1)

    @pl.when(s % 2 == 1)
    def _():
        step(1, 0)


def _attention(qkv, batch, seq, tq, n_chunks):
    hd = C_HEAD_DIM
    gw = C_GROUP * hd
    bph = seq // tq
    n_blocks = batch * C_KV_HEADS * bph

    def coords(blk):
        return blk // (C_KV_HEADS * bph), (blk // bph) % C_KV_HEADS, blk % bph

    def q_map(s):
        b, h, i = coords(jnp.minimum(s, n_blocks - 1))
        return b, i, h

    def k_map(s):
        b, h, _ = coords(jnp.minimum(s, n_blocks - 1))
        return b, 0, C_Q_HEADS + h

    def v_map(s):
        b, h, _ = coords(jnp.maximum(s - 1, 0))
        return b, 0, C_Q_HEADS + C_KV_HEADS + h

    def o_map(s):
        b, h, i = coords(jnp.maximum(s - 1, 0))
        return b, i, h

    return pl.pallas_call(
        functools.partial(_attn_kernel, tq=tq, blocks_per_head=bph, n_chunks=n_chunks),
        grid=(n_blocks + 1,),
        in_specs=[pl.BlockSpec((None, tq, gw), q_map),
                  pl.BlockSpec((None, seq, hd), k_map),
                  pl.BlockSpec((None, seq, hd), v_map)],
        out_specs=pl.BlockSpec((None, tq, gw), o_map),
        out_shape=jax.ShapeDtypeStruct((batch, seq, C_Q_HEADS * hd), BF16),
        scratch_shapes=[pltpu.VMEM((hd + BF16_SUBLANES, seq), BF16),
                        pltpu.VMEM((2, seq, C_GROUP * tq), F32),
                        pltpu.VMEM((2, 1, C_GROUP * tq), F32)],
        compiler_params=_cparams("arbitrary"),
        name="gqa_attention",
    )(qkv, qkv, qkv)


def _conv_ffn_ln(x, xb, w_up, conv_w, conv_b, w_down, g, b, seq, out_dtypes):
    act = _ffn_up(xb, w_up.astype(BF16), conv_w, conv_b, seq, tm=1024, tn=512, n_sub=1)
    return _proj_ln([act], w_down.astype(BF16), x, g, b, 256, out_dtypes, "ffn_down_ln")


def kernel(x, w_in_ab, hgrn_lb_table, hgrn_norm_g, gmlp_ln_g, gmlp_ln_b, gmlp_ws, gmlp_bias,
           w_out_ab, w_in_attn, q_norm_g, k_norm_g, w_out_attn, ffn_up, ffn_conv_w, ffn_conv_b,
           ffn_down, ln1_g, ln1_b, ln2_g, ln2_b):
    batch, seq, d = x.shape
    M = batch * seq
    x = x.reshape(M, d)
    xb = x.astype(BF16)

    h0 = _matmul(xb, w_in_ab[0].astype(BF16), 1024, 1024, F32, "inproj_ab")
    o_a = _hgrn(h0.reshape(batch, seq, -1), hgrn_lb_table, hgrn_norm_g[0].reshape(1, -1), 0, batch, seq)
    o_b = _gmlp(h0, gmlp_ln_g[0], gmlp_ln_b[0], gmlp_ws[0], gmlp_bias[0], rows=512)
    both = (F32, BF16)
    x, xb = _proj_ln([o_a.reshape(M, -1), o_b], w_out_ab[0].astype(BF16), x, ln1_g[0], ln1_b[0],
                     512, both, "outproj_ln")
    x, xb = _conv_ffn_ln(x, xb, ffn_up[0], ffn_conv_w[0], ffn_conv_b[0], ffn_down[0], ln2_g[0], ln2_b[0],
                         seq, both)

    scale = C_HEAD_DIM ** -0.5 * math.log2(math.e)
    gain = jnp.concatenate([jnp.tile(q_norm_g[0] * scale, C_Q_HEADS), jnp.tile(k_norm_g[0], C_KV_HEADS)])
    cos, sin = _rope_tables(seq)
    qkv = _qkv_proj(xb, w_in_attn[0].astype(BF16), gain.reshape(1, -1), cos, sin, seq, tm=1024, tn=512)
    att = _attention(qkv.reshape(batch, seq, -1), batch, seq, tq=128, n_chunks=8)
    x, xb = _proj_ln([att.reshape(M, -1)], w_out_attn[0].astype(BF16), x, ln1_g[1], ln1_b[1],
                     512, both, "outproj_ln")
    (x,) = _conv_ffn_ln(x, xb, ffn_up[1], ffn_conv_w[1], ffn_conv_b[1], ffn_down[1], ln2_g[1], ln2_b[1],
                        seq, (F32,))
    return x.reshape(batch, seq, d)
```

```python
import functools
import math

import jax
import jax.numpy as jnp
from jax import lax
from jax.experimental import pallas as pl
from jax.experimental.pallas import tpu as pltpu

F32 = jnp.float32
BF16 = jnp.bfloat16

D_MODEL = 2048
GRID_W = 64
A_HEAD_DIM = 128
A_WIDTH = D_MODEL // 2
A_HEADS = A_WIDTH // A_HEAD_DIM
HGRN_CHUNK = 64
B_WIDTH = D_MODEL // 2
B_GROUP_DIM = 128
B_GROUPS = B_WIDTH // B_GROUP_DIM
B_CHUNK = 128
C_HEAD_DIM = 128
C_Q_HEADS = D_MODEL // C_HEAD_DIM
C_KV_HEADS = C_Q_HEADS // 4
C_GROUP = C_Q_HEADS // C_KV_HEADS
ROPE_THETA = 10000.0
D_FF = 5632
DEPTH = 2
ALPHA = (2.0 * DEPTH) ** 0.25
LN_EPS = 1e-5
RMS_EPS = 1e-6

V7X_VMEM_LIMIT_BYTES = 56 * 1024 * 1024
BF16_SUBLANES = 16

NT_DIMS = (((1,), (1,)), ((), ()))


def _cparams(*sem):
    return pltpu.CompilerParams(dimension_semantics=sem, vmem_limit_bytes=V7X_VMEM_LIMIT_BYTES)


def _layer_norm(y, g, b):
    mu = jnp.mean(y, axis=-1, keepdims=True)
    yc = y - mu
    var = jnp.mean(yc * yc, axis=-1, keepdims=True)
    return yc * lax.rsqrt(var + LN_EPS) * g + b


def _mm_kernel(x_ref, w_ref, o_ref, wb_ref):
    @pl.when(pl.program_id(1) == 0)
    def _():
        wb_ref[...] = w_ref[...].astype(BF16)

    o_ref[...] = jnp.dot(x_ref[...], wb_ref[...], preferred_element_type=F32).astype(o_ref.dtype)


def _matmul(x, w, layer, tm, tn, out_dtype, name):
    M, K = x.shape
    N = w.shape[2]
    return pl.pallas_call(
        _mm_kernel,
        grid=(N // tn, M // tm),
        in_specs=[pl.BlockSpec((tm, K), lambda j, i: (i, 0)),
                  pl.BlockSpec((None, K, tn), lambda j, i: (layer, 0, j))],
        out_specs=pl.BlockSpec((tm, tn), lambda j, i: (i, j)),
        out_shape=jax.ShapeDtypeStruct((M, N), out_dtype),
        scratch_shapes=[pltpu.VMEM((K, tn), BF16)],
        compiler_params=_cparams("parallel", "arbitrary"),
        name=name,
    )(x, w)


HGRN_GROUP = 4
HGRN_ROWS = HGRN_GROUP * HGRN_CHUNK


def _hgrn_kernel(tab_ref, q_ref, ff_ref, fb_ref, i_ref, g_ref, ng_ref, o_ref,
                 ofw_ref, obw_ref, qef_ref, kef_ref, k2f_ref, qbf_ref, decf_ref,
                 qeb_ref, keb_ref, k2b_ref, qbb_ref, decb_ref, *, layer, seq):
    C, G, R, hd = HGRN_CHUNK, HGRN_GROUP, HGRN_ROWS, A_HEAD_DIM
    n_groups = seq // R
    tab = tab_ref[...]
    e = jnp.exp(tab - jnp.max(tab, axis=0, keepdims=True))
    sm = e / jnp.sum(e, axis=0, keepdims=True)
    lb = jnp.sum(sm[:layer + 1], axis=0, keepdims=True)

    row = lax.broadcasted_iota(jnp.int32, (R, R), 0)
    col = lax.broadcasted_iota(jnp.int32, (R, R), 1)
    same = (row // C) == (col // C)
    lower = same & (col <= row)
    upper = same & (col >= row)
    X = BF16_SUBLANES
    jrow = lax.broadcasted_iota(jnp.int32, (X, R), 0)
    jcol = lax.broadcasted_iota(jnp.int32, (X, R), 1)
    in_chunk = (jcol // C) == jrow
    last_rows = in_chunk.astype(BF16)
    mid_f = (in_chunk & (jcol % C <= C // 2 - 1)).astype(BF16)
    mid_b = (in_chunk & (jcol % C >= C // 2)).astype(BF16)
    cum_f = jnp.concatenate([lower.astype(BF16), mid_f, last_rows], axis=0)
    cum_b = jnp.concatenate([upper.astype(BF16), mid_b, last_rows], axis=0)

    for ref in (k2f_ref, qbf_ref, k2b_ref, qbb_ref):
        ref[...] = jnp.zeros(ref.shape, BF16)

    def per_chunk(extra, j0):
        return jnp.concatenate(
            [jnp.broadcast_to(extra[j0 + c:j0 + c + 1], (C, hd)) for c in range(G)], axis=0)

    def prepare(start, d, slot):
        f_ref, cum = d["f"], d["cum"]
        rows = pl.ds(start, R)
        f = lb + (1.0 - lb) * jax.nn.sigmoid(f_ref[rows, :])
        k = 1.0 - f
        lf = jnp.log(f)
        hi = lf.astype(BF16)
        r1 = lf - hi.astype(F32)
        mid = r1.astype(BF16)
        lo = (r1 - mid.astype(F32)).astype(BF16)
        yield
        ball = jnp.dot(cum, jnp.concatenate([hi, mid, lo], axis=1), preferred_element_type=F32)
        ball = ball[:, 0:hd] + ball[:, hd:2 * hd] + ball[:, 2 * hd:3 * hd]
        yield
        b = ball[0:R]
        b_mid = per_chunk(ball, R)
        b_last = per_chunk(ball, R + X)
        q = q_ref[rows, :]
        d["qe"][slot] = (q * jnp.exp(b - b_mid)).astype(BF16)
        d["ke"][slot] = (k * jnp.exp(b_mid - b)).astype(BF16)
        yield
        qb = (q * jnp.exp(b)).astype(BF16)
        k2 = (k * jnp.exp(b_last - b)).astype(BF16)
        for c in range(G):
            d["k2"][slot, c * C:(c + 1) * C, c * hd:(c + 1) * hd] = k2[c * C:(c + 1) * C]
            d["qb"][slot, c * C:(c + 1) * C, c * hd:(c + 1) * hd] = qb[c * C:(c + 1) * C]
        d["dec"][slot] = jnp.exp(ball[R + X:R + X + 8])
        yield

    def apply(start, d, slot, st):
        rows = pl.ds(start, R)
        v = i_ref[rows, :]
        sc = lax.dot_general(d["qe"][slot], d["ke"][slot], NT_DIMS, preferred_element_type=F32)
        sc = jnp.where(d["mask"], sc, 0.0)
        yield
        o = jnp.dot(sc.astype(BF16), v.astype(BF16), preferred_element_type=F32)
        kv_t = jnp.dot(v.T.astype(BF16), d["k2"][slot], preferred_element_type=F32)
        yield
        dec = d["dec"][slot]
        entering = [None] * G
        for c in (reversed(range(G)) if d["reverse"] else range(G)):
            entering[c] = st.astype(BF16)
            st = st * dec[c:c + 1] + kv_t[:, c * hd:(c + 1) * hd]
        yield
        o = o + lax.dot_general(d["qb"][slot], jnp.concatenate(entering, axis=1), NT_DIMS,
                                preferred_element_type=F32)
        d["out"][rows, :] = o
        return st

    def interleave(*gens):
        results = [None] * len(gens)
        active = list(enumerate(gens))
        while active:
            still = []
            for idx, g in active:
                try:
                    next(g)
                    still.append((idx, g))
                except StopIteration as done:
                    results[idx] = done.value
            active = still
        return results

    fw = dict(f=ff_ref, cum=cum_f, mask=lower, reverse=False, out=ofw_ref,
              qe=qef_ref, ke=kef_ref, k2=k2f_ref, qb=qbf_ref, dec=decf_ref)
    bw = dict(f=fb_ref, cum=cum_b, mask=upper, reverse=True, out=obw_ref,
              qe=qeb_ref, ke=keb_ref, k2=k2b_ref, qb=qbb_ref, dec=decb_ref)

    def starts(g):
        g = jnp.minimum(g, n_groups - 1)
        return pl.multiple_of(g * R, R), pl.multiple_of((n_groups - 1 - g) * R, R)

    s_f, s_b = starts(0)
    interleave(prepare(s_f, fw, 0), prepare(s_b, bw, 0))

    def body(n, carry):
        st_f, st_b = carry
        for slot in (0, 1):
            a_f, a_b = starts(2 * n + slot)
            p_f, p_b = starts(2 * n + slot + 1)
            _, _, st_f, st_b = interleave(prepare(p_f, fw, 1 - slot), prepare(p_b, bw, 1 - slot),
                                          apply(a_f, fw, slot, st_f), apply(a_b, bw, slot, st_b))
        return st_f, st_b

    zero = jnp.zeros((hd, hd), F32)
    lax.fori_loop(0, n_groups // 2, body, (zero, zero))

    ng = ng_ref[...]
    R = 256

    def fin(r, _):
        rows = pl.ds(pl.multiple_of(r * R, R), R)
        o = ofw_ref[rows, :] + obw_ref[rows, :]
        y = o * lax.rsqrt(jnp.mean(o * o, axis=-1, keepdims=True) + RMS_EPS) * ng
        g = g_ref[rows, :]
        o_ref[rows, :] = (y * (g * jax.nn.sigmoid(g))).astype(o_ref.dtype)
        return 0

    lax.fori_loop(0, seq // R, fin, 0)


def _hgrn(h0, lb_table, norm_g, layer, batch, seq):
    hd = A_HEAD_DIM

    def col(part):
        return pl.BlockSpec((None, seq, hd), lambda b, h: (b, 0, part * A_HEADS + h))

    n_tab = lb_table.shape[0]
    return pl.pallas_call(
        functools.partial(_hgrn_kernel, layer=layer, seq=seq),
        grid=(batch, A_HEADS),
        in_specs=[pl.BlockSpec((n_tab, hd), lambda b, h: (0, h)),
                  col(0), col(1), col(2), col(3), col(4),
                  pl.BlockSpec((1, hd), lambda b, h: (0, h))],
        out_specs=pl.BlockSpec((None, seq, hd), lambda b, h: (b, 0, h)),
        out_shape=jax.ShapeDtypeStruct((batch, seq, A_WIDTH), BF16),
        scratch_shapes=[pltpu.VMEM((seq, hd), F32), pltpu.VMEM((seq, hd), F32)]
        + 2 * [pltpu.VMEM((2, HGRN_ROWS, hd), BF16), pltpu.VMEM((2, HGRN_ROWS, hd), BF16),
               pltpu.VMEM((2, HGRN_ROWS, HGRN_GROUP * hd), BF16),
               pltpu.VMEM((2, HGRN_ROWS, HGRN_GROUP * hd), BF16),
               pltpu.VMEM((2, 8, hd), F32)],
        compiler_params=_cparams("parallel", "parallel"),
        name="hgrn2",
    )(lb_table, h0, h0, h0, h0, h0, norm_g)


def _gmlp_kernel(u_ref, v_ref, lg_ref, lbias_ref, ws_ref, bias_ref, o_ref, *, rows):
    vln = _layer_norm(v_ref[...], lg_ref[...], lbias_ref[...]).astype(BF16)
    for c in range(rows // B_CHUNK):
        r = slice(c * B_CHUNK, (c + 1) * B_CHUNK)
        for g in range(B_GROUPS):
            cs = slice(g * B_GROUP_DIM, (g + 1) * B_GROUP_DIM)
            s = jnp.dot(ws_ref[g], vln[r, cs], preferred_element_type=F32) + bias_ref[g]
            o_ref[r, cs] = (u_ref[r, cs] * s).astype(o_ref.dtype)


def _gmlp(h0, ln_g, ln_b, ws, bias, rows):
    M = h0.shape[0]
    u_blk = 5 * A_WIDTH // B_WIDTH
    bias_b = jnp.broadcast_to(bias[:, :, None], (B_GROUPS, B_CHUNK, B_GROUP_DIM))
    return pl.pallas_call(
        functools.partial(_gmlp_kernel, rows=rows),
        grid=(M // rows,),
        in_specs=[pl.BlockSpec((rows, B_WIDTH), lambda i: (i, u_blk)),
                  pl.BlockSpec((rows, B_WIDTH), lambda i: (i, u_blk + 1)),
                  pl.BlockSpec((1, B_WIDTH), lambda i: (0, 0)),
                  pl.BlockSpec((1, B_WIDTH), lambda i: (0, 0)),
                  pl.BlockSpec((B_GROUPS, B_CHUNK, B_CHUNK), lambda i: (0, 0, 0)),
                  pl.BlockSpec((B_GROUPS, B_CHUNK, B_GROUP_DIM), lambda i: (0, 0, 0))],
        out_specs=pl.BlockSpec((rows, B_WIDTH), lambda i: (i, 0)),
        out_shape=jax.ShapeDtypeStruct((M, B_WIDTH), BF16),
        compiler_params=_cparams("parallel"),
        name="gmlp",
    )(h0, h0, ln_g.reshape(1, -1), ln_b.reshape(1, -1), ws.astype(BF16), bias_b)


def _proj_ln_kernel(*refs, n_in):
    ins = refs[:n_in]
    w_ref, x_ref, g_ref, b_ref = refs[n_in:n_in + 4]
    out_refs = refs[n_in + 4:]
    acc = None
    off = 0
    for r in ins:
        kk = r.shape[1]
        d = jnp.dot(r[...], w_ref[off:off + kk, :], preferred_element_type=F32)
        acc = d if acc is None else acc + d
        off += kk
    out = _layer_norm(ALPHA * x_ref[...] + acc, g_ref[...], b_ref[...])
    for o_ref in out_refs:
        o_ref[...] = out.astype(o_ref.dtype)


def _proj_ln(parts, w, x, g, b, tm, out_dtypes, name):
    M, D = x.shape
    K = w.shape[0]
    in_specs = [pl.BlockSpec((tm, p.shape[1]), lambda i: (i, 0)) for p in parts]
    in_specs += [pl.BlockSpec((K, D), lambda i: (0, 0), pipeline_mode=pl.Buffered(1)),
                 pl.BlockSpec((tm, D), lambda i: (i, 0)),
                 pl.BlockSpec((1, D), lambda i: (0, 0)),
                 pl.BlockSpec((1, D), lambda i: (0, 0))]
    return pl.pallas_call(
        functools.partial(_proj_ln_kernel, n_in=len(parts)),
        grid=(M // tm,),
        in_specs=in_specs,
        out_specs=[pl.BlockSpec((tm, D), lambda i: (i, 0)) for _ in out_dtypes],
        out_shape=[jax.ShapeDtypeStruct((M, D), dt) for dt in out_dtypes],
        compiler_params=_cparams("parallel"),
        name=name,
    )(*parts, w, x, g.reshape(1, -1), b.reshape(1, -1))


def _ffn_up_kernel(xm_ref, xp_ref, xn_ref, wg_ref, wv_ref, cwg_ref, cwv_ref, cbg_ref, cbv_ref,
                   o_ref, lhs_ref, wgb_ref, wvb_ref, *, tm, seq):
    H = BF16_SUBLANES
    i = pl.program_id(1)

    @pl.when(i == 0)
    def _():
        wgb_ref[...] = wg_ref[...].astype(BF16)
        wvb_ref[...] = wv_ref[...].astype(BF16)

    t0 = (i * tm) % seq
    zero = jnp.zeros((H, xm_ref.shape[1]), BF16)
    lhs_ref[0:H, :] = jnp.where(t0 != 0, xp_ref[...], zero)
    lhs_ref[H:H + tm, :] = xm_ref[...]
    lhs_ref[H + tm:2 * H + tm, :] = jnp.where(t0 + tm != seq, xn_ref[...], zero)
    lhs = lhs_ref[...]

    def conv(w_ref, cw_ref, cb_ref):
        h = jnp.dot(lhs, w_ref[...], preferred_element_type=F32)
        cw = cw_ref[...]
        return (h[H - 1:H - 1 + tm] * cw[0:1] + h[H:H + tm] * cw[1:2]
                + h[H + 1:H + 1 + tm] * cw[2:3] + cb_ref[...])

    gate = conv(wgb_ref, cwg_ref, cbg_ref)
    val = conv(wvb_ref, cwv_ref, cbv_ref)
    o_ref[...] = (gate * jax.nn.sigmoid(gate) * val).astype(o_ref.dtype)


def _ffn_up(xb, w_up, conv_w, conv_b, layer, seq, tm, tn):
    M, D = xb.shape
    H = BF16_SUBLANES
    nj = D_FF // tn
    hb = tm // H
    n_hblk = M // H
    conv_b = conv_b.reshape(conv_b.shape[0], 1, -1)
    return pl.pallas_call(
        functools.partial(_ffn_up_kernel, tm=tm, seq=seq),
        grid=(nj, M // tm),
        in_specs=[pl.BlockSpec((tm, D), lambda j, i: (i, 0)),
                  pl.BlockSpec((H, D), lambda j, i: (jnp.maximum(i * hb - 1, 0), 0)),
                  pl.BlockSpec((H, D), lambda j, i: (jnp.minimum((i + 1) * hb, n_hblk - 1), 0)),
                  pl.BlockSpec((None, D, tn), lambda j, i: (layer, 0, j)),
                  pl.BlockSpec((None, D, tn), lambda j, i: (layer, 0, j + nj)),
                  pl.BlockSpec((None, 3, tn), lambda j, i: (layer, 0, j)),
                  pl.BlockSpec((None, 3, tn), lambda j, i: (layer, 0, j + nj)),
                  pl.BlockSpec((None, 1, tn), lambda j, i: (layer, 0, j)),
                  pl.BlockSpec((None, 1, tn), lambda j, i: (layer, 0, j + nj))],
        out_specs=pl.BlockSpec((tm, tn), lambda j, i: (i, j)),
        out_shape=jax.ShapeDtypeStruct((M, D_FF), BF16),
        scratch_shapes=[pltpu.VMEM((tm + 2 * H, D), BF16),
                        pltpu.VMEM((D, tn), BF16), pltpu.VMEM((D, tn), BF16)],
        compiler_params=_cparams("parallel", "arbitrary"),
        name="ffn_up_conv",
    )(xb, xb, xb, w_up, w_up, conv_w, conv_w, conv_b, conv_b)


def _rope_tables(seq):
    t = jnp.arange(seq, dtype=jnp.int32)
    r = (t // GRID_W).astype(F32)
    c = (t % GRID_W).astype(F32)
    half = C_HEAD_DIM // 2
    inv_freq = jnp.exp(-math.log(ROPE_THETA) * jnp.arange(0, half, 2, dtype=F32) / half)
    ang_r = r[:, None] * inv_freq
    ang_c = c[:, None] * inv_freq
    ang = jnp.concatenate([ang_r, ang_r, ang_c, ang_c], axis=-1)
    sign = jnp.tile(jnp.concatenate([-jnp.ones(half // 2, F32), jnp.ones(half // 2, F32)]), 2)
    return jnp.cos(ang), jnp.sin(ang) * sign


def _qkv_kernel(x_ref, w_ref, gain_ref, cos_ref, sin_ref, o_ref, wb_ref, *, n_norm_tiles):
    @pl.when(pl.program_id(1) == 0)
    def _():
        wb_ref[...] = w_ref[...].astype(BF16)

    acc = jnp.dot(x_ref[...], wb_ref[...], preferred_element_type=F32)
    hd = C_HEAD_DIM
    j = pl.program_id(0)

    @pl.when(j < n_norm_tiles)
    def _():
        cos = cos_ref[...]
        sin = sin_ref[...]
        lane = lax.broadcasted_iota(jnp.int32, (1, hd), 1)
        low = (lane % (hd // 2)) < (hd // 4)
        for h in range(acc.shape[1] // hd):
            cs = slice(h * hd, (h + 1) * hd)
            a = acc[:, cs]
            y = a * lax.rsqrt(jnp.mean(a * a, axis=-1, keepdims=True) + RMS_EPS) * gain_ref[:, cs]
            partner = jnp.where(low, pltpu.roll(y, hd - hd // 4, 1), pltpu.roll(y, hd // 4, 1))
            o_ref[:, cs] = (y * cos + partner * sin).astype(o_ref.dtype)

    @pl.when(j >= n_norm_tiles)
    def _():
        o_ref[...] = acc.astype(o_ref.dtype)


def _qkv_proj(xb, w, layer, gain, cos, sin, seq, tm, tn):
    M, D = xb.shape
    N = w.shape[2]
    n_norm_tiles = gain.shape[1] // tn
    n_pos_blk = seq // tm
    return pl.pallas_call(
        functools.partial(_qkv_kernel, n_norm_tiles=n_norm_tiles),
        grid=(N // tn, M // tm),
        in_specs=[pl.BlockSpec((tm, D), lambda j, i: (i, 0)),
                  pl.BlockSpec((None, D, tn), lambda j, i: (layer, 0, j)),
                  pl.BlockSpec((1, tn), lambda j, i: (0, jnp.minimum(j, n_norm_tiles - 1))),
                  pl.BlockSpec((tm, C_HEAD_DIM), lambda j, i: (i % n_pos_blk, 0)),
                  pl.BlockSpec((tm, C_HEAD_DIM), lambda j, i: (i % n_pos_blk, 0))],
        out_specs=pl.BlockSpec((tm, tn), lambda j, i: (i, j)),
        out_shape=jax.ShapeDtypeStruct((M, N), BF16),
        scratch_shapes=[pltpu.VMEM((D, tn), BF16)],
        compiler_params=_cparams("parallel", "arbitrary"),
        name="qkv_norm_rope",
    )(xb, w, gain, cos, sin)


def _attn_kernel(q_ref, k_ref, v_ref, o_ref, vt_ref, st_ref, m_ref, *, tq, blocks_per_head, n_chunks):
    hd = C_HEAD_DIM
    s = pl.program_id(0)
    seq = k_ref.shape[0]
    ck = seq // n_chunks

    @pl.when(s == 0)
    def _():
        st_ref[1] = jnp.zeros(st_ref.shape[1:], F32)
        m_ref[1] = jnp.zeros(m_ref.shape[1:], F32)

    @pl.when(jnp.maximum(s - 1, 0) % blocks_per_head == 0)
    def _():
        vt_ref[0:hd, :] = v_ref[...].astype(F32).T.astype(BF16)
        vt_ref[hd:, :] = jnp.ones((vt_ref.shape[0] - hd, seq), BF16)

    def step(fill, drain):
        q = jnp.concatenate([q_ref[:, g * hd:(g + 1) * hd] for g in range(C_GROUP)], axis=0)
        m_drain = m_ref[drain]
        m_fill = None
        acc = None
        for c in range(n_chunks):
            rows = slice(c * ck, (c + 1) * ck)
            sc = lax.dot_general(k_ref[rows, :], q, NT_DIMS, preferred_element_type=F32)
            st_ref[fill, rows, :] = sc
            mc = jnp.max(sc, axis=0, keepdims=True)
            m_fill = mc if c == 0 else jnp.maximum(m_fill, mc)
            p = jnp.exp2(st_ref[drain, rows, :] - m_drain).astype(BF16)
            pv = jnp.dot(vt_ref[:, rows], p, preferred_element_type=F32)
            acc = pv if c == 0 else acc + pv
        m_ref[fill] = m_fill
        o = acc[0:hd] * (1.0 / acc[hd:hd + 1])
        for g in range(C_GROUP):
            o_ref[:, g * hd:(g + 1) * hd] = o[:, g * tq:(g + 1) * tq].T.astype(o_ref.dtype)

    @pl.when(s % 2 == 0)
    def _():
        step(0, 1)

    @pl.when(s % 2 == 1)
    def _():
        step(1, 0)


def _attention(qkv, batch, seq, tq, n_chunks):
    hd = C_HEAD_DIM
    gw = C_GROUP * hd
    bph = seq // tq
    n_blocks = batch * C_KV_HEADS * bph

    def coords(blk):
        return blk // (C_KV_HEADS * bph), (blk // bph) % C_KV_HEADS, blk % bph

    def q_map(s):
        b, h, i = coords(jnp.minimum(s, n_blocks - 1))
        return b, i, h

    def k_map(s):
        b, h, _ = coords(jnp.minimum(s, n_blocks - 1))
        return b, 0, C_Q_HEADS + h

    def v_map(s):
        b, h, _ = coords(jnp.maximum(s - 1, 0))
        return b, 0, C_Q_HEADS + C_KV_HEADS + h

    def o_map(s):
        b, h, i = coords(jnp.maximum(s - 1, 0))
        return b, i, h

    return pl.pallas_call(
        functools.partial(_attn_kernel, tq=tq, blocks_per_head=bph, n_chunks=n_chunks),
        grid=(n_blocks + 1,),
        in_specs=[pl.BlockSpec((None, tq, gw), q_map),
                  pl.BlockSpec((None, seq, hd), k_map),
                  pl.BlockSpec((None, seq, hd), v_map)],
        out_specs=pl.BlockSpec((None, tq, gw), o_map),
        out_shape=jax.ShapeDtypeStruct((batch, seq, C_Q_HEADS * hd), BF16),
        scratch_shapes=[pltpu.VMEM((hd + BF16_SUBLANES, seq), BF16),
                        pltpu.VMEM((2, seq, C_GROUP * tq), F32),
                        pltpu.VMEM((2, 1, C_GROUP * tq), F32)],
        compiler_params=_cparams("arbitrary"),
        name="gqa_attention",
    )(qkv, qkv, qkv)


def _conv_ffn_ln(x, xb, w_up, conv_w, conv_b, w_down, g, b, layer, seq, out_dtypes):
    act = _ffn_up(xb, w_up, conv_w, conv_b, layer, seq, tm=1024, tn=512)
    return _proj_ln([act], w_down[layer].astype(BF16), x, g[layer], b[layer], 256, out_dtypes, "ffn_down_ln")


def kernel(x, w_in_ab, hgrn_lb_table, hgrn_norm_g, gmlp_ln_g, gmlp_ln_b, gmlp_ws, gmlp_bias,
           w_out_ab, w_in_attn, q_norm_g, k_norm_g, w_out_attn, ffn_up, ffn_conv_w, ffn_conv_b,
           ffn_down, ln1_g, ln1_b, ln2_g, ln2_b):
    batch, seq, d = x.shape
    M = batch * seq
    x = x.reshape(M, d)
    xb = x.astype(BF16)

    h0 = _matmul(xb, w_in_ab, 0, 1024, 1024, F32, "inproj_ab")
    o_a = _hgrn(h0.reshape(batch, seq, -1), hgrn_lb_table, hgrn_norm_g[0].reshape(1, -1), 0, batch, seq)
    o_b = _gmlp(h0, gmlp_ln_g[0], gmlp_ln_b[0], gmlp_ws[0], gmlp_bias[0], rows=512)
    both = (F32, BF16)
    x, xb = _proj_ln([o_a.reshape(M, -1), o_b], w_out_ab[0].astype(BF16), x, ln1_g[0], ln1_b[0],
                     512, both, "outproj_ln")
    x, xb = _conv_ffn_ln(x, xb, ffn_up, ffn_conv_w, ffn_conv_b, ffn_down, ln2_g, ln2_b, 0, seq, both)

    scale = C_HEAD_DIM ** -0.5 * math.log2(math.e)
    gain = jnp.concatenate([jnp.tile(q_norm_g[0] * scale, C_Q_HEADS), jnp.tile(k_norm_g[0], C_KV_HEADS)])
    cos, sin = _rope_tables(seq)
    qkv = _qkv_proj(xb, w_in_attn, 0, gain.reshape(1, -1), cos, sin, seq, tm=1024, tn=512)
    att = _attention(qkv.reshape(batch, seq, -1), batch, seq, tq=128, n_chunks=8)
    x, xb = _proj_ln([att.reshape(M, -1)], w_out_attn[0].astype(BF16), x, ln1_g[1], ln1_b[1],
                     512, both, "outproj_ln")
    (x,) = _conv_ffn_ln(x, xb, ffn_up, ffn_conv_w, ffn_conv_b, ffn_down, ln2_g, ln2_b, 1, seq, (F32,))
    return x.reshape(batch, seq, d)
```

```python
import functools
import math

import jax
import jax.numpy as jnp
from jax import lax
from jax.experimental import pallas as pl
from jax.experimental.pallas import tpu as pltpu

F32 = jnp.float32
BF16 = jnp.bfloat16

D_MODEL = 2048
GRID_W = 64
A_HEAD_DIM = 128
A_WIDTH = D_MODEL // 2
A_HEADS = A_WIDTH // A_HEAD_DIM
HGRN_CHUNK = 64
B_WIDTH = D_MODEL // 2
B_GROUP_DIM = 128
B_GROUPS = B_WIDTH // B_GROUP_DIM
B_CHUNK = 128
C_HEAD_DIM = 128
C_Q_HEADS = D_MODEL // C_HEAD_DIM
C_KV_HEADS = C_Q_HEADS // 4
C_GROUP = C_Q_HEADS // C_KV_HEADS
ROPE_THETA = 10000.0
D_FF = 5632
DEPTH = 2
ALPHA = (2.0 * DEPTH) ** 0.25
LN_EPS = 1e-5
RMS_EPS = 1e-6

V7X_VMEM_LIMIT_BYTES = 56 * 1024 * 1024
BF16_SUBLANES = 16

NT_DIMS = (((1,), (1,)), ((), ()))


def _cparams(*sem):
    return pltpu.CompilerParams(dimension_semantics=sem, vmem_limit_bytes=V7X_VMEM_LIMIT_BYTES)


def _layer_norm(y, g, b):
    mu = jnp.mean(y, axis=-1, keepdims=True)
    yc = y - mu
    var = jnp.mean(yc * yc, axis=-1, keepdims=True)
    return yc * lax.rsqrt(var + LN_EPS) * g + b


def _mm_kernel(x_ref, w_ref, o_ref, wb_ref):
    @pl.when(pl.program_id(1) == 0)
    def _():
        wb_ref[...] = w_ref[...].astype(BF16)

    o_ref[...] = jnp.dot(x_ref[...], wb_ref[...], preferred_element_type=F32).astype(o_ref.dtype)


def _matmul(x, w, layer, tm, tn, out_dtype, name):
    M, K = x.shape
    N = w.shape[2]
    return pl.pallas_call(
        _mm_kernel,
        grid=(N // tn, M // tm),
        in_specs=[pl.BlockSpec((tm, K), lambda j, i: (i, 0)),
                  pl.BlockSpec((None, K, tn), lambda j, i: (layer, 0, j))],
        out_specs=pl.BlockSpec((tm, tn), lambda j, i: (i, j)),
        out_shape=jax.ShapeDtypeStruct((M, N), out_dtype),
        scratch_shapes=[pltpu.VMEM((K, tn), BF16)],
        compiler_params=_cparams("parallel", "arbitrary"),
        name=name,
    )(x, w)


HGRN_GROUP = 4
HGRN_ROWS = HGRN_GROUP * HGRN_CHUNK


def _hgrn_kernel(tab_ref, q_ref, ff_ref, fb_ref, i_ref, g_ref, ng_ref, o_ref,
                 ofw_ref, obw_ref, qef_ref, kef_ref, k2f_ref, qbf_ref, decf_ref,
                 qeb_ref, keb_ref, k2b_ref, qbb_ref, decb_ref, *, layer, seq):
    C, G, R, hd = HGRN_CHUNK, HGRN_GROUP, HGRN_ROWS, A_HEAD_DIM
    n_groups = seq // R
    tab = tab_ref[...]
    e = jnp.exp(tab - jnp.max(tab, axis=0, keepdims=True))
    sm = e / jnp.sum(e, axis=0, keepdims=True)
    lb = jnp.sum(sm[:layer + 1], axis=0, keepdims=True)

    row = lax.broadcasted_iota(jnp.int32, (R, R), 0)
    col = lax.broadcasted_iota(jnp.int32, (R, R), 1)
    same = (row // C) == (col // C)
    lower = same & (col <= row)
    upper = same & (col >= row)
    X = BF16_SUBLANES
    jrow = lax.broadcasted_iota(jnp.int32, (X, R), 0)
    jcol = lax.broadcasted_iota(jnp.int32, (X, R), 1)
    in_chunk = (jcol // C) == jrow
    last_rows = in_chunk.astype(BF16)
    mid_f = (in_chunk & (jcol % C <= C // 2 - 1)).astype(BF16)
    mid_b = (in_chunk & (jcol % C >= C // 2)).astype(BF16)
    cum_f = jnp.concatenate([lower.astype(BF16), mid_f, last_rows], axis=0)
    cum_b = jnp.concatenate([upper.astype(BF16), mid_b, last_rows], axis=0)

    for ref in (k2f_ref, qbf_ref, k2b_ref, qbb_ref):
        ref[...] = jnp.zeros(ref.shape, BF16)

    def per_chunk(extra, j0):
        return jnp.concatenate(
            [jnp.broadcast_to(extra[j0 + c:j0 + c + 1], (C, hd)) for c in range(G)], axis=0)

    def prepare(start, d, slot):
        f_ref, cum = d["f"], d["cum"]
        rows = pl.ds(start, R)
        f = lb + (1.0 - lb) * jax.nn.sigmoid(f_ref[rows, :])
        k = 1.0 - f
        lf = jnp.log(f)
        hi = lf.astype(BF16)
        r1 = lf - hi.astype(F32)
        mid = r1.astype(BF16)
        lo = (r1 - mid.astype(F32)).astype(BF16)
        yield
        ball = jnp.dot(cum, jnp.concatenate([hi, mid, lo], axis=1), preferred_element_type=F32)
        ball = ball[:, 0:hd] + ball[:, hd:2 * hd] + ball[:, 2 * hd:3 * hd]
        yield
        b = ball[0:R]
        b_mid = per_chunk(ball, R)
        b_last = per_chunk(ball, R + X)
        q = q_ref[rows, :]
        d["qe"][slot] = (q * jnp.exp(b - b_mid)).astype(BF16)
        d["ke"][slot] = (k * jnp.exp(b_mid - b)).astype(BF16)
        yield
        qb = (q * jnp.exp(b)).astype(BF16)
        k2 = (k * jnp.exp(b_last - b)).astype(BF16)
        for c in range(G):
            d["k2"][slot, c * C:(c + 1) * C, c * hd:(c + 1) * hd] = k2[c * C:(c + 1) * C]
            d["qb"][slot, c * C:(c + 1) * C, c * hd:(c + 1) * hd] = qb[c * C:(c + 1) * C]
        d["dec"][slot] = jnp.exp(ball[R + X:R + X + 8])
        yield

    def apply(start, d, slot, st):
        rows = pl.ds(start, R)
        v = i_ref[rows, :]
        sc = lax.dot_general(d["qe"][slot], d["ke"][slot], NT_DIMS, preferred_element_type=F32)
        sc = jnp.where(d["mask"], sc, 0.0)
        yield
        o = jnp.dot(sc.astype(BF16), v.astype(BF16), preferred_element_type=F32)
        kv_t = jnp.dot(v.T.astype(BF16), d["k2"][slot], preferred_element_type=F32)
        yield
        dec = d["dec"][slot]
        entering = [None] * G
        for c in (reversed(range(G)) if d["reverse"] else range(G)):
            entering[c] = st.astype(BF16)
            st = st * dec[c:c + 1] + kv_t[:, c * hd:(c + 1) * hd]
        yield
        o = o + lax.dot_general(d["qb"][slot], jnp.concatenate(entering, axis=1), NT_DIMS,
                                preferred_element_type=F32)
        d["out"][rows, :] = o
        return st

    def interleave(*gens):
        results = [None] * len(gens)
        active = list(enumerate(gens))
        while active:
            still = []
            for idx, g in active:
                try:
                    next(g)
                    still.append((idx, g))
                except StopIteration as done:
                    results[idx] = done.value
            active = still
        return results

    fw = dict(f=ff_ref, cum=cum_f, mask=lower, reverse=False, out=ofw_ref,
              qe=qef_ref, ke=kef_ref, k2=k2f_ref, qb=qbf_ref, dec=decf_ref)
    bw = dict(f=fb_ref, cum=cum_b, mask=upper, reverse=True, out=obw_ref,
              qe=qeb_ref, ke=keb_ref, k2=k2b_ref, qb=qbb_ref, dec=decb_ref)

    def starts(g):
        g = jnp.minimum(g, n_groups - 1)
        return pl.multiple_of(g * R, R), pl.multiple_of((n_groups - 1 - g) * R, R)

    s_f, s_b = starts(0)
    interleave(prepare(s_f, fw, 0), prepare(s_b, bw, 0))

    def body(n, carry):
        st_f, st_b = carry
        for slot in (0, 1):
            a_f, a_b = starts(2 * n + slot)
            p_f, p_b = starts(2 * n + slot + 1)
            _, _, st_f, st_b = interleave(prepare(p_f, fw, 1 - slot), prepare(p_b, bw, 1 - slot),
                                          apply(a_f, fw, slot, st_f), apply(a_b, bw, slot, st_b))
        return st_f, st_b

    zero = jnp.zeros((hd, hd), F32)
    lax.fori_loop(0, n_groups // 2, body, (zero, zero))

    ng = ng_ref[...]
    R = 256

    def fin(r, _):
        rows = pl.ds(pl.multiple_of(r * R, R), R)
        o = ofw_ref[rows, :] + obw_ref[rows, :]
        y = o * lax.rsqrt(jnp.mean(o * o, axis=-1, keepdims=True) + RMS_EPS) * ng
        g = g_ref[rows, :]
        o_ref[rows, :] = (y * (g * jax.nn.sigmoid(g))).astype(o_ref.dtype)
        return 0

    lax.fori_loop(0, seq // R, fin, 0)


def _hgrn(h0, lb_table, norm_g, layer, batch, seq):
    hd = A_HEAD_DIM

    def col(part):
        return pl.BlockSpec((None, seq, hd), lambda b, h: (b, 0, part * A_HEADS + h))

    n_tab = lb_table.shape[0]
    return pl.pallas_call(
        functools.partial(_hgrn_kernel, layer=layer, seq=seq),
        grid=(batch, A_HEADS),
        in_specs=[pl.BlockSpec((n_tab, hd), lambda b, h: (0, h)),
                  col(0), col(1), col(2), col(3), col(4),
                  pl.BlockSpec((1, hd), lambda b, h: (0, h))],
        out_specs=pl.BlockSpec((None, seq, hd), lambda b, h: (b, 0, h)),
        out_shape=jax.ShapeDtypeStruct((batch, seq, A_WIDTH), BF16),
        scratch_shapes=[pltpu.VMEM((seq, hd), F32), pltpu.VMEM((seq, hd), F32)]
        + 2 * [pltpu.VMEM((2, HGRN_ROWS, hd), BF16), pltpu.VMEM((2, HGRN_ROWS, hd), BF16),
               pltpu.VMEM((2, HGRN_ROWS, HGRN_GROUP * hd), BF16),
               pltpu.VMEM((2, HGRN_ROWS, HGRN_GROUP * hd), BF16),
               pltpu.VMEM((2, 8, hd), F32)],
        compiler_params=_cparams("parallel", "parallel"),
        name="hgrn2",
    )(lb_table, h0, h0, h0, h0, h0, norm_g)


def _gmlp_kernel(u_ref, v_ref, lg_ref, lbias_ref, ws_ref, bias_ref, o_ref, *, rows):
    vln = _layer_norm(v_ref[...], lg_ref[...], lbias_ref[...]).astype(BF16)
    for c in range(rows // B_CHUNK):
        r = slice(c * B_CHUNK, (c + 1) * B_CHUNK)
        for g in range(B_GROUPS):
            cs = slice(g * B_GROUP_DIM, (g + 1) * B_GROUP_DIM)
            s = jnp.dot(ws_ref[g], vln[r, cs], preferred_element_type=F32) + bias_ref[g]
            o_ref[r, cs] = (u_ref[r, cs] * s).astype(o_ref.dtype)


def _gmlp(h0, ln_g, ln_b, ws, bias, rows):
    M = h0.shape[0]
    u_blk = 5 * A_WIDTH // B_WIDTH
    bias_b = jnp.broadcast_to(bias[:, :, None], (B_GROUPS, B_CHUNK, B_GROUP_DIM))
    return pl.pallas_call(
        functools.partial(_gmlp_kernel, rows=rows),
        grid=(M // rows,),
        in_specs=[pl.BlockSpec((rows, B_WIDTH), lambda i: (i, u_blk)),
                  pl.BlockSpec((rows, B_WIDTH), lambda i: (i, u_blk + 1)),
                  pl.BlockSpec((1, B_WIDTH), lambda i: (0, 0)),
                  pl.BlockSpec((1, B_WIDTH), lambda i: (0, 0)),
                  pl.BlockSpec((B_GROUPS, B_CHUNK, B_CHUNK), lambda i: (0, 0, 0)),
                  pl.BlockSpec((B_GROUPS, B_CHUNK, B_GROUP_DIM), lambda i: (0, 0, 0))],
        out_specs=pl.BlockSpec((rows, B_WIDTH), lambda i: (i, 0)),
        out_shape=jax.ShapeDtypeStruct((M, B_WIDTH), BF16),
        compiler_params=_cparams("parallel"),
        name="gmlp",
    )(h0, h0, ln_g.reshape(1, -1), ln_b.reshape(1, -1), ws.astype(BF16), bias_b)


PROJ_STAGE_ROWS = 512


def _proj_ln_kernel(*refs, n_in, n_out, layer):
    ins = refs[:n_in]
    w_hbm, x_ref, g_ref, b_ref = refs[n_in:n_in + 4]
    out_refs = refs[n_in + 4:n_in + 4 + n_out]
    wb_ref, stage_ref, sem = refs[n_in + 4 + n_out:]
    K = wb_ref.shape[0]
    n_stage = K // PROJ_STAGE_ROWS

    @pl.when(pl.program_id(0) == 0)
    def _():
        def chunk_copy(c):
            rows = pl.ds(c * PROJ_STAGE_ROWS, PROJ_STAGE_ROWS)
            return pltpu.make_async_copy(w_hbm.at[layer, rows, :], stage_ref.at[c % 2], sem.at[c % 2])

        chunk_copy(0).start()
        for c in range(n_stage):
            if c + 1 < n_stage:
                chunk_copy(c + 1).start()
            chunk_copy(c).wait()
            wb_ref[c * PROJ_STAGE_ROWS:(c + 1) * PROJ_STAGE_ROWS, :] = stage_ref[c % 2].astype(BF16)

    acc = None
    off = 0
    for r in ins:
        kk = r.shape[1]
        d = jnp.dot(r[...], wb_ref[off:off + kk, :], preferred_element_type=F32)
        acc = d if acc is None else acc + d
        off += kk
    out = _layer_norm(ALPHA * x_ref[...] + acc, g_ref[...], b_ref[...])
    for o_ref in out_refs:
        o_ref[...] = out.astype(o_ref.dtype)


def _proj_ln(parts, w, layer, x, g, b, tm, out_dtypes, name):
    M, D = x.shape
    K = w.shape[1]
    assert K % PROJ_STAGE_ROWS == 0
    in_specs = [pl.BlockSpec((tm, p.shape[1]), lambda i: (i, 0)) for p in parts]
    in_specs += [pl.BlockSpec(memory_space=pl.ANY),
                 pl.BlockSpec((tm, D), lambda i: (i, 0)),
                 pl.BlockSpec((1, D), lambda i: (0, 0)),
                 pl.BlockSpec((1, D), lambda i: (0, 0))]
    return pl.pallas_call(
        functools.partial(_proj_ln_kernel, n_in=len(parts), n_out=len(out_dtypes), layer=layer),
        grid=(M // tm,),
        in_specs=in_specs,
        out_specs=[pl.BlockSpec((tm, D), lambda i: (i, 0)) for _ in out_dtypes],
        out_shape=[jax.ShapeDtypeStruct((M, D), dt) for dt in out_dtypes],
        scratch_shapes=[pltpu.VMEM((K, D), BF16),
                        pltpu.VMEM((2, PROJ_STAGE_ROWS, D), F32),
                        pltpu.SemaphoreType.DMA((2,))],
        compiler_params=_cparams("arbitrary"),
        name=name,
    )(*parts, w, x, g.reshape(1, -1), b.reshape(1, -1))


def _ffn_up_kernel(xm_ref, xp_ref, xn_ref, wg_ref, wv_ref, cwg_ref, cwv_ref, cbg_ref, cbv_ref,
                   o_ref, lhs_ref, wgb_ref, wvb_ref, hg_ref, hv_ref,
                   *, tm, seq, rows_per_col, n_tiles, n_col, n_k):
    H = BF16_SUBLANES
    s = pl.program_id(0)
    D = lhs_ref.shape[1]
    tn = o_ref.shape[1]
    cc = tn // n_col
    kc = D // n_k
    n_pieces = n_col * n_k
    rc = tm * n_col // n_pieces

    @pl.when(s == 0)
    def _():
        hg_ref[1] = jnp.zeros(hg_ref.shape[1:], F32)
        hv_ref[1] = jnp.zeros(hv_ref.shape[1:], F32)

    @pl.when((s % rows_per_col == 0) & (s < n_tiles))
    def _():
        wgb_ref[...] = wg_ref[...].astype(BF16)
        wvb_ref[...] = wv_ref[...].astype(BF16)

    t0 = ((jnp.minimum(s, n_tiles - 1) % rows_per_col) * tm) % seq
    zero = jnp.zeros((H, D), BF16)
    lhs_ref[0:H, :] = jnp.where(t0 != 0, xp_ref[...], zero)
    lhs_ref[H:H + tm, :] = xm_ref[...]
    lhs_ref[H + tm:2 * H + tm, :] = jnp.where(t0 + tm != seq, xn_ref[...], zero)

    def conv(h_ref, slot, r0, cols, cw, cb):
        h = h_ref[slot, r0 + H - 8:r0 + H + rc + 8, cols]
        return h[7:7 + rc] * cw[0:1] + h[8:8 + rc] * cw[1:2] + h[9:9 + rc] * cw[2:3] + cb

    def step(fill, drain):
        for p in range(n_pieces):
            c, k = divmod(p, n_k)
            cols = slice(c * cc, (c + 1) * cc)
            ks = slice(k * kc, (k + 1) * kc)
            lhs = lhs_ref[:, ks]
            for h_ref, w_ref in ((hg_ref, wgb_ref), (hv_ref, wvb_ref)):
                part = jnp.dot(lhs, w_ref[ks, cols], preferred_element_type=F32)
                if k == 0:
                    h_ref[fill, :, cols] = part
                else:
                    h_ref[fill, :, cols] += part
            r0 = k * rc
            gate = conv(hg_ref, drain, r0, cols, cwg_ref[:, cols], cbg_ref[:, cols])
            val = conv(hv_ref, drain, r0, cols, cwv_ref[:, cols], cbv_ref[:, cols])
            o_ref[r0:r0 + rc, cols] = (gate * jax.nn.sigmoid(gate) * val).astype(o_ref.dtype)

    @pl.when(s % 2 == 0)
    def _():
        step(0, 1)

    @pl.when(s % 2 == 1)
    def _():
        step(1, 0)


def _ffn_up(xb, w_up, conv_w, conv_b, layer, seq, tm, tn):
    M, D = xb.shape
    H = BF16_SUBLANES
    nj = D_FF // tn
    ni = M // tm
    hb = tm // H
    n_hblk = M // H
    n_tiles = nj * ni
    conv_b = conv_b.reshape(conv_b.shape[0], 1, -1)

    def fill(s):
        t = jnp.minimum(s, n_tiles - 1)
        return t // ni, t % ni

    def drain(s):
        t = jnp.maximum(s - 1, 0)
        return t // ni, t % ni

    return pl.pallas_call(
        functools.partial(_ffn_up_kernel, tm=tm, seq=seq, rows_per_col=ni, n_tiles=n_tiles, n_col=2, n_k=4),
        grid=(n_tiles + 1,),
        in_specs=[pl.BlockSpec((tm, D), lambda s: (fill(s)[1], 0)),
                  pl.BlockSpec((H, D), lambda s: (jnp.maximum(fill(s)[1] * hb - 1, 0), 0)),
                  pl.BlockSpec((H, D), lambda s: (jnp.minimum((fill(s)[1] + 1) * hb, n_hblk - 1), 0)),
                  pl.BlockSpec((None, D, tn), lambda s: (layer, 0, fill(s)[0])),
                  pl.BlockSpec((None, D, tn), lambda s: (layer, 0, fill(s)[0] + nj)),
                  pl.BlockSpec((None, 3, tn), lambda s: (layer, 0, drain(s)[0])),
                  pl.BlockSpec((None, 3, tn), lambda s: (layer, 0, drain(s)[0] + nj)),
                  pl.BlockSpec((None, 1, tn), lambda s: (layer, 0, drain(s)[0])),
                  pl.BlockSpec((None, 1, tn), lambda s: (layer, 0, drain(s)[0] + nj))],
        out_specs=pl.BlockSpec((tm, tn), lambda s: (drain(s)[1], drain(s)[0])),
        out_shape=jax.ShapeDtypeStruct((M, D_FF), BF16),
        scratch_shapes=[pltpu.VMEM((tm + 2 * H, D), BF16),
                        pltpu.VMEM((D, tn), BF16), pltpu.VMEM((D, tn), BF16),
                        pltpu.VMEM((2, tm + 2 * H, tn), F32), pltpu.VMEM((2, tm + 2 * H, tn), F32)],
        compiler_params=_cparams("arbitrary"),
        name="ffn_up_conv",
    )(xb, xb, xb, w_up, w_up, conv_w, conv_w, conv_b, conv_b)


def _rope_tables(seq):
    t = jnp.arange(seq, dtype=jnp.int32)
    r = (t // GRID_W).astype(F32)
    c = (t % GRID_W).astype(F32)
    half = C_HEAD_DIM // 2
    inv_freq = jnp.exp(-math.log(ROPE_THETA) * jnp.arange(0, half, 2, dtype=F32) / half)
    ang_r = r[:, None] * inv_freq
    ang_c = c[:, None] * inv_freq
    ang = jnp.concatenate([ang_r, ang_r, ang_c, ang_c], axis=-1)
    sign = jnp.tile(jnp.concatenate([-jnp.ones(half // 2, F32), jnp.ones(half // 2, F32)]), 2)
    return jnp.cos(ang), jnp.sin(ang) * sign


def _qkv_kernel(x_ref, w_ref, gain_ref, cos_ref, sin_ref, o_ref, wb_ref, *, n_norm_tiles):
    @pl.when(pl.program_id(1) == 0)
    def _():
        wb_ref[...] = w_ref[...].astype(BF16)

    acc = jnp.dot(x_ref[...], wb_ref[...], preferred_element_type=F32)
    hd = C_HEAD_DIM
    j = pl.program_id(0)

    @pl.when(j < n_norm_tiles)
    def _():
        cos = cos_ref[...]
        sin = sin_ref[...]
        lane = lax.broadcasted_iota(jnp.int32, (1, hd), 1)
        low = (lane % (hd // 2)) < (hd // 4)
        for h in range(acc.shape[1] // hd):
            cs = slice(h * hd, (h + 1) * hd)
            a = acc[:, cs]
            y = a * lax.rsqrt(jnp.mean(a * a, axis=-1, keepdims=True) + RMS_EPS) * gain_ref[:, cs]
            partner = jnp.where(low, pltpu.roll(y, hd - hd // 4, 1), pltpu.roll(y, hd // 4, 1))
            o_ref[:, cs] = (y * cos + partner * sin).astype(o_ref.dtype)

    @pl.when(j >= n_norm_tiles)
    def _():
        o_ref[...] = acc.astype(o_ref.dtype)


def _qkv_proj(xb, w, layer, gain, cos, sin, seq, tm, tn):
    M, D = xb.shape
    N = w.shape[2]
    n_norm_tiles = gain.shape[1] // tn
    n_pos_blk = seq // tm
    return pl.pallas_call(
        functools.partial(_qkv_kernel, n_norm_tiles=n_norm_tiles),
        grid=(N // tn, M // tm),
        in_specs=[pl.BlockSpec((tm, D), lambda j, i: (i, 0)),
                  pl.BlockSpec((None, D, tn), lambda j, i: (layer, 0, j)),
                  pl.BlockSpec((1, tn), lambda j, i: (0, jnp.minimum(j, n_norm_tiles - 1))),
                  pl.BlockSpec((tm, C_HEAD_DIM), lambda j, i: (i % n_pos_blk, 0)),
                  pl.BlockSpec((tm, C_HEAD_DIM), lambda j, i: (i % n_pos_blk, 0))],
        out_specs=pl.BlockSpec((tm, tn), lambda j, i: (i, j)),
        out_shape=jax.ShapeDtypeStruct((M, N), BF16),
        scratch_shapes=[pltpu.VMEM((D, tn), BF16)],
        compiler_params=_cparams("parallel", "arbitrary"),
        name="qkv_norm_rope",
    )(xb, w, gain, cos, sin)


def _attn_kernel(q_ref, k_ref, v_ref, o_ref, vt_ref, st_ref, m_ref, *, tq, blocks_per_head, n_chunks):
    hd = C_HEAD_DIM
    s = pl.program_id(0)
    seq = k_ref.shape[0]
    ck = seq // n_chunks

    @pl.when(s == 0)
    def _():
        st_ref[1] = jnp.zeros(st_ref.shape[1:], F32)
        m_ref[1] = jnp.zeros(m_ref.shape[1:], F32)

    @pl.when(jnp.maximum(s - 1, 0) % blocks_per_head == 0)
    def _():
        vt_ref[0:hd, :] = v_ref[...].astype(F32).T.astype(BF16)
        vt_ref[hd:, :] = jnp.ones((vt_ref.shape[0] - hd, seq), BF16)

    def step(fill, drain):
        q = jnp.concatenate([q_ref[:, g * hd:(g + 1) * hd] for g in range(C_GROUP)], axis=0)
        m_drain = m_ref[drain]
        m_fill = None
        acc = None
        for c in range(n_chunks):
            rows = slice(c * ck, (c + 1) * ck)
            sc = lax.dot_general(k_ref[rows, :], q, NT_DIMS, preferred_element_type=F32)
            st_ref[fill, rows, :] = sc
            mc = jnp.max(sc, axis=0, keepdims=True)
            m_fill = mc if c == 0 else jnp.maximum(m_fill, mc)
            p = jnp.exp2(st_ref[drain, rows, :] - m_drain).astype(BF16)
            pv = jnp.dot(vt_ref[:, rows], p, preferred_element_type=F32)
            acc = pv if c == 0 else acc + pv
        m_ref[fill] = m_fill
        o = acc[0:hd] * (1.0 / acc[hd:hd + 1])
        for g in range(C_GROUP):
            o_ref[:, g * hd:(g + 1) * hd] = o[:, g * tq:(g + 1) * tq].T.astype(o_ref.dtype)

    @pl.when(s % 2 == 0)
    def _():
        step(0, 1)

    @pl.when(s % 2 == 1)
    def _():
        step(1, 0)


def _attention(qkv, batch, seq, tq, n_chunks):
    hd = C_HEAD_DIM
    gw = C_GROUP * hd
    bph = seq // tq
    n_blocks = batch * C_KV_HEADS * bph

    def coords(blk):
        return blk // (C_KV_HEADS * bph), (blk // bph) % C_KV_HEADS, blk % bph

    def q_map(s):
        b, h, i = coords(jnp.minimum(s, n_blocks - 1))
        return b, i, h

    def k_map(s):
        b, h, _ = coords(jnp.minimum(s, n_blocks - 1))
        return b, 0, C_Q_HEADS + h

    def v_map(s):
        b, h, _ = coords(jnp.maximum(s - 1, 0))
        return b, 0, C_Q_HEADS + C_KV_HEADS + h

    def o_map(s):
        b, h, i = coords(jnp.maximum(s - 1, 0))
        return b, i, h

    return pl.pallas_call(
        functools.partial(_attn_kernel, tq=tq, blocks_per_head=bph, n_chunks=n_chunks),
        grid=(n_blocks + 1,),
        in_specs=[pl.BlockSpec((None, tq, gw), q_map),
                  pl.BlockSpec((None, seq, hd), k_map),
                  pl.BlockSpec((None, seq, hd), v_map)],
        out_specs=pl.BlockSpec((None, tq, gw), o_map),
        out_shape=jax.ShapeDtypeStruct((batch, seq, C_Q_HEADS * hd), BF16),
        scratch_shapes=[pltpu.VMEM((hd + BF16_SUBLANES, seq), BF16),
                        pltpu.VMEM((2, seq, C_GROUP * tq), F32),
                        pltpu.VMEM((2, 1, C_GROUP * tq), F32)],
        compiler_params=_cparams("arbitrary"),
        name="gqa_attention",
    )(qkv, qkv, qkv)


def _conv_ffn_ln(x, xb, w_up, conv_w, conv_b, w_down, g, b, layer, seq, out_dtypes):
    act = _ffn_up(xb, w_up, conv_w, conv_b, layer, seq, tm=1024, tn=512)
    return _proj_ln([act], w_down, layer, x, g[layer], b[layer], 256, out_dtypes, "ffn_down_ln")


def kernel(x, w_in_ab, hgrn_lb_table, hgrn_norm_g, gmlp_ln_g, gmlp_ln_b, gmlp_ws, gmlp_bias,
           w_out_ab, w_in_attn, q_norm_g, k_norm_g, w_out_attn, ffn_up, ffn_conv_w, ffn_conv_b,
           ffn_down, ln1_g, ln1_b, ln2_g, ln2_b):
    batch, seq, d = x.shape
    M = batch * seq
    x = x.reshape(M, d)
    xb = x.astype(BF16)

    h0 = _matmul(xb, w_in_ab, 0, 1024, 1024, F32, "inproj_ab")
    o_a = _hgrn(h0.reshape(batch, seq, -1), hgrn_lb_table, hgrn_norm_g[0].reshape(1, -1), 0, batch, seq)
    o_b = _gmlp(h0, gmlp_ln_g[0], gmlp_ln_b[0], gmlp_ws[0], gmlp_bias[0], rows=512)
    both = (F32, BF16)
    x, xb = _proj_ln([o_a.reshape(M, -1), o_b], w_out_ab, 0, x, ln1_g[0], ln1_b[0], 512, both, "outproj_ln")
    x, xb = _conv_ffn_ln(x, xb, ffn_up, ffn_conv_w, ffn_conv_b, ffn_down, ln2_g, ln2_b, 0, seq, both)

    scale = C_HEAD_DIM ** -0.5 * math.log2(math.e)
    gain = jnp.concatenate([jnp.tile(q_norm_g[0] * scale, C_Q_HEADS), jnp.tile(k_norm_g[0], C_KV_HEADS)])
    cos, sin = _rope_tables(seq)
    qkv = _qkv_proj(xb, w_in_attn, 0, gain.reshape(1, -1), cos, sin, seq, tm=1024, tn=512)
    att = _attention(qkv.reshape(batch, seq, -1), batch, seq, tq=128, n_chunks=8)
    x, xb = _proj_ln([att.reshape(M, -1)], w_out_attn, 0, x, ln1_g[1], ln1_b[1], 512, both, "outproj_ln")
    (x,) = _conv_ffn_ln(x, xb, ffn_up, ffn_conv_w, ffn_conv_b, ffn_down, ln2_g, ln2_b, 1, seq, (F32,))
    return x.reshape(batch, seq, d)
```

```python
import functools
import math

import jax
import jax.numpy as jnp
from jax import lax
from jax.experimental import pallas as pl
from jax.experimental.pallas import tpu as pltpu

F32 = jnp.float32
BF16 = jnp.bfloat16

D_MODEL = 2048
GRID_W = 64
A_HEAD_DIM = 128
A_WIDTH = D_MODEL // 2
A_HEADS = A_WIDTH // A_HEAD_DIM
HGRN_CHUNK = 64
B_WIDTH = D_MODEL // 2
B_GROUP_DIM = 128
B_GROUPS = B_WIDTH // B_GROUP_DIM
B_CHUNK = 128
C_HEAD_DIM = 128
C_Q_HEADS = D_MODEL // C_HEAD_DIM
C_KV_HEADS = C_Q_HEADS // 4
C_GROUP = C_Q_HEADS // C_KV_HEADS
ROPE_THETA = 10000.0
D_FF = 5632
DEPTH = 2
ALPHA = (2.0 * DEPTH) ** 0.25
LN_EPS = 1e-5
RMS_EPS = 1e-6

V7X_VMEM_LIMIT_BYTES = 56 * 1024 * 1024
BF16_SUBLANES = 16

NT_DIMS = (((1,), (1,)), ((), ()))


def _cparams(*sem):
    return pltpu.CompilerParams(dimension_semantics=sem, vmem_limit_bytes=V7X_VMEM_LIMIT_BYTES)


def _layer_norm(y, g, b):
    mu = jnp.mean(y, axis=-1, keepdims=True)
    yc = y - mu
    var = jnp.mean(yc * yc, axis=-1, keepdims=True)
    return yc * lax.rsqrt(var + LN_EPS) * g + b


def _mm_kernel(x_ref, w_ref, o_ref, wb_ref):
    @pl.when(pl.program_id(1) == 0)
    def _():
        wb_ref[...] = w_ref[...].astype(BF16)

    o_ref[...] = jnp.dot(x_ref[...], wb_ref[...], preferred_element_type=F32).astype(o_ref.dtype)


def _matmul(x, w, layer, tm, tn, out_dtype, name):
    M, K = x.shape
    N = w.shape[2]
    return pl.pallas_call(
        _mm_kernel,
        grid=(N // tn, M // tm),
        in_specs=[pl.BlockSpec((tm, K), lambda j, i: (i, 0)),
                  pl.BlockSpec((None, K, tn), lambda j, i: (layer, 0, j))],
        out_specs=pl.BlockSpec((tm, tn), lambda j, i: (i, j)),
        out_shape=jax.ShapeDtypeStruct((M, N), out_dtype),
        scratch_shapes=[pltpu.VMEM((K, tn), BF16)],
        compiler_params=_cparams("parallel", "arbitrary"),
        name=name,
    )(x, w)


HGRN_GROUP = 4
HGRN_ROWS = HGRN_GROUP * HGRN_CHUNK


def _hgrn_kernel(tab_ref, q_ref, ff_ref, fb_ref, i_ref, g_ref, ng_ref, o_ref,
                 ofw_ref, obw_ref, qef_ref, kef_ref, k2f_ref, qbf_ref, decf_ref,
                 qeb_ref, keb_ref, k2b_ref, qbb_ref, decb_ref, *, layer, seq):
    C, G, R, hd = HGRN_CHUNK, HGRN_GROUP, HGRN_ROWS, A_HEAD_DIM
    n_groups = seq // R
    tab = tab_ref[...]
    e = jnp.exp(tab - jnp.max(tab, axis=0, keepdims=True))
    sm = e / jnp.sum(e, axis=0, keepdims=True)
    lb = jnp.sum(sm[:layer + 1], axis=0, keepdims=True)

    row = lax.broadcasted_iota(jnp.int32, (R, R), 0)
    col = lax.broadcasted_iota(jnp.int32, (R, R), 1)
    same = (row // C) == (col // C)
    lower = same & (col <= row)
    upper = same & (col >= row)
    X = BF16_SUBLANES
    jrow = lax.broadcasted_iota(jnp.int32, (X, R), 0)
    jcol = lax.broadcasted_iota(jnp.int32, (X, R), 1)
    in_chunk = (jcol // C) == jrow
    last_rows = in_chunk.astype(BF16)
    mid_f = (in_chunk & (jcol % C <= C // 2 - 1)).astype(BF16)
    mid_b = (in_chunk & (jcol % C >= C // 2)).astype(BF16)
    cum_f = jnp.concatenate([lower.astype(BF16), mid_f, last_rows], axis=0)
    cum_b = jnp.concatenate([upper.astype(BF16), mid_b, last_rows], axis=0)

    for ref in (k2f_ref, qbf_ref, k2b_ref, qbb_ref):
        ref[...] = jnp.zeros(ref.shape, BF16)

    def per_chunk(extra, j0):
        return jnp.concatenate(
            [jnp.broadcast_to(extra[j0 + c:j0 + c + 1], (C, hd)) for c in range(G)], axis=0)

    def prepare(start, d, slot):
        f_ref, cum = d["f"], d["cum"]
        rows = pl.ds(start, R)
        f = lb + (1.0 - lb) * jax.nn.sigmoid(f_ref[rows, :])
        k = 1.0 - f
        lf = jnp.log(f)
        hi = lf.astype(BF16)
        r1 = lf - hi.astype(F32)
        mid = r1.astype(BF16)
        lo = (r1 - mid.astype(F32)).astype(BF16)
        yield
        ball = jnp.dot(cum, jnp.concatenate([hi, mid, lo], axis=1), preferred_element_type=F32)
        ball = ball[:, 0:hd] + ball[:, hd:2 * hd] + ball[:, 2 * hd:3 * hd]
        yield
        b = ball[0:R]
        b_mid = per_chunk(ball, R)
        b_last = per_chunk(ball, R + X)
        q = q_ref[rows, :]
        d["qe"][slot] = (q * jnp.exp(b - b_mid)).astype(BF16)
        d["ke"][slot] = (k * jnp.exp(b_mid - b)).astype(BF16)
        yield
        qb = (q * jnp.exp(b)).astype(BF16)
        k2 = (k * jnp.exp(b_last - b)).astype(BF16)
        for c in range(G):
            d["k2"][slot, c * C:(c + 1) * C, c * hd:(c + 1) * hd] = k2[c * C:(c + 1) * C]
            d["qb"][slot, c * C:(c + 1) * C, c * hd:(c + 1) * hd] = qb[c * C:(c + 1) * C]
        d["dec"][slot] = jnp.exp(ball[R + X:R + X + 8])
        yield

    def apply(start, d, slot, st):
        rows = pl.ds(start, R)
        v = i_ref[rows, :]
        sc = lax.dot_general(d["qe"][slot], d["ke"][slot], NT_DIMS, preferred_element_type=F32)
        sc = jnp.where(d["mask"], sc, 0.0)
        yield
        o = jnp.dot(sc.astype(BF16), v.astype(BF16), preferred_element_type=F32)
        kv_t = jnp.dot(v.T.astype(BF16), d["k2"][slot], preferred_element_type=F32)
        yield
        dec = d["dec"][slot]
        entering = [None] * G
        for c in (reversed(range(G)) if d["reverse"] else range(G)):
            entering[c] = st.astype(BF16)
            st = st * dec[c:c + 1] + kv_t[:, c * hd:(c + 1) * hd]
        yield
        o = o + lax.dot_general(d["qb"][slot], jnp.concatenate(entering, axis=1), NT_DIMS,
                                preferred_element_type=F32)
        d["out"][rows, :] = o
        return st

    def interleave(*gens):
        results = [None] * len(gens)
        active = list(enumerate(gens))
        while active:
            still = []
            for idx, g in active:
                try:
                    next(g)
                    still.append((idx, g))
                except StopIteration as done:
                    results[idx] = done.value
            active = still
        return results

    fw = dict(f=ff_ref, cum=cum_f, mask=lower, reverse=False, out=ofw_ref,
              qe=qef_ref, ke=kef_ref, k2=k2f_ref, qb=qbf_ref, dec=decf_ref)
    bw = dict(f=fb_ref, cum=cum_b, mask=upper, reverse=True, out=obw_ref,
              qe=qeb_ref, ke=keb_ref, k2=k2b_ref, qb=qbb_ref, dec=decb_ref)

    def starts(g):
        g = jnp.minimum(g, n_groups - 1)
        return pl.multiple_of(g * R, R), pl.multiple_of((n_groups - 1 - g) * R, R)

    s_f, s_b = starts(0)
    interleave(prepare(s_f, fw, 0), prepare(s_b, bw, 0))

    def body(n, carry):
        st_f, st_b = carry
        for slot in (0, 1):
            a_f, a_b = starts(2 * n + slot)
            p_f, p_b = starts(2 * n + slot + 1)
            _, _, st_f, st_b = interleave(prepare(p_f, fw, 1 - slot), prepare(p_b, bw, 1 - slot),
                                          apply(a_f, fw, slot, st_f), apply(a_b, bw, slot, st_b))
        return st_f, st_b

    zero = jnp.zeros((hd, hd), F32)
    lax.fori_loop(0, n_groups // 2, body, (zero, zero))

    ng = ng_ref[...]
    R = 256

    def fin(r, _):
        rows = pl.ds(pl.multiple_of(r * R, R), R)
        o = ofw_ref[rows, :] + obw_ref[rows, :]
        y = o * lax.rsqrt(jnp.mean(o * o, axis=-1, keepdims=True) + RMS_EPS) * ng
        g = g_ref[rows, :]
        o_ref[rows, :] = (y * (g * jax.nn.sigmoid(g))).astype(o_ref.dtype)
        return 0

    lax.fori_loop(0, seq // R, fin, 0)


def _hgrn(h0, lb_table, norm_g, layer, batch, seq):
    hd = A_HEAD_DIM

    def col(part):
        return pl.BlockSpec((None, seq, hd), lambda b, h: (b, 0, part * A_HEADS + h))

    n_tab = lb_table.shape[0]
    return pl.pallas_call(
        functools.partial(_hgrn_kernel, layer=layer, seq=seq),
        grid=(batch, A_HEADS),
        in_specs=[pl.BlockSpec((n_tab, hd), lambda b, h: (0, h)),
                  col(0), col(1), col(2), col(3), col(4),
                  pl.BlockSpec((1, hd), lambda b, h: (0, h))],
        out_specs=pl.BlockSpec((None, seq, hd), lambda b, h: (b, 0, h)),
        out_shape=jax.ShapeDtypeStruct((batch, seq, A_WIDTH), BF16),
        scratch_shapes=[pltpu.VMEM((seq, hd), F32), pltpu.VMEM((seq, hd), F32)]
        + 2 * [pltpu.VMEM((2, HGRN_ROWS, hd), BF16), pltpu.VMEM((2, HGRN_ROWS, hd), BF16),
               pltpu.VMEM((2, HGRN_ROWS, HGRN_GROUP * hd), BF16),
               pltpu.VMEM((2, HGRN_ROWS, HGRN_GROUP * hd), BF16),
               pltpu.VMEM((2, 8, hd), F32)],
        compiler_params=_cparams("parallel", "parallel"),
        name="hgrn2",
    )(lb_table, h0, h0, h0, h0, h0, norm_g)


def _gmlp_kernel(u_ref, v_ref, lg_ref, lbias_ref, ws_ref, bias_ref, o_ref, *, rows):
    vln = _layer_norm(v_ref[...], lg_ref[...], lbias_ref[...]).astype(BF16)
    for c in range(rows // B_CHUNK):
        r = slice(c * B_CHUNK, (c + 1) * B_CHUNK)
        for g in range(B_GROUPS):
            cs = slice(g * B_GROUP_DIM, (g + 1) * B_GROUP_DIM)
            s = jnp.dot(ws_ref[g], vln[r, cs], preferred_element_type=F32) + bias_ref[g]
            o_ref[r, cs] = (u_ref[r, cs] * s).astype(o_ref.dtype)


def _gmlp(h0, ln_g, ln_b, ws, bias, rows):
    M = h0.shape[0]
    u_blk = 5 * A_WIDTH // B_WIDTH
    bias_b = jnp.broadcast_to(bias[:, :, None], (B_GROUPS, B_CHUNK, B_GROUP_DIM))
    return pl.pallas_call(
        functools.partial(_gmlp_kernel, rows=rows),
        grid=(M // rows,),
        in_specs=[pl.BlockSpec((rows, B_WIDTH), lambda i: (i, u_blk)),
                  pl.BlockSpec((rows, B_WIDTH), lambda i: (i, u_blk + 1)),
                  pl.BlockSpec((1, B_WIDTH), lambda i: (0, 0)),
                  pl.BlockSpec((1, B_WIDTH), lambda i: (0, 0)),
                  pl.BlockSpec((B_GROUPS, B_CHUNK, B_CHUNK), lambda i: (0, 0, 0)),
                  pl.BlockSpec((B_GROUPS, B_CHUNK, B_GROUP_DIM), lambda i: (0, 0, 0))],
        out_specs=pl.BlockSpec((rows, B_WIDTH), lambda i: (i, 0)),
        out_shape=jax.ShapeDtypeStruct((M, B_WIDTH), BF16),
        compiler_params=_cparams("parallel"),
        name="gmlp",
    )(h0, h0, ln_g.reshape(1, -1), ln_b.reshape(1, -1), ws.astype(BF16), bias_b)


PROJ_STAGE_ROWS = 512


def _proj_ln_kernel(*refs, n_in, n_out, layer):
    ins = refs[:n_in]
    w_hbm, x_ref, g_ref, b_ref = refs[n_in:n_in + 4]
    out_refs = refs[n_in + 4:n_in + 4 + n_out]
    wb_ref, stage_ref, sem = refs[n_in + 4 + n_out:]
    K = wb_ref.shape[0]
    n_stage = K // PROJ_STAGE_ROWS

    @pl.when(pl.program_id(0) == 0)
    def _():
        def chunk_copy(c):
            rows = pl.ds(c * PROJ_STAGE_ROWS, PROJ_STAGE_ROWS)
            return pltpu.make_async_copy(w_hbm.at[layer, rows, :], stage_ref.at[c % 2], sem.at[c % 2])

        chunk_copy(0).start()
        for c in range(n_stage):
            if c + 1 < n_stage:
                chunk_copy(c + 1).start()
            chunk_copy(c).wait()
            wb_ref[c * PROJ_STAGE_ROWS:(c + 1) * PROJ_STAGE_ROWS, :] = stage_ref[c % 2].astype(BF16)

    acc = None
    off = 0
    for r in ins:
        kk = r.shape[1]
        d = jnp.dot(r[...], wb_ref[off:off + kk, :], preferred_element_type=F32)
        acc = d if acc is None else acc + d
        off += kk
    out = _layer_norm(ALPHA * x_ref[...] + acc, g_ref[...], b_ref[...])
    for o_ref in out_refs:
        o_ref[...] = out.astype(o_ref.dtype)


def _proj_ln(parts, w, layer, x, g, b, tm, out_dtypes, name):
    M, D = x.shape
    K = w.shape[1]
    assert K % PROJ_STAGE_ROWS == 0
    in_specs = [pl.BlockSpec((tm, p.shape[1]), lambda i: (i, 0)) for p in parts]
    in_specs += [pl.BlockSpec(memory_space=pl.ANY),
                 pl.BlockSpec((tm, D), lambda i: (i, 0)),
                 pl.BlockSpec((1, D), lambda i: (0, 0)),
                 pl.BlockSpec((1, D), lambda i: (0, 0))]
    return pl.pallas_call(
        functools.partial(_proj_ln_kernel, n_in=len(parts), n_out=len(out_dtypes), layer=layer),
        grid=(M // tm,),
        in_specs=in_specs,
        out_specs=[pl.BlockSpec((tm, D), lambda i: (i, 0)) for _ in out_dtypes],
        out_shape=[jax.ShapeDtypeStruct((M, D), dt) for dt in out_dtypes],
        scratch_shapes=[pltpu.VMEM((K, D), BF16),
                        pltpu.VMEM((2, PROJ_STAGE_ROWS, D), F32),
                        pltpu.SemaphoreType.DMA((2,))],
        compiler_params=_cparams("arbitrary"),
        name=name,
    )(*parts, w, x, g.reshape(1, -1), b.reshape(1, -1))


def _ffn_up_kernel(xm_ref, xp_ref, xn_ref, wg_ref, wv_ref, cwg_ref, cwv_ref, cbg_ref, cbv_ref,
                   o_ref, lhs_ref, wgb_ref, wvb_ref, *, tm, seq):
    H = BF16_SUBLANES
    i = pl.program_id(1)

    @pl.when(i == 0)
    def _():
        wgb_ref[...] = wg_ref[...].astype(BF16)
        wvb_ref[...] = wv_ref[...].astype(BF16)

    t0 = (i * tm) % seq
    zero = jnp.zeros((H, xm_ref.shape[1]), BF16)
    lhs_ref[0:H, :] = jnp.where(t0 != 0, xp_ref[...], zero)
    lhs_ref[H:H + tm, :] = xm_ref[...]
    lhs_ref[H + tm:2 * H + tm, :] = jnp.where(t0 + tm != seq, xn_ref[...], zero)
    lhs = lhs_ref[...]

    def conv(w_ref, cw_ref, cb_ref):
        h = jnp.dot(lhs, w_ref[...], preferred_element_type=F32)
        cw = cw_ref[...]
        return (h[H - 1:H - 1 + tm] * cw[0:1] + h[H:H + tm] * cw[1:2]
                + h[H + 1:H + 1 + tm] * cw[2:3] + cb_ref[...])

    gate = conv(wgb_ref, cwg_ref, cbg_ref)
    val = conv(wvb_ref, cwv_ref, cbv_ref)
    o_ref[...] = (gate * jax.nn.sigmoid(gate) * val).astype(o_ref.dtype)


def _ffn_up(xb, w_up, conv_w, conv_b, layer, seq, tm, tn):
    M, D = xb.shape
    H = BF16_SUBLANES
    nj = D_FF // tn
    hb = tm // H
    n_hblk = M // H
    conv_b = conv_b.reshape(conv_b.shape[0], 1, -1)
    return pl.pallas_call(
        functools.partial(_ffn_up_kernel, tm=tm, seq=seq),
        grid=(nj, M // tm),
        in_specs=[pl.BlockSpec((tm, D), lambda j, i: (i, 0)),
                  pl.BlockSpec((H, D), lambda j, i: (jnp.maximum(i * hb - 1, 0), 0)),
                  pl.BlockSpec((H, D), lambda j, i: (jnp.minimum((i + 1) * hb, n_hblk - 1), 0)),
                  pl.BlockSpec((None, D, tn), lambda j, i: (layer, 0, j)),
                  pl.BlockSpec((None, D, tn), lambda j, i: (layer, 0, j + nj)),
                  pl.BlockSpec((None, 3, tn), lambda j, i: (layer, 0, j)),
                  pl.BlockSpec((None, 3, tn), lambda j, i: (layer, 0, j + nj)),
                  pl.BlockSpec((None, 1, tn), lambda j, i: (layer, 0, j)),
                  pl.BlockSpec((None, 1, tn), lambda j, i: (layer, 0, j + nj))],
        out_specs=pl.BlockSpec((tm, tn), lambda j, i: (i, j)),
        out_shape=jax.ShapeDtypeStruct((M, D_FF), BF16),
        scratch_shapes=[pltpu.VMEM((tm + 2 * H, D), BF16),
                        pltpu.VMEM((D, tn), BF16), pltpu.VMEM((D, tn), BF16)],
        compiler_params=_cparams("parallel", "arbitrary"),
        name="ffn_up_conv",
    )(xb, xb, xb, w_up, w_up, conv_w, conv_w, conv_b, conv_b)


def _rope_tables(seq):
    t = jnp.arange(seq, dtype=jnp.int32)
    r = (t // GRID_W).astype(F32)
    c = (t % GRID_W).astype(F32)
    half = C_HEAD_DIM // 2
    inv_freq = jnp.exp(-math.log(ROPE_THETA) * jnp.arange(0, half, 2, dtype=F32) / half)
    ang_r = r[:, None] * inv_freq
    ang_c = c[:, None] * inv_freq
    ang = jnp.concatenate([ang_r, ang_r, ang_c, ang_c], axis=-1)
    sign = jnp.tile(jnp.concatenate([-jnp.ones(half // 2, F32), jnp.ones(half // 2, F32)]), 2)
    return jnp.cos(ang), jnp.sin(ang) * sign


def _qkv_kernel(x_ref, w_ref, gain_ref, cos_ref, sin_ref, o_ref, wb_ref, h_ref,
                *, rows_per_col, n_tiles, n_norm_tiles):
    s = pl.program_id(0)
    hd = C_HEAD_DIM

    @pl.when(s == 0)
    def _():
        h_ref[1] = jnp.zeros(h_ref.shape[1:], F32)

    @pl.when((s % rows_per_col == 0) & (s < n_tiles))
    def _():
        wb_ref[...] = w_ref[...].astype(BF16)

    plain = (jnp.maximum(s - 1, 0) // rows_per_col) >= n_norm_tiles

    def step(fill, drain):
        cos = cos_ref[...]
        sin = sin_ref[...]
        lane = lax.broadcasted_iota(jnp.int32, (1, hd), 1)
        low = (lane % (hd // 2)) < (hd // 4)
        for h in range(o_ref.shape[1] // hd):
            cs = slice(h * hd, (h + 1) * hd)
            a = h_ref[drain, :, cs]
            y = a * lax.rsqrt(jnp.mean(a * a, axis=-1, keepdims=True) + RMS_EPS) * gain_ref[:, cs]
            partner = jnp.where(low, pltpu.roll(y, hd - hd // 4, 1), pltpu.roll(y, hd // 4, 1))
            o_ref[:, cs] = jnp.where(plain, a, y * cos + partner * sin).astype(o_ref.dtype)
        h_ref[fill] = jnp.dot(x_ref[...], wb_ref[...], preferred_element_type=F32)

    @pl.when(s % 2 == 0)
    def _():
        step(0, 1)

    @pl.when(s % 2 == 1)
    def _():
        step(1, 0)


def _qkv_proj(xb, w, layer, gain, cos, sin, seq, tm, tn):
    M, D = xb.shape
    N = w.shape[2]
    n_norm_tiles = gain.shape[1] // tn
    n_pos_blk = seq // tm
    ni = M // tm
    n_tiles = (N // tn) * ni

    def fill(s):
        t = jnp.minimum(s, n_tiles - 1)
        return t // ni, t % ni

    def drain(s):
        t = jnp.maximum(s - 1, 0)
        return t // ni, t % ni

    return pl.pallas_call(
        functools.partial(_qkv_kernel, rows_per_col=ni, n_tiles=n_tiles, n_norm_tiles=n_norm_tiles),
        grid=(n_tiles + 1,),
        in_specs=[pl.BlockSpec((tm, D), lambda s: (fill(s)[1], 0)),
                  pl.BlockSpec((None, D, tn), lambda s: (layer, 0, fill(s)[0])),
                  pl.BlockSpec((1, tn), lambda s: (0, jnp.minimum(drain(s)[0], n_norm_tiles - 1))),
                  pl.BlockSpec((tm, C_HEAD_DIM), lambda s: (drain(s)[1] % n_pos_blk, 0)),
                  pl.BlockSpec((tm, C_HEAD_DIM), lambda s: (drain(s)[1] % n_pos_blk, 0))],
        out_specs=pl.BlockSpec((tm, tn), lambda s: (drain(s)[1], drain(s)[0])),
        out_shape=jax.ShapeDtypeStruct((M, N), BF16),
        scratch_shapes=[pltpu.VMEM((D, tn), BF16), pltpu.VMEM((2, tm, tn), F32)],
        compiler_params=_cparams("arbitrary"),
        name="qkv_norm_rope",
    )(xb, w, gain, cos, sin)


def _attn_kernel(q_ref, k_ref, v_ref, o_ref, vt_ref, st_ref, m_ref, *, tq, blocks_per_head, n_chunks):
    hd = C_HEAD_DIM
    s = pl.program_id(0)
    seq = k_ref.shape[0]
    ck = seq // n_chunks

    @pl.when(s == 0)
    def _():
        st_ref[1] = jnp.zeros(st_ref.shape[1:], F32)
        m_ref[1] = jnp.zeros(m_ref.shape[1:], F32)

    @pl.when(jnp.maximum(s - 1, 0) % blocks_per_head == 0)
    def _():
        vt_ref[0:hd, :] = v_ref[...].astype(F32).T.astype(BF16)
        vt_ref[hd:, :] = jnp.ones((vt_ref.shape[0] - hd, seq), BF16)

    def step(fill, drain):
        q = jnp.concatenate([q_ref[:, g * hd:(g + 1) * hd] for g in range(C_GROUP)], axis=0)
        m_drain = m_ref[drain]
        m_fill = None
        acc = None
        for c in range(n_chunks):
            rows = slice(c * ck, (c + 1) * ck)
            sc = lax.dot_general(k_ref[rows, :], q, NT_DIMS, preferred_element_type=F32)
            st_ref[fill, rows, :] = sc
            mc = jnp.max(sc, axis=0, keepdims=True)
            m_fill = mc if c == 0 else jnp.maximum(m_fill, mc)
            p = jnp.exp2(st_ref[drain, rows, :] - m_drain).astype(BF16)
            pv = jnp.dot(vt_ref[:, rows], p, preferred_element_type=F32)
            acc = pv if c == 0 else acc + pv
        m_ref[fill] = m_fill
        o = acc[0:hd] * (1.0 / acc[hd:hd + 1])
        for g in range(C_GROUP):
            o_ref[:, g * hd:(g + 1) * hd] = o[:, g * tq:(g + 1) * tq].T.astype(o_ref.dtype)

    @pl.when(s % 2 == 0)
    def _():
        step(0, 1)

    @pl.when(s % 2 == 1)
    def _():
        step(1, 0)


def _attention(qkv, batch, seq, tq, n_chunks):
    hd = C_HEAD_DIM
    gw = C_GROUP * hd
    bph = seq // tq
    n_blocks = batch * C_KV_HEADS * bph

    def coords(blk):
        return blk // (C_KV_HEADS * bph), (blk // bph) % C_KV_HEADS, blk % bph

    def q_map(s):
        b, h, i = coords(jnp.minimum(s, n_blocks - 1))
        return b, i, h

    def k_map(s):
        b, h, _ = coords(jnp.minimum(s, n_blocks - 1))
        return b, 0, C_Q_HEADS + h

    def v_map(s):
        b, h, _ = coords(jnp.maximum(s - 1, 0))
        return b, 0, C_Q_HEADS + C_KV_HEADS + h

    def o_map(s):
        b, h, i = coords(jnp.maximum(s - 1, 0))
        return b, i, h

    return pl.pallas_call(
        functools.partial(_attn_kernel, tq=tq, blocks_per_head=bph, n_chunks=n_chunks),
        grid=(n_blocks + 1,),
        in_specs=[pl.BlockSpec((None, tq, gw), q_map),
                  pl.BlockSpec((None, seq, hd), k_map),
                  pl.BlockSpec((None, seq, hd), v_map)],
        out_specs=pl.BlockSpec((None, tq, gw), o_map),
        out_shape=jax.ShapeDtypeStruct((batch, seq, C_Q_HEADS * hd), BF16),
        scratch_shapes=[pltpu.VMEM((hd + BF16_SUBLANES, seq), BF16),
                        pltpu.VMEM((2, seq, C_GROUP * tq), F32),
                        pltpu.VMEM((2, 1, C_GROUP * tq), F32)],
        compiler_params=_cparams("arbitrary"),
        name="gqa_attention",
    )(qkv, qkv, qkv)


def _conv_ffn_ln(x, xb, w_up, conv_w, conv_b, w_down, g, b, layer, seq, out_dtypes):
    act = _ffn_up(xb, w_up, conv_w, conv_b, layer, seq, tm=1024, tn=512)
    return _proj_ln([act], w_down, layer, x, g[layer], b[layer], 256, out_dtypes, "ffn_down_ln")


def kernel(x, w_in_ab, hgrn_lb_table, hgrn_norm_g, gmlp_ln_g, gmlp_ln_b, gmlp_ws, gmlp_bias,
           w_out_ab, w_in_attn, q_norm_g, k_norm_g, w_out_attn, ffn_up, ffn_conv_w, ffn_conv_b,
           ffn_down, ln1_g, ln1_b, ln2_g, ln2_b):
    batch, seq, d = x.shape
    M = batch * seq
    x = x.reshape(M, d)
    xb = x.astype(BF16)

    h0 = _matmul(xb, w_in_ab, 0, 1024, 1024, F32, "inproj_ab")
    o_a = _hgrn(h0.reshape(batch, seq, -1), hgrn_lb_table, hgrn_norm_g[0].reshape(1, -1), 0, batch, seq)
    o_b = _gmlp(h0, gmlp_ln_g[0], gmlp_ln_b[0], gmlp_ws[0], gmlp_bias[0], rows=512)
    both = (F32, BF16)
    x, xb = _proj_ln([o_a.reshape(M, -1), o_b], w_out_ab, 0, x, ln1_g[0], ln1_b[0], 512, both, "outproj_ln")
    x, xb = _conv_ffn_ln(x, xb, ffn_up, ffn_conv_w, ffn_conv_b, ffn_down, ln2_g, ln2_b, 0, seq, both)

    scale = C_HEAD_DIM ** -0.5 * math.log2(math.e)
    gain = jnp.concatenate([jnp.tile(q_norm_g[0] * scale, C_Q_HEADS), jnp.tile(k_norm_g[0], C_KV_HEADS)])
    cos, sin = _rope_tables(seq)
    qkv = _qkv_proj(xb, w_in_attn, 0, gain.reshape(1, -1), cos, sin, seq, tm=1024, tn=512)
    att = _attention(qkv.reshape(batch, seq, -1), batch, seq, tq=256, n_chunks=8)
    x, xb = _proj_ln([att.reshape(M, -1)], w_out_attn, 0, x, ln1_g[1], ln1_b[1], 512, both, "outproj_ln")
    (x,) = _conv_ffn_ln(x, xb, ffn_up, ffn_conv_w, ffn_conv_b, ffn_down, ln2_g, ln2_b, 1, seq, (F32,))
    return x.reshape(batch, seq, d)
```

```python
import functools
import math

import jax
import jax.numpy as jnp
from jax import lax
from jax.experimental import pallas as pl
from jax.experimental.pallas import tpu as pltpu

F32 = jnp.float32
BF16 = jnp.bfloat16

D_MODEL = 2048
GRID_W = 64
A_HEAD_DIM = 128
A_WIDTH = D_MODEL // 2
A_HEADS = A_WIDTH // A_HEAD_DIM
HGRN_CHUNK = 64
B_WIDTH = D_MODEL // 2
B_GROUP_DIM = 128
B_GROUPS = B_WIDTH // B_GROUP_DIM
B_CHUNK = 128
C_HEAD_DIM = 128
C_Q_HEADS = D_MODEL // C_HEAD_DIM
C_KV_HEADS = C_Q_HEADS // 4
C_GROUP = C_Q_HEADS // C_KV_HEADS
ROPE_THETA = 10000.0
D_FF = 5632
DEPTH = 2
ALPHA = (2.0 * DEPTH) ** 0.25
LN_EPS = 1e-5
RMS_EPS = 1e-6

V7X_VMEM_LIMIT_BYTES = 56 * 1024 * 1024
BF16_SUBLANES = 16

NT_DIMS = (((1,), (1,)), ((), ()))


def _cparams(*sem):
    return pltpu.CompilerParams(dimension_semantics=sem, vmem_limit_bytes=V7X_VMEM_LIMIT_BYTES)


def _layer_norm(y, g, b):
    mu = jnp.mean(y, axis=-1, keepdims=True)
    yc = y - mu
    var = jnp.mean(yc * yc, axis=-1, keepdims=True)
    return yc * lax.rsqrt(var + LN_EPS) * g + b


def _mm_kernel(x_ref, w_ref, o_ref, wb_ref):
    @pl.when(pl.program_id(1) == 0)
    def _():
        wb_ref[...] = w_ref[...].astype(BF16)

    o_ref[...] = jnp.dot(x_ref[...], wb_ref[...], preferred_element_type=F32).astype(o_ref.dtype)


def _matmul(x, w, layer, n_col_tiles, col_tile_of, tm, tn, out_dtype, name):
    M, K = x.shape
    return pl.pallas_call(
        _mm_kernel,
        grid=(n_col_tiles, M // tm),
        in_specs=[pl.BlockSpec((tm, K), lambda j, i: (i, 0)),
                  pl.BlockSpec((None, K, tn), lambda j, i: (layer, 0, col_tile_of(j)))],
        out_specs=pl.BlockSpec((tm, tn), lambda j, i: (i, j)),
        out_shape=jax.ShapeDtypeStruct((M, n_col_tiles * tn), out_dtype),
        scratch_shapes=[pltpu.VMEM((K, tn), BF16)],
        compiler_params=_cparams("parallel", "arbitrary"),
        name=name,
    )(x, w)


HGRN_GROUP = 4
HGRN_ROWS = HGRN_GROUP * HGRN_CHUNK


def _hgrn_kernel(tab_ref, q_ref, ff_ref, fb_ref, i_ref, g_ref, ng_ref, o_ref,
                 ofw_ref, obw_ref, qef_ref, kef_ref, k2f_ref, qbf_ref, decf_ref,
                 qeb_ref, keb_ref, k2b_ref, qbb_ref, decb_ref, *, layer, seq):
    C, G, R, hd = HGRN_CHUNK, HGRN_GROUP, HGRN_ROWS, A_HEAD_DIM
    n_groups = seq // R
    tab = tab_ref[...]
    e = jnp.exp(tab - jnp.max(tab, axis=0, keepdims=True))
    sm = e / jnp.sum(e, axis=0, keepdims=True)
    lb = jnp.sum(sm[:layer + 1], axis=0, keepdims=True)

    row = lax.broadcasted_iota(jnp.int32, (R, R), 0)
    col = lax.broadcasted_iota(jnp.int32, (R, R), 1)
    same = (row // C) == (col // C)
    lower = same & (col <= row)
    upper = same & (col >= row)
    X = BF16_SUBLANES
    jrow = lax.broadcasted_iota(jnp.int32, (X, R), 0)
    jcol = lax.broadcasted_iota(jnp.int32, (X, R), 1)
    in_chunk = (jcol // C) == jrow
    last_rows = in_chunk.astype(BF16)
    mid_f = (in_chunk & (jcol % C <= C // 2 - 1)).astype(BF16)
    mid_b = (in_chunk & (jcol % C >= C // 2)).astype(BF16)
    cum_f = jnp.concatenate([lower.astype(BF16), mid_f, last_rows], axis=0)
    cum_b = jnp.concatenate([upper.astype(BF16), mid_b, last_rows], axis=0)

    for ref in (k2f_ref, qbf_ref, k2b_ref, qbb_ref):
        ref[...] = jnp.zeros(ref.shape, BF16)

    def per_chunk(extra, j0):
        return jnp.concatenate(
            [jnp.broadcast_to(extra[j0 + c:j0 + c + 1], (C, hd)) for c in range(G)], axis=0)

    def prepare(start, d, slot):
        f_ref, cum = d["f"], d["cum"]
        rows = pl.ds(start, R)
        f = lb + (1.0 - lb) * jax.nn.sigmoid(f_ref[rows, :])
        k = 1.0 - f
        lf = jnp.log(f)
        hi = lf.astype(BF16)
        r1 = lf - hi.astype(F32)
        mid = r1.astype(BF16)
        lo = (r1 - mid.astype(F32)).astype(BF16)
        yield
        ball = jnp.dot(cum, jnp.concatenate([hi, mid, lo], axis=1), preferred_element_type=F32)
        ball = ball[:, 0:hd] + ball[:, hd:2 * hd] + ball[:, 2 * hd:3 * hd]
        yield
        b = ball[0:R]
        b_mid = per_chunk(ball, R)
        b_last = per_chunk(ball, R + X)
        q = q_ref[rows, :].astype(F32)
        d["qe"][slot] = (q * jnp.exp(b - b_mid)).astype(BF16)
        d["ke"][slot] = (k * jnp.exp(b_mid - b)).astype(BF16)
        yield
        qb = (q * jnp.exp(b)).astype(BF16)
        k2 = (k * jnp.exp(b_last - b)).astype(BF16)
        for c in range(G):
            d["k2"][slot, c * C:(c + 1) * C, c * hd:(c + 1) * hd] = k2[c * C:(c + 1) * C]
            d["qb"][slot, c * C:(c + 1) * C, c * hd:(c + 1) * hd] = qb[c * C:(c + 1) * C]
        d["dec"][slot] = jnp.exp(ball[R + X:R + X + 8])
        yield

    def apply(start, d, slot, st):
        rows = pl.ds(start, R)
        v = i_ref[rows, :]
        sc = lax.dot_general(d["qe"][slot], d["ke"][slot], NT_DIMS, preferred_element_type=F32)
        sc = jnp.where(d["mask"], sc, 0.0)
        yield
        o = jnp.dot(sc.astype(BF16), v, preferred_element_type=F32)
        kv_t = jnp.dot(v.astype(F32).T.astype(BF16), d["k2"][slot], preferred_element_type=F32)
        yield
        dec = d["dec"][slot]
        entering = [None] * G
        for c in (reversed(range(G)) if d["reverse"] else range(G)):
            entering[c] = st.astype(BF16)
            st = st * dec[c:c + 1] + kv_t[:, c * hd:(c + 1) * hd]
        yield
        o = o + lax.dot_general(d["qb"][slot], jnp.concatenate(entering, axis=1), NT_DIMS,
                                preferred_element_type=F32)
        d["out"][rows, :] = o
        return st

    def interleave(*gens):
        results = [None] * len(gens)
        active = list(enumerate(gens))
        while active:
            still = []
            for idx, g in active:
                try:
                    next(g)
                    still.append((idx, g))
                except StopIteration as done:
                    results[idx] = done.value
            active = still
        return results

    fw = dict(f=ff_ref, cum=cum_f, mask=lower, reverse=False, out=ofw_ref,
              qe=qef_ref, ke=kef_ref, k2=k2f_ref, qb=qbf_ref, dec=decf_ref)
    bw = dict(f=fb_ref, cum=cum_b, mask=upper, reverse=True, out=obw_ref,
              qe=qeb_ref, ke=keb_ref, k2=k2b_ref, qb=qbb_ref, dec=decb_ref)

    def starts(g):
        g = jnp.minimum(g, n_groups - 1)
        return pl.multiple_of(g * R, R), pl.multiple_of((n_groups - 1 - g) * R, R)

    s_f, s_b = starts(0)
    interleave(prepare(s_f, fw, 0), prepare(s_b, bw, 0))

    def body(n, carry):
        st_f, st_b = carry
        for slot in (0, 1):
            a_f, a_b = starts(2 * n + slot)
            p_f, p_b = starts(2 * n + slot + 1)
            _, _, st_f, st_b = interleave(prepare(p_f, fw, 1 - slot), prepare(p_b, bw, 1 - slot),
                                          apply(a_f, fw, slot, st_f), apply(a_b, bw, slot, st_b))
        return st_f, st_b

    zero = jnp.zeros((hd, hd), F32)
    lax.fori_loop(0, n_groups // 2, body, (zero, zero))

    ng = ng_ref[...]
    R = 256

    def fin(r, _):
        rows = pl.ds(pl.multiple_of(r * R, R), R)
        o = ofw_ref[rows, :] + obw_ref[rows, :]
        y = o * lax.rsqrt(jnp.mean(o * o, axis=-1, keepdims=True) + RMS_EPS) * ng
        g = g_ref[rows, :].astype(F32)
        o_ref[rows, :] = (y * (g * jax.nn.sigmoid(g))).astype(o_ref.dtype)
        return 0

    lax.fori_loop(0, seq // R, fin, 0)


def _hgrn(h_act, h_gate, lb_table, norm_g, layer, batch, seq):
    hd = A_HEAD_DIM

    def col(part):
        return pl.BlockSpec((None, seq, hd), lambda b, h: (b, 0, part * A_HEADS + h))

    n_tab = lb_table.shape[0]
    return pl.pallas_call(
        functools.partial(_hgrn_kernel, layer=layer, seq=seq),
        grid=(batch, A_HEADS),
        in_specs=[pl.BlockSpec((n_tab, hd), lambda b, h: (0, h)),
                  col(0), col(0), col(1), col(1), col(2),
                  pl.BlockSpec((1, hd), lambda b, h: (0, h))],
        out_specs=pl.BlockSpec((None, seq, hd), lambda b, h: (b, 0, h)),
        out_shape=jax.ShapeDtypeStruct((batch, seq, A_WIDTH), BF16),
        scratch_shapes=[pltpu.VMEM((seq, hd), F32), pltpu.VMEM((seq, hd), F32)]
        + 2 * [pltpu.VMEM((2, HGRN_ROWS, hd), BF16), pltpu.VMEM((2, HGRN_ROWS, hd), BF16),
               pltpu.VMEM((2, HGRN_ROWS, HGRN_GROUP * hd), BF16),
               pltpu.VMEM((2, HGRN_ROWS, HGRN_GROUP * hd), BF16),
               pltpu.VMEM((2, 8, hd), F32)],
        compiler_params=_cparams("parallel", "parallel"),
        name="hgrn2",
    )(lb_table, h_act, h_gate, h_gate, h_act, h_act, norm_g)


def _gmlp_kernel(u_ref, v_ref, lg_ref, lbias_ref, ws_ref, bias_ref, o_ref, *, rows):
    vln = _layer_norm(v_ref[...].astype(F32), lg_ref[...], lbias_ref[...]).astype(BF16)
    for c in range(rows // B_CHUNK):
        r = slice(c * B_CHUNK, (c + 1) * B_CHUNK)
        for g in range(B_GROUPS):
            cs = slice(g * B_GROUP_DIM, (g + 1) * B_GROUP_DIM)
            s = jnp.dot(ws_ref[g], vln[r, cs], preferred_element_type=F32) + bias_ref[g]
            o_ref[r, cs] = (u_ref[r, cs] * s).astype(o_ref.dtype)


def _gmlp(h_act, u_blk, ln_g, ln_b, ws, bias, rows):
    M = h_act.shape[0]
    bias_b = jnp.broadcast_to(bias[:, :, None], (B_GROUPS, B_CHUNK, B_GROUP_DIM))
    return pl.pallas_call(
        functools.partial(_gmlp_kernel, rows=rows),
        grid=(M // rows,),
        in_specs=[pl.BlockSpec((rows, B_WIDTH), lambda i: (i, u_blk)),
                  pl.BlockSpec((rows, B_WIDTH), lambda i: (i, u_blk + 1)),
                  pl.BlockSpec((1, B_WIDTH), lambda i: (0, 0)),
                  pl.BlockSpec((1, B_WIDTH), lambda i: (0, 0)),
                  pl.BlockSpec((B_GROUPS, B_CHUNK, B_CHUNK), lambda i: (0, 0, 0)),
                  pl.BlockSpec((B_GROUPS, B_CHUNK, B_GROUP_DIM), lambda i: (0, 0, 0))],
        out_specs=pl.BlockSpec((rows, B_WIDTH), lambda i: (i, 0)),
        out_shape=jax.ShapeDtypeStruct((M, B_WIDTH), BF16),
        compiler_params=_cparams("parallel"),
        name="gmlp",
    )(h_act, h_act, ln_g.reshape(1, -1), ln_b.reshape(1, -1), ws.astype(BF16), bias_b)


PROJ_STAGE_ROWS = 512


def _proj_ln_kernel(*refs, n_in, n_out, layer):
    ins = refs[:n_in]
    w_hbm, x_ref, g_ref, b_ref = refs[n_in:n_in + 4]
    out_refs = refs[n_in + 4:n_in + 4 + n_out]
    wb_ref, stage_ref, sem = refs[n_in + 4 + n_out:]
    K = wb_ref.shape[0]
    n_stage = K // PROJ_STAGE_ROWS

    @pl.when(pl.program_id(0) == 0)
    def _():
        def chunk_copy(c):
            rows = pl.ds(c * PROJ_STAGE_ROWS, PROJ_STAGE_ROWS)
            return pltpu.make_async_copy(w_hbm.at[layer, rows, :], stage_ref.at[c % 2], sem.at[c % 2])

        chunk_copy(0).start()
        for c in range(n_stage):
            if c + 1 < n_stage:
                chunk_copy(c + 1).start()
            chunk_copy(c).wait()
            wb_ref[c * PROJ_STAGE_ROWS:(c + 1) * PROJ_STAGE_ROWS, :] = stage_ref[c % 2].astype(BF16)

    acc = None
    off = 0
    for r in ins:
        kk = r.shape[1]
        d = jnp.dot(r[...], wb_ref[off:off + kk, :], preferred_element_type=F32)
        acc = d if acc is None else acc + d
        off += kk
    out = _layer_norm(ALPHA * x_ref[...] + acc, g_ref[...], b_ref[...])
    for o_ref in out_refs:
        o_ref[...] = out.astype(o_ref.dtype)


def _proj_ln(parts, w, layer, x, g, b, tm, out_dtypes, name):
    M, D = x.shape
    K = w.shape[1]
    assert K % PROJ_STAGE_ROWS == 0
    in_specs = [pl.BlockSpec((tm, p.shape[1]), lambda i: (i, 0)) for p in parts]
    in_specs += [pl.BlockSpec(memory_space=pl.ANY),
                 pl.BlockSpec((tm, D), lambda i: (i, 0)),
                 pl.BlockSpec((1, D), lambda i: (0, 0)),
                 pl.BlockSpec((1, D), lambda i: (0, 0))]
    return pl.pallas_call(
        functools.partial(_proj_ln_kernel, n_in=len(parts), n_out=len(out_dtypes), layer=layer),
        grid=(M // tm,),
        in_specs=in_specs,
        out_specs=[pl.BlockSpec((tm, D), lambda i: (i, 0)) for _ in out_dtypes],
        out_shape=[jax.ShapeDtypeStruct((M, D), dt) for dt in out_dtypes],
        scratch_shapes=[pltpu.VMEM((K, D), BF16),
                        pltpu.VMEM((2, PROJ_STAGE_ROWS, D), F32),
                        pltpu.SemaphoreType.DMA((2,))],
        compiler_params=_cparams("arbitrary"),
        name=name,
    )(*parts, w, x, g.reshape(1, -1), b.reshape(1, -1))


def _ffn_up_kernel(xm_ref, xp_ref, xn_ref, wg_ref, wv_ref, cwg_ref, cwv_ref, cbg_ref, cbv_ref,
                   o_ref, lhs_ref, wgb_ref, wvb_ref, *, tm, seq):
    H = BF16_SUBLANES
    i = pl.program_id(1)

    @pl.when(i == 0)
    def _():
        wgb_ref[...] = wg_ref[...].astype(BF16)
        wvb_ref[...] = wv_ref[...].astype(BF16)

    t0 = (i * tm) % seq
    zero = jnp.zeros((H, xm_ref.shape[1]), BF16)
    lhs_ref[0:H, :] = jnp.where(t0 != 0, xp_ref[...], zero)
    lhs_ref[H:H + tm, :] = xm_ref[...]
    lhs_ref[H + tm:2 * H + tm, :] = jnp.where(t0 + tm != seq, xn_ref[...], zero)
    lhs = lhs_ref[...]

    def conv(w_ref, cw_ref, cb_ref):
        h = jnp.dot(lhs, w_ref[...], preferred_element_type=F32)
        cw = cw_ref[...]
        return (h[H - 1:H - 1 + tm] * cw[0:1] + h[H:H + tm] * cw[1:2]
                + h[H + 1:H + 1 + tm] * cw[2:3] + cb_ref[...])

    gate = conv(wgb_ref, cwg_ref, cbg_ref)
    val = conv(wvb_ref, cwv_ref, cbv_ref)
    o_ref[...] = (gate * jax.nn.sigmoid(gate) * val).astype(o_ref.dtype)


def _ffn_up(xb, w_up, conv_w, conv_b, layer, seq, tm, tn):
    M, D = xb.shape
    H = BF16_SUBLANES
    nj = D_FF // tn
    hb = tm // H
    n_hblk = M // H
    conv_b = conv_b.reshape(conv_b.shape[0], 1, -1)
    return pl.pallas_call(
        functools.partial(_ffn_up_kernel, tm=tm, seq=seq),
        grid=(nj, M // tm),
        in_specs=[pl.BlockSpec((tm, D), lambda j, i: (i, 0)),
                  pl.BlockSpec((H, D), lambda j, i: (jnp.maximum(i * hb - 1, 0), 0)),
                  pl.BlockSpec((H, D), lambda j, i: (jnp.minimum((i + 1) * hb, n_hblk - 1), 0)),
                  pl.BlockSpec((None, D, tn), lambda j, i: (layer, 0, j)),
                  pl.BlockSpec((None, D, tn), lambda j, i: (layer, 0, j + nj)),
                  pl.BlockSpec((None, 3, tn), lambda j, i: (layer, 0, j)),
                  pl.BlockSpec((None, 3, tn), lambda j, i: (layer, 0, j + nj)),
                  pl.BlockSpec((None, 1, tn), lambda j, i: (layer, 0, j)),
                  pl.BlockSpec((None, 1, tn), lambda j, i: (layer, 0, j + nj))],
        out_specs=pl.BlockSpec((tm, tn), lambda j, i: (i, j)),
        out_shape=jax.ShapeDtypeStruct((M, D_FF), BF16),
        scratch_shapes=[pltpu.VMEM((tm + 2 * H, D), BF16),
                        pltpu.VMEM((D, tn), BF16), pltpu.VMEM((D, tn), BF16)],
        compiler_params=_cparams("parallel", "arbitrary"),
        name="ffn_up_conv",
    )(xb, xb, xb, w_up, w_up, conv_w, conv_w, conv_b, conv_b)


def _rope_tables(seq):
    t = jnp.arange(seq, dtype=jnp.int32)
    r = (t // GRID_W).astype(F32)
    c = (t % GRID_W).astype(F32)
    half = C_HEAD_DIM // 2
    inv_freq = jnp.exp(-math.log(ROPE_THETA) * jnp.arange(0, half, 2, dtype=F32) / half)
    ang_r = r[:, None] * inv_freq
    ang_c = c[:, None] * inv_freq
    ang = jnp.concatenate([ang_r, ang_r, ang_c, ang_c], axis=-1)
    sign = jnp.tile(jnp.concatenate([-jnp.ones(half // 2, F32), jnp.ones(half // 2, F32)]), 2)
    return jnp.cos(ang), jnp.sin(ang) * sign


def _qkv_kernel(x_ref, w_ref, gain_ref, cos_ref, sin_ref, o_ref, wb_ref, h_ref,
                *, rows_per_col, n_tiles, n_norm_heads):
    s = pl.program_id(0)
    hd = C_HEAD_DIM

    @pl.when(s == 0)
    def _():
        h_ref[1] = jnp.zeros(h_ref.shape[1:], F32)

    @pl.when((s % rows_per_col == 0) & (s < n_tiles))
    def _():
        wb_ref[...] = w_ref[...].astype(BF16)

    heads_per_tile = o_ref.shape[1] // hd
    first_head = (jnp.maximum(s - 1, 0) // rows_per_col) * heads_per_tile

    def step(fill, drain):
        cos = cos_ref[...]
        sin = sin_ref[...]
        lane = lax.broadcasted_iota(jnp.int32, (1, hd), 1)
        low = (lane % (hd // 2)) < (hd // 4)
        for h in range(heads_per_tile):
            cs = slice(h * hd, (h + 1) * hd)
            a = h_ref[drain, :, cs]
            y = a * lax.rsqrt(jnp.mean(a * a, axis=-1, keepdims=True) + RMS_EPS) * gain_ref[:, cs]
            partner = jnp.where(low, pltpu.roll(y, hd - hd // 4, 1), pltpu.roll(y, hd // 4, 1))
            plain = first_head + h >= n_norm_heads
            o_ref[:, cs] = jnp.where(plain, a, y * cos + partner * sin).astype(o_ref.dtype)
        h_ref[fill] = jnp.dot(x_ref[...], wb_ref[...], preferred_element_type=F32)

    @pl.when(s % 2 == 0)
    def _():
        step(0, 1)

    @pl.when(s % 2 == 1)
    def _():
        step(1, 0)


def _qkv_proj(xb, w, layer, gain, cos, sin, seq, tm, tn):
    M, D = xb.shape
    N = w.shape[2]
    n_pos_blk = seq // tm
    ni = M // tm
    n_tiles = (N // tn) * ni

    def fill(s):
        t = jnp.minimum(s, n_tiles - 1)
        return t // ni, t % ni

    def drain(s):
        t = jnp.maximum(s - 1, 0)
        return t // ni, t % ni

    return pl.pallas_call(
        functools.partial(_qkv_kernel, rows_per_col=ni, n_tiles=n_tiles,
                          n_norm_heads=C_Q_HEADS + C_KV_HEADS),
        grid=(n_tiles + 1,),
        in_specs=[pl.BlockSpec((tm, D), lambda s: (fill(s)[1], 0)),
                  pl.BlockSpec((None, D, tn), lambda s: (layer, 0, fill(s)[0])),
                  pl.BlockSpec((1, tn), lambda s: (0, drain(s)[0])),
                  pl.BlockSpec((tm, C_HEAD_DIM), lambda s: (drain(s)[1] % n_pos_blk, 0)),
                  pl.BlockSpec((tm, C_HEAD_DIM), lambda s: (drain(s)[1] % n_pos_blk, 0))],
        out_specs=pl.BlockSpec((tm, tn), lambda s: (drain(s)[1], drain(s)[0])),
        out_shape=jax.ShapeDtypeStruct((M, N), BF16),
        scratch_shapes=[pltpu.VMEM((D, tn), BF16), pltpu.VMEM((2, tm, tn), F32)],
        compiler_params=_cparams("arbitrary"),
        name="qkv_norm_rope",
    )(xb, w, gain, cos, sin)


def _attn_kernel(q_ref, k_ref, v_ref, o_ref, vt_ref, st_ref, m_ref, *, tq, blocks_per_head, n_chunks):
    hd = C_HEAD_DIM
    s = pl.program_id(0)
    seq = k_ref.shape[0]
    ck = seq // n_chunks

    @pl.when(s == 0)
    def _():
        st_ref[1] = jnp.zeros(st_ref.shape[1:], F32)
        m_ref[1] = jnp.zeros(m_ref.shape[1:], F32)

    @pl.when(jnp.maximum(s - 1, 0) % blocks_per_head == 0)
    def _():
        vt_ref[0:hd, :] = v_ref[...].astype(F32).T.astype(BF16)
        vt_ref[hd:, :] = jnp.ones((vt_ref.shape[0] - hd, seq), BF16)

    def step(fill, drain):
        q = jnp.concatenate([q_ref[:, g * hd:(g + 1) * hd] for g in range(C_GROUP)], axis=0)
        m_drain = m_ref[drain]
        m_fill = None
        acc = None
        for c in range(n_chunks):
            rows = slice(c * ck, (c + 1) * ck)
            sc = lax.dot_general(k_ref[rows, :], q, NT_DIMS, preferred_element_type=F32)
            st_ref[fill, rows, :] = sc
            mc = jnp.max(sc, axis=0, keepdims=True)
            m_fill = mc if c == 0 else jnp.maximum(m_fill, mc)
            p = jnp.exp2(st_ref[drain, rows, :] - m_drain).astype(BF16)
            pv = jnp.dot(vt_ref[:, rows], p, preferred_element_type=F32)
            acc = pv if c == 0 else acc + pv
        m_ref[fill] = m_fill
        o = acc[0:hd] * (1.0 / acc[hd:hd + 1])
        for g in range(C_GROUP):
            o_ref[:, g * hd:(g + 1) * hd] = o[:, g * tq:(g + 1) * tq].T.astype(o_ref.dtype)

    @pl.when(s % 2 == 0)
    def _():
        step(0, 1)

    @pl.when(s % 2 == 1)
    def _():
        step(1, 0)


def _attention(qkv, batch, seq, tq, n_chunks):
    hd = C_HEAD_DIM
    gw = C_GROUP * hd
    bph = seq // tq
    n_blocks = batch * C_KV_HEADS * bph

    def coords(blk):
        return blk // (C_KV_HEADS * bph), (blk // bph) % C_KV_HEADS, blk % bph

    def q_map(s):
        b, h, i = coords(jnp.minimum(s, n_blocks - 1))
        return b, i, h

    def k_map(s):
        b, h, _ = coords(jnp.minimum(s, n_blocks - 1))
        return b, 0, C_Q_HEADS + h

    def v_map(s):
        b, h, _ = coords(jnp.maximum(s - 1, 0))
        return b, 0, C_Q_HEADS + C_KV_HEADS + h

    def o_map(s):
        b, h, i = coords(jnp.maximum(s - 1, 0))
        return b, i, h

    return pl.pallas_call(
        functools.partial(_attn_kernel, tq=tq, blocks_per_head=bph, n_chunks=n_chunks),
        grid=(n_blocks + 1,),
        in_specs=[pl.BlockSpec((None, tq, gw), q_map),
                  pl.BlockSpec((None, seq, hd), k_map),
                  pl.BlockSpec((None, seq, hd), v_map)],
        out_specs=pl.BlockSpec((None, tq, gw), o_map),
        out_shape=jax.ShapeDtypeStruct((batch, seq, C_Q_HEADS * hd), BF16),
        scratch_shapes=[pltpu.VMEM((hd + BF16_SUBLANES, seq), BF16),
                        pltpu.VMEM((2, seq, C_GROUP * tq), F32),
                        pltpu.VMEM((2, 1, C_GROUP * tq), F32)],
        compiler_params=_cparams("arbitrary"),
        name="gqa_attention",
    )(qkv, qkv, qkv)


def _conv_ffn_ln(x, xb, w_up, conv_w, conv_b, w_down, g, b, layer, seq, out_dtypes):
    act = _ffn_up(xb, w_up, conv_w, conv_b, layer, seq, tm=1024, tn=512)
    return _proj_ln([act], w_down, layer, x, g[layer], b[layer], 256, out_dtypes, "ffn_down_ln")


def kernel(x, w_in_ab, hgrn_lb_table, hgrn_norm_g, gmlp_ln_g, gmlp_ln_b, gmlp_ws, gmlp_bias,
           w_out_ab, w_in_attn, q_norm_g, k_norm_g, w_out_attn, ffn_up, ffn_conv_w, ffn_conv_b,
           ffn_down, ln1_g, ln1_b, ln2_g, ln2_b):
    batch, seq, d = x.shape
    M = batch * seq
    x = x.reshape(M, d)
    xb = x.astype(BF16)

    h_gate = _matmul(xb, w_in_ab, 0, 2, lambda j: j + 1, 1024, 1024, F32, "inproj_gates")
    h_act = _matmul(xb, w_in_ab, 0, 5, lambda j: j + 2 * jnp.minimum(j, 1), 1024, 1024, BF16, "inproj_acts")
    o_a = _hgrn(h_act.reshape(batch, seq, -1), h_gate.reshape(batch, seq, -1), hgrn_lb_table,
                hgrn_norm_g[0].reshape(1, -1), 0, batch, seq)
    o_b = _gmlp(h_act, 3, gmlp_ln_g[0], gmlp_ln_b[0], gmlp_ws[0], gmlp_bias[0], rows=512)
    both = (F32, BF16)
    x, xb = _proj_ln([o_a.reshape(M, -1), o_b], w_out_ab, 0, x, ln1_g[0], ln1_b[0], 512, both, "outproj_ln")
    x, xb = _conv_ffn_ln(x, xb, ffn_up, ffn_conv_w, ffn_conv_b, ffn_down, ln2_g, ln2_b, 0, seq, both)

    scale = C_HEAD_DIM ** -0.5 * math.log2(math.e)
    gain = jnp.concatenate([jnp.tile(q_norm_g[0] * scale, C_Q_HEADS), jnp.tile(k_norm_g[0], C_KV_HEADS),
                            jnp.ones((C_KV_HEADS * C_HEAD_DIM,), F32)])
    cos, sin = _rope_tables(seq)
    qkv = _qkv_proj(xb, w_in_attn, 0, gain.reshape(1, -1), cos, sin, seq, tm=1024, tn=1024)
    att = _attention(qkv.reshape(batch, seq, -1), batch, seq, tq=128, n_chunks=8)
    x, xb = _proj_ln([att.reshape(M, -1)], w_out_attn, 0, x, ln1_g[1], ln1_b[1], 512, both, "outproj_ln")
    (x,) = _conv_ffn_ln(x, xb, ffn_up, ffn_conv_w, ffn_conv_b, ffn_down, ln2_g, ln2_b, 1, seq, (F32,))
    return x.reshape(batch, seq, d)
```

```python
import functools
import math

import jax
import jax.numpy as jnp
from jax import lax
from jax.experimental import pallas as pl
from jax.experimental.pallas import tpu as pltpu

F32 = jnp.float32
BF16 = jnp.bfloat16

D_MODEL = 2048
GRID_W = 64
A_HEAD_DIM = 128
A_WIDTH = D_MODEL // 2
A_HEADS = A_WIDTH // A_HEAD_DIM
HGRN_CHUNK = 64
B_WIDTH = D_MODEL // 2
B_GROUP_DIM = 128
B_GROUPS = B_WIDTH // B_GROUP_DIM
B_CHUNK = 128
C_HEAD_DIM = 128
C_Q_HEADS = D_MODEL // C_HEAD_DIM
C_KV_HEADS = C_Q_HEADS // 4
C_GROUP = C_Q_HEADS // C_KV_HEADS
ROPE_THETA = 10000.0
D_FF = 5632
DEPTH = 2
ALPHA = (2.0 * DEPTH) ** 0.25
LN_EPS = 1e-5
RMS_EPS = 1e-6

V7X_VMEM_LIMIT_BYTES = 56 * 1024 * 1024
BF16_SUBLANES = 16

NT_DIMS = (((1,), (1,)), ((), ()))


def _cparams(*sem):
    return pltpu.CompilerParams(dimension_semantics=sem, vmem_limit_bytes=V7X_VMEM_LIMIT_BYTES)


def _layer_norm(y, g, b):
    mu = jnp.mean(y, axis=-1, keepdims=True)
    yc = y - mu
    var = jnp.mean(yc * yc, axis=-1, keepdims=True)
    return yc * lax.rsqrt(var + LN_EPS) * g + b


def _mm_kernel(x_ref, w_ref, o_ref, wb_ref):
    @pl.when(pl.program_id(1) == 0)
    def _():
        wb_ref[...] = w_ref[...].astype(BF16)

    o_ref[...] = jnp.dot(x_ref[...], wb_ref[...], preferred_element_type=F32).astype(o_ref.dtype)


def _matmul(x, w, layer, n_col_tiles, col_tile_of, tm, tn, out_dtype, name):
    M, K = x.shape
    return pl.pallas_call(
        _mm_kernel,
        grid=(n_col_tiles, M // tm),
        in_specs=[pl.BlockSpec((tm, K), lambda j, i: (i, 0)),
                  pl.BlockSpec((None, K, tn), lambda j, i: (layer, 0, col_tile_of(j)))],
        out_specs=pl.BlockSpec((tm, tn), lambda j, i: (i, j)),
        out_shape=jax.ShapeDtypeStruct((M, n_col_tiles * tn), out_dtype),
        scratch_shapes=[pltpu.VMEM((K, tn), BF16)],
        compiler_params=_cparams("parallel", "arbitrary"),
        name=name,
    )(x, w)


HGRN_GROUP = 4
HGRN_ROWS = HGRN_GROUP * HGRN_CHUNK


def _hgrn_kernel(tab_ref, q_ref, ff_ref, fb_ref, i_ref, g_ref, ng_ref, o_ref,
                 ofw_ref, obw_ref, qef_ref, kef_ref, k2f_ref, qbf_ref, decf_ref,
                 qeb_ref, keb_ref, k2b_ref, qbb_ref, decb_ref, *, layer, seq):
    C, G, R, hd = HGRN_CHUNK, HGRN_GROUP, HGRN_ROWS, A_HEAD_DIM
    n_groups = seq // R
    tab = tab_ref[...]
    e = jnp.exp(tab - jnp.max(tab, axis=0, keepdims=True))
    sm = e / jnp.sum(e, axis=0, keepdims=True)
    lb = jnp.sum(sm[:layer + 1], axis=0, keepdims=True)

    row = lax.broadcasted_iota(jnp.int32, (R, R), 0)
    col = lax.broadcasted_iota(jnp.int32, (R, R), 1)
    same = (row // C) == (col // C)
    lower = same & (col <= row)
    upper = same & (col >= row)
    X = BF16_SUBLANES
    jrow = lax.broadcasted_iota(jnp.int32, (X, R), 0)
    jcol = lax.broadcasted_iota(jnp.int32, (X, R), 1)
    in_chunk = (jcol // C) == jrow
    last_rows = in_chunk.astype(BF16)
    mid_f = (in_chunk & (jcol % C <= C // 2 - 1)).astype(BF16)
    mid_b = (in_chunk & (jcol % C >= C // 2)).astype(BF16)
    cum_f = jnp.concatenate([lower.astype(BF16), mid_f, last_rows], axis=0)
    cum_b = jnp.concatenate([upper.astype(BF16), mid_b, last_rows], axis=0)

    for ref in (k2f_ref, qbf_ref, k2b_ref, qbb_ref):
        ref[...] = jnp.zeros(ref.shape, BF16)

    def per_chunk(extra, j0):
        return jnp.concatenate(
            [jnp.broadcast_to(extra[j0 + c:j0 + c + 1], (C, hd)) for c in range(G)], axis=0)

    def prepare(start, d, slot):
        f_ref, cum = d["f"], d["cum"]
        rows = pl.ds(start, R)
        f = lb + (1.0 - lb) * jax.nn.sigmoid(f_ref[rows, :])
        k = 1.0 - f
        lf = jnp.log(f)
        hi = lf.astype(BF16)
        r1 = lf - hi.astype(F32)
        mid = r1.astype(BF16)
        lo = (r1 - mid.astype(F32)).astype(BF16)
        yield
        ball = jnp.dot(cum, jnp.concatenate([hi, mid, lo], axis=1), preferred_element_type=F32)
        ball = ball[:, 0:hd] + ball[:, hd:2 * hd] + ball[:, 2 * hd:3 * hd]
        yield
        b = ball[0:R]
        b_mid = per_chunk(ball, R)
        b_last = per_chunk(ball, R + X)
        q = q_ref[rows, :].astype(F32)
        d["qe"][slot] = (q * jnp.exp(b - b_mid)).astype(BF16)
        d["ke"][slot] = (k * jnp.exp(b_mid - b)).astype(BF16)
        yield
        qb = (q * jnp.exp(b)).astype(BF16)
        k2 = (k * jnp.exp(b_last - b)).astype(BF16)
        for c in range(G):
            d["k2"][slot, c * C:(c + 1) * C, c * hd:(c + 1) * hd] = k2[c * C:(c + 1) * C]
            d["qb"][slot, c * C:(c + 1) * C, c * hd:(c + 1) * hd] = qb[c * C:(c + 1) * C]
        d["dec"][slot] = jnp.exp(ball[R + X:R + X + 8])
        yield

    def apply(start, d, slot, st):
        rows = pl.ds(start, R)
        v = i_ref[rows, :]
        sc = lax.dot_general(d["qe"][slot], d["ke"][slot], NT_DIMS, preferred_element_type=F32)
        sc = jnp.where(d["mask"], sc, 0.0)
        yield
        o = jnp.dot(sc.astype(BF16), v, preferred_element_type=F32)
        kv_t = jnp.dot(v.astype(F32).T.astype(BF16), d["k2"][slot], preferred_element_type=F32)
        yield
        dec = d["dec"][slot]
        entering = [None] * G
        for c in (reversed(range(G)) if d["reverse"] else range(G)):
            entering[c] = st.astype(BF16)
            st = st * dec[c:c + 1] + kv_t[:, c * hd:(c + 1) * hd]
        yield
        o = o + lax.dot_general(d["qb"][slot], jnp.concatenate(entering, axis=1), NT_DIMS,
                                preferred_element_type=F32)
        d["out"][rows, :] = o
        return st

    def interleave(*gens):
        results = [None] * len(gens)
        active = list(enumerate(gens))
        while active:
            still = []
            for idx, g in active:
                try:
                    next(g)
                    still.append((idx, g))
                except StopIteration as done:
                    results[idx] = done.value
            active = still
        return results

    fw = dict(f=ff_ref, cum=cum_f, mask=lower, reverse=False, out=ofw_ref,
              qe=qef_ref, ke=kef_ref, k2=k2f_ref, qb=qbf_ref, dec=decf_ref)
    bw = dict(f=fb_ref, cum=cum_b, mask=upper, reverse=True, out=obw_ref,
              qe=qeb_ref, ke=keb_ref, k2=k2b_ref, qb=qbb_ref, dec=decb_ref)

    def starts(g):
        g = jnp.minimum(g, n_groups - 1)
        return pl.multiple_of(g * R, R), pl.multiple_of((n_groups - 1 - g) * R, R)

    s_f, s_b = starts(0)
    interleave(prepare(s_f, fw, 0), prepare(s_b, bw, 0))

    def body(n, carry):
        st_f, st_b = carry
        for slot in (0, 1):
            a_f, a_b = starts(2 * n + slot)
            p_f, p_b = starts(2 * n + slot + 1)
            _, _, st_f, st_b = interleave(prepare(p_f, fw, 1 - slot), prepare(p_b, bw, 1 - slot),
                                          apply(a_f, fw, slot, st_f), apply(a_b, bw, slot, st_b))
        return st_f, st_b

    zero = jnp.zeros((hd, hd), F32)
    lax.fori_loop(0, n_groups // 2, body, (zero, zero))

    ng = ng_ref[...]
    R = 256

    def fin(r, _):
        rows = pl.ds(pl.multiple_of(r * R, R), R)
        o = ofw_ref[rows, :] + obw_ref[rows, :]
        y = o * lax.rsqrt(jnp.mean(o * o, axis=-1, keepdims=True) + RMS_EPS) * ng
        g = g_ref[rows, :].astype(F32)
        o_ref[rows, :] = (y * (g * jax.nn.sigmoid(g))).astype(o_ref.dtype)
        return 0

    lax.fori_loop(0, seq // R, fin, 0)


def _hgrn(h_act, h_gate, lb_table, norm_g, layer, batch, seq):
    hd = A_HEAD_DIM

    def col(part):
        return pl.BlockSpec((None, seq, hd), lambda b, h: (b, 0, part * A_HEADS + h))

    n_tab = lb_table.shape[0]
    return pl.pallas_call(
        functools.partial(_hgrn_kernel, layer=layer, seq=seq),
        grid=(batch, A_HEADS),
        in_specs=[pl.BlockSpec((n_tab, hd), lambda b, h: (0, h)),
                  col(0), col(0), col(1), col(1), col(2),
                  pl.BlockSpec((1, hd), lambda b, h: (0, h))],
        out_specs=pl.BlockSpec((None, seq, hd), lambda b, h: (b, 0, h)),
        out_shape=jax.ShapeDtypeStruct((batch, seq, A_WIDTH), BF16),
        scratch_shapes=[pltpu.VMEM((seq, hd), F32), pltpu.VMEM((seq, hd), F32)]
        + 2 * [pltpu.VMEM((2, HGRN_ROWS, hd), BF16), pltpu.VMEM((2, HGRN_ROWS, hd), BF16),
               pltpu.VMEM((2, HGRN_ROWS, HGRN_GROUP * hd), BF16),
               pltpu.VMEM((2, HGRN_ROWS, HGRN_GROUP * hd), BF16),
               pltpu.VMEM((2, 8, hd), F32)],
        compiler_params=_cparams("parallel", "parallel"),
        name="hgrn2",
    )(lb_table, h_act, h_gate, h_gate, h_act, h_act, norm_g)


def _gmlp_kernel(u_ref, v_ref, lg_ref, lbias_ref, ws_ref, bias_ref, o_ref, *, rows):
    vln = _layer_norm(v_ref[...].astype(F32), lg_ref[...], lbias_ref[...]).astype(BF16)
    for c in range(rows // B_CHUNK):
        r = slice(c * B_CHUNK, (c + 1) * B_CHUNK)
        for g in range(B_GROUPS):
            cs = slice(g * B_GROUP_DIM, (g + 1) * B_GROUP_DIM)
            s = jnp.dot(ws_ref[g], vln[r, cs], preferred_element_type=F32) + bias_ref[g]
            o_ref[r, cs] = (u_ref[r, cs] * s).astype(o_ref.dtype)


def _gmlp(h_act, u_blk, ln_g, ln_b, ws, bias, rows):
    M = h_act.shape[0]
    bias_b = jnp.broadcast_to(bias[:, :, None], (B_GROUPS, B_CHUNK, B_GROUP_DIM))
    return pl.pallas_call(
        functools.partial(_gmlp_kernel, rows=rows),
        grid=(M // rows,),
        in_specs=[pl.BlockSpec((rows, B_WIDTH), lambda i: (i, u_blk)),
                  pl.BlockSpec((rows, B_WIDTH), lambda i: (i, u_blk + 1)),
                  pl.BlockSpec((1, B_WIDTH), lambda i: (0, 0)),
                  pl.BlockSpec((1, B_WIDTH), lambda i: (0, 0)),
                  pl.BlockSpec((B_GROUPS, B_CHUNK, B_CHUNK), lambda i: (0, 0, 0)),
                  pl.BlockSpec((B_GROUPS, B_CHUNK, B_GROUP_DIM), lambda i: (0, 0, 0))],
        out_specs=pl.BlockSpec((rows, B_WIDTH), lambda i: (i, 0)),
        out_shape=jax.ShapeDtypeStruct((M, B_WIDTH), BF16),
        compiler_params=_cparams("parallel"),
        name="gmlp",
    )(h_act, h_act, ln_g.reshape(1, -1), ln_b.reshape(1, -1), ws.astype(BF16), bias_b)


PROJ_STAGE_ROWS = 512


def _proj_ln_kernel(*refs, n_in, n_out, layer):
    ins = refs[:n_in]
    w_hbm, x_ref, g_ref, b_ref = refs[n_in:n_in + 4]
    out_refs = refs[n_in + 4:n_in + 4 + n_out]
    wb_ref, stage_ref, sem = refs[n_in + 4 + n_out:]
    K = wb_ref.shape[0]
    n_stage = K // PROJ_STAGE_ROWS

    @pl.when(pl.program_id(0) == 0)
    def _():
        def chunk_copy(c):
            rows = pl.ds(c * PROJ_STAGE_ROWS, PROJ_STAGE_ROWS)
            return pltpu.make_async_copy(w_hbm.at[layer, rows, :], stage_ref.at[c % 2], sem.at[c % 2])

        chunk_copy(0).start()
        for c in range(n_stage):
            if c + 1 < n_stage:
                chunk_copy(c + 1).start()
            chunk_copy(c).wait()
            wb_ref[c * PROJ_STAGE_ROWS:(c + 1) * PROJ_STAGE_ROWS, :] = stage_ref[c % 2].astype(BF16)

    acc = None
    off = 0
    for r in ins:
        kk = r.shape[1]
        d = jnp.dot(r[...], wb_ref[off:off + kk, :], preferred_element_type=F32)
        acc = d if acc is None else acc + d
        off += kk
    out = _layer_norm(ALPHA * x_ref[...] + acc, g_ref[...], b_ref[...])
    for o_ref in out_refs:
        o_ref[...] = out.astype(o_ref.dtype)


def _proj_ln(parts, w, layer, x, g, b, tm, out_dtypes, name):
    M, D = x.shape
    K = w.shape[1]
    assert K % PROJ_STAGE_ROWS == 0
    in_specs = [pl.BlockSpec((tm, p.shape[1]), lambda i: (i, 0)) for p in parts]
    in_specs += [pl.BlockSpec(memory_space=pl.ANY),
                 pl.BlockSpec((tm, D), lambda i: (i, 0)),
                 pl.BlockSpec((1, D), lambda i: (0, 0)),
                 pl.BlockSpec((1, D), lambda i: (0, 0))]
    return pl.pallas_call(
        functools.partial(_proj_ln_kernel, n_in=len(parts), n_out=len(out_dtypes), layer=layer),
        grid=(M // tm,),
        in_specs=in_specs,
        out_specs=[pl.BlockSpec((tm, D), lambda i: (i, 0)) for _ in out_dtypes],
        out_shape=[jax.ShapeDtypeStruct((M, D), dt) for dt in out_dtypes],
        scratch_shapes=[pltpu.VMEM((K, D), BF16),
                        pltpu.VMEM((2, PROJ_STAGE_ROWS, D), F32),
                        pltpu.SemaphoreType.DMA((2,))],
        compiler_params=_cparams("arbitrary"),
        name=name,
    )(*parts, w, x, g.reshape(1, -1), b.reshape(1, -1))


def _ffn_up_kernel(xm_ref, xp_ref, xn_ref, wg_ref, wv_ref, cwg_ref, cwv_ref, cbg_ref, cbv_ref,
                   o_ref, lhs_ref, wgb_ref, wvb_ref, *, tm, seq):
    H = BF16_SUBLANES
    i = pl.program_id(1)

    @pl.when(i == 0)
    def _():
        wgb_ref[...] = wg_ref[...].astype(BF16)
        wvb_ref[...] = wv_ref[...].astype(BF16)

    t0 = (i * tm) % seq
    zero = jnp.zeros((H, xm_ref.shape[1]), BF16)
    lhs_ref[0:H, :] = jnp.where(t0 != 0, xp_ref[...], zero)
    lhs_ref[H:H + tm, :] = xm_ref[...]
    lhs_ref[H + tm:2 * H + tm, :] = jnp.where(t0 + tm != seq, xn_ref[...], zero)
    lhs = lhs_ref[...]

    def conv(w_ref, cw_ref, cb_ref):
        h = jnp.dot(lhs, w_ref[...], preferred_element_type=F32)
        cw = cw_ref[...]
        n = h.shape[0]
        prev = pltpu.roll(h, 1, 0)[H:H + tm]
        nxt = pltpu.roll(h, n - 1, 0)[H:H + tm]
        return prev * cw[0:1] + h[H:H + tm] * cw[1:2] + nxt * cw[2:3] + cb_ref[...]

    gate = conv(wgb_ref, cwg_ref, cbg_ref)
    val = conv(wvb_ref, cwv_ref, cbv_ref)
    o_ref[...] = (gate * jax.nn.sigmoid(gate) * val).astype(o_ref.dtype)


def _ffn_up(xb, w_up, conv_w, conv_b, layer, seq, tm, tn):
    M, D = xb.shape
    H = BF16_SUBLANES
    nj = D_FF // tn
    hb = tm // H
    n_hblk = M // H
    conv_b = conv_b.reshape(conv_b.shape[0], 1, -1)
    return pl.pallas_call(
        functools.partial(_ffn_up_kernel, tm=tm, seq=seq),
        grid=(nj, M // tm),
        in_specs=[pl.BlockSpec((tm, D), lambda j, i: (i, 0)),
                  pl.BlockSpec((H, D), lambda j, i: (jnp.maximum(i * hb - 1, 0), 0)),
                  pl.BlockSpec((H, D), lambda j, i: (jnp.minimum((i + 1) * hb, n_hblk - 1), 0)),
                  pl.BlockSpec((None, D, tn), lambda j, i: (layer, 0, j)),
                  pl.BlockSpec((None, D, tn), lambda j, i: (layer, 0, j + nj)),
                  pl.BlockSpec((None, 3, tn), lambda j, i: (layer, 0, j)),
                  pl.BlockSpec((None, 3, tn), lambda j, i: (layer, 0, j + nj)),
                  pl.BlockSpec((None, 1, tn), lambda j, i: (layer, 0, j)),
                  pl.BlockSpec((None, 1, tn), lambda j, i: (layer, 0, j + nj))],
        out_specs=pl.BlockSpec((tm, tn), lambda j, i: (i, j)),
        out_shape=jax.ShapeDtypeStruct((M, D_FF), BF16),
        scratch_shapes=[pltpu.VMEM((tm + 2 * H, D), BF16),
                        pltpu.VMEM((D, tn), BF16), pltpu.VMEM((D, tn), BF16)],
        compiler_params=_cparams("parallel", "arbitrary"),
        name="ffn_up_conv",
    )(xb, xb, xb, w_up, w_up, conv_w, conv_w, conv_b, conv_b)


def _rope_tables(seq):
    t = jnp.arange(seq, dtype=jnp.int32)
    r = (t // GRID_W).astype(F32)
    c = (t % GRID_W).astype(F32)
    half = C_HEAD_DIM // 2
    inv_freq = jnp.exp(-math.log(ROPE_THETA) * jnp.arange(0, half, 2, dtype=F32) / half)
    ang_r = r[:, None] * inv_freq
    ang_c = c[:, None] * inv_freq
    ang = jnp.concatenate([ang_r, ang_r, ang_c, ang_c], axis=-1)
    sign = jnp.tile(jnp.concatenate([-jnp.ones(half // 2, F32), jnp.ones(half // 2, F32)]), 2)
    return jnp.cos(ang), jnp.sin(ang) * sign


def _qkv_kernel(x_ref, w_ref, gain_ref, cos_ref, sin_ref, o_ref, wb_ref, h_ref,
                *, rows_per_col, n_tiles, n_norm_heads):
    s = pl.program_id(0)
    hd = C_HEAD_DIM

    @pl.when(s == 0)
    def _():
        h_ref[1] = jnp.zeros(h_ref.shape[1:], F32)

    @pl.when((s % rows_per_col == 0) & (s < n_tiles))
    def _():
        wb_ref[...] = w_ref[...].astype(BF16)

    heads_per_tile = o_ref.shape[1] // hd
    first_head = (jnp.maximum(s - 1, 0) // rows_per_col) * heads_per_tile

    def step(fill, drain):
        cos = cos_ref[...]
        sin = sin_ref[...]
        lane = lax.broadcasted_iota(jnp.int32, (1, hd), 1)
        low = (lane % (hd // 2)) < (hd // 4)
        for h in range(heads_per_tile):
            cs = slice(h * hd, (h + 1) * hd)
            a = h_ref[drain, :, cs]
            y = a * lax.rsqrt(jnp.mean(a * a, axis=-1, keepdims=True) + RMS_EPS) * gain_ref[:, cs]
            partner = jnp.where(low, pltpu.roll(y, hd - hd // 4, 1), pltpu.roll(y, hd // 4, 1))
            plain = first_head + h >= n_norm_heads
            o_ref[:, cs] = jnp.where(plain, a, y * cos + partner * sin).astype(o_ref.dtype)
        h_ref[fill] = jnp.dot(x_ref[...], wb_ref[...], preferred_element_type=F32)

    @pl.when(s % 2 == 0)
    def _():
        step(0, 1)

    @pl.when(s % 2 == 1)
    def _():
        step(1, 0)


def _qkv_proj(xb, w, layer, gain, cos, sin, seq, tm, tn):
    M, D = xb.shape
    N = w.shape[2]
    n_pos_blk = seq // tm
    ni = M // tm
    n_tiles = (N // tn) * ni

    def fill(s):
        t = jnp.minimum(s, n_tiles - 1)
        return t // ni, t % ni

    def drain(s):
        t = jnp.maximum(s - 1, 0)
        return t // ni, t % ni

    return pl.pallas_call(
        functools.partial(_qkv_kernel, rows_per_col=ni, n_tiles=n_tiles,
                          n_norm_heads=C_Q_HEADS + C_KV_HEADS),
        grid=(n_tiles + 1,),
        in_specs=[pl.BlockSpec((tm, D), lambda s: (fill(s)[1], 0)),
                  pl.BlockSpec((None, D, tn), lambda s: (layer, 0, fill(s)[0])),
                  pl.BlockSpec((1, tn), lambda s: (0, drain(s)[0])),
                  pl.BlockSpec((tm, C_HEAD_DIM), lambda s: (drain(s)[1] % n_pos_blk, 0)),
                  pl.BlockSpec((tm, C_HEAD_DIM), lambda s: (drain(s)[1] % n_pos_blk, 0))],
        out_specs=pl.BlockSpec((tm, tn), lambda s: (drain(s)[1], drain(s)[0])),
        out_shape=jax.ShapeDtypeStruct((M, N), BF16),
        scratch_shapes=[pltpu.VMEM((D, tn), BF16), pltpu.VMEM((2, tm, tn), F32)],
        compiler_params=_cparams("arbitrary"),
        name="qkv_norm_rope",
    )(xb, w, gain, cos, sin)


def _attn_kernel(q_ref, k_ref, v_ref, o_ref, vt_ref, st_ref, m_ref, *, tq, n_sub, blocks_per_head, n_chunks):
    hd = C_HEAD_DIM
    s = pl.program_id(0)
    seq = k_ref.shape[0]
    ck = seq // n_chunks

    @pl.when(s == 0)
    def _():
        st_ref[1] = jnp.zeros(st_ref.shape[1:], F32)
        m_ref[1] = jnp.zeros(m_ref.shape[1:], F32)

    @pl.when(jnp.maximum(s - 1, 0) % blocks_per_head == 0)
    def _():
        vt_ref[0:hd, :] = v_ref[...].astype(F32).T.astype(BF16)
        vt_ref[hd:, :] = jnp.ones((vt_ref.shape[0] - hd, seq), BF16)

    sub = tq // n_sub

    def write_out(a, acc):
        o = acc[0:hd] * (1.0 / acc[hd:hd + 1])
        for g in range(C_GROUP):
            o_ref[a * sub:(a + 1) * sub, g * hd:(g + 1) * hd] = (
                o[:, g * sub:(g + 1) * sub].T.astype(o_ref.dtype))

    def step(fill, drain):
        qs = [jnp.concatenate([q_ref[a * sub:(a + 1) * sub, g * hd:(g + 1) * hd] for g in range(C_GROUP)],
                              axis=0) for a in range(n_sub)]
        pending = None
        for a in range(n_sub):
            m_drain = m_ref[drain, a]
            m_fill = None
            acc = None
            for c in range(n_chunks):
                rows = slice(c * ck, (c + 1) * ck)
                sc = lax.dot_general(k_ref[rows, :], qs[a], NT_DIMS, preferred_element_type=F32)
                st_ref[fill, a, rows, :] = sc
                mc = jnp.max(sc, axis=0, keepdims=True)
                m_fill = mc if c == 0 else jnp.maximum(m_fill, mc)
                if c == 1 and pending is not None:
                    write_out(*pending)
                    pending = None
                p = jnp.exp2(st_ref[drain, a, rows, :] - m_drain).astype(BF16)
                pv = jnp.dot(vt_ref[:, rows], p, preferred_element_type=F32)
                acc = pv if c == 0 else acc + pv
            m_ref[fill, a] = m_fill
            pending = (a, acc)
        write_out(*pending)

    @pl.when(s % 2 == 0)
    def _():
        step(0, 1)

    @pl.when(s % 2 == 1)
    def _():
        step(1, 0)


def _attention(qkv, batch, seq, tq, n_sub, n_chunks):
    hd = C_HEAD_DIM
    gw = C_GROUP * hd
    bph = seq // tq
    n_blocks = batch * C_KV_HEADS * bph

    def coords(blk):
        return blk // (C_KV_HEADS * bph), (blk // bph) % C_KV_HEADS, blk % bph

    def q_map(s):
        b, h, i = coords(jnp.minimum(s, n_blocks - 1))
        return b, i, h

    def k_map(s):
        b, h, _ = coords(jnp.minimum(s, n_blocks - 1))
        return b, 0, C_Q_HEADS + h

    def v_map(s):
        b, h, _ = coords(jnp.maximum(s - 1, 0))
        return b, 0, C_Q_HEADS + C_KV_HEADS + h

    def o_map(s):
        b, h, i = coords(jnp.maximum(s - 1, 0))
        return b, i, h

    return pl.pallas_call(
        functools.partial(_attn_kernel, tq=tq, n_sub=n_sub, blocks_per_head=bph, n_chunks=n_chunks),
        grid=(n_blocks + 1,),
        in_specs=[pl.BlockSpec((None, tq, gw), q_map),
                  pl.BlockSpec((None, seq, hd), k_map),
                  pl.BlockSpec((None, seq, hd), v_map)],
        out_specs=pl.BlockSpec((None, tq, gw), o_map),
        out_shape=jax.ShapeDtypeStruct((batch, seq, C_Q_HEADS * hd), BF16),
        scratch_shapes=[pltpu.VMEM((hd + BF16_SUBLANES, seq), BF16),
                        pltpu.VMEM((2, n_sub, seq, C_GROUP * tq // n_sub), F32),
                        pltpu.VMEM((2, n_sub, 1, C_GROUP * tq // n_sub), F32)],
        compiler_params=_cparams("arbitrary"),
        name="gqa_attention",
    )(qkv, qkv, qkv)


def _conv_ffn_ln(x, xb, w_up, conv_w, conv_b, w_down, g, b, layer, seq, out_dtypes):
    act = _ffn_up(xb, w_up, conv_w, conv_b, layer, seq, tm=1024, tn=512)
    return _proj_ln([act], w_down, layer, x, g[layer], b[layer], 256, out_dtypes, "ffn_down_ln")


def kernel(x, w_in_ab, hgrn_lb_table, hgrn_norm_g, gmlp_ln_g, gmlp_ln_b, gmlp_ws, gmlp_bias,
           w_out_ab, w_in_attn, q_norm_g, k_norm_g, w_out_attn, ffn_up, ffn_conv_w, ffn_conv_b,
           ffn_down, ln1_g, ln1_b, ln2_g, ln2_b):
    batch, seq, d = x.shape
    M = batch * seq
    x = x.reshape(M, d)
    xb = x.astype(BF16)

    h_gate = _matmul(xb, w_in_ab, 0, 2, lambda j: j + 1, 1024, 1024, F32, "inproj_gates")
    h_act = _matmul(xb, w_in_ab, 0, 5, lambda j: j + 2 * jnp.minimum(j, 1), 1024, 1024, BF16, "inproj_acts")
    o_a = _hgrn(h_act.reshape(batch, seq, -1), h_gate.reshape(batch, seq, -1), hgrn_lb_table,
                hgrn_norm_g[0].reshape(1, -1), 0, batch, seq)
    o_b = _gmlp(h_act, 3, gmlp_ln_g[0], gmlp_ln_b[0], gmlp_ws[0], gmlp_bias[0], rows=512)
    both = (F32, BF16)
    x, xb = _proj_ln([o_a.reshape(M, -1), o_b], w_out_ab, 0, x, ln1_g[0], ln1_b[0], 512, both, "outproj_ln")
    x, xb = _conv_ffn_ln(x, xb, ffn_up, ffn_conv_w, ffn_conv_b, ffn_down, ln2_g, ln2_b, 0, seq, both)

    scale = C_HEAD_DIM ** -0.5 * math.log2(math.e)
    gain = jnp.concatenate([jnp.tile(q_norm_g[0] * scale, C_Q_HEADS), jnp.tile(k_norm_g[0], C_KV_HEADS),
                            jnp.ones((C_KV_HEADS * C_HEAD_DIM,), F32)])
    cos, sin = _rope_tables(seq)
    qkv = _qkv_proj(xb, w_in_attn, 0, gain.reshape(1, -1), cos, sin, seq, tm=1024, tn=1024)
    att = _attention(qkv.reshape(batch, seq, -1), batch, seq, tq=256, n_sub=2, n_chunks=8)
    x, xb = _proj_ln([att.reshape(M, -1)], w_out_attn, 0, x, ln1_g[1], ln1_b[1], 512, both, "outproj_ln")
    (x,) = _conv_ffn_ln(x, xb, ffn_up, ffn_conv_w, ffn_conv_b, ffn_down, ln2_g, ln2_b, 1, seq, (F32,))
    return x.reshape(batch, seq, d)
```

```python
import functools
import math

import jax
import jax.numpy as jnp
import numpy as np
from jax import lax
from jax.experimental import pallas as pl
from jax.experimental.pallas import tpu as pltpu

F32 = jnp.float32
BF16 = jnp.bfloat16

D_MODEL = 2048
GRID_W = 64
A_HEAD_DIM = 128
A_WIDTH = D_MODEL // 2
A_HEADS = A_WIDTH // A_HEAD_DIM
HGRN_CHUNK = 64
B_WIDTH = D_MODEL // 2
B_GROUP_DIM = 128
B_GROUPS = B_WIDTH // B_GROUP_DIM
B_CHUNK = 128
C_HEAD_DIM = 128
C_Q_HEADS = D_MODEL // C_HEAD_DIM
C_KV_HEADS = C_Q_HEADS // 4
C_GROUP = C_Q_HEADS // C_KV_HEADS
ROPE_THETA = 10000.0
D_FF = 5632
DEPTH = 2
ALPHA = (2.0 * DEPTH) ** 0.25
LN_EPS = 1e-5
RMS_EPS = 1e-6

V7X_VMEM_LIMIT_BYTES = 56 * 1024 * 1024
BF16_SUBLANES = 16

NT_DIMS = (((1,), (1,)), ((), ()))


def _cparams(*sem):
    return pltpu.CompilerParams(dimension_semantics=sem, vmem_limit_bytes=V7X_VMEM_LIMIT_BYTES)


def _layer_norm(y, g, b):
    mu = jnp.mean(y, axis=-1, keepdims=True)
    yc = y - mu
    var = jnp.mean(yc * yc, axis=-1, keepdims=True)
    return yc * lax.rsqrt(var + LN_EPS) * g + b


def _mm_kernel(x_ref, w_ref, o_ref, wb_ref):
    @pl.when(pl.program_id(1) == 0)
    def _():
        wb_ref[...] = w_ref[...].astype(BF16)

    o_ref[...] = jnp.dot(x_ref[...], wb_ref[...], preferred_element_type=F32).astype(o_ref.dtype)


def _matmul(x, w, layer, n_col_tiles, col_tile_of, tm, tn, out_dtype, name):
    M, K = x.shape
    return pl.pallas_call(
        _mm_kernel,
        grid=(n_col_tiles, M // tm),
        in_specs=[pl.BlockSpec((tm, K), lambda j, i: (i, 0)),
                  pl.BlockSpec((None, K, tn), lambda j, i: (layer, 0, col_tile_of(j)))],
        out_specs=pl.BlockSpec((tm, tn), lambda j, i: (i, j)),
        out_shape=jax.ShapeDtypeStruct((M, n_col_tiles * tn), out_dtype),
        scratch_shapes=[pltpu.VMEM((K, tn), BF16)],
        compiler_params=_cparams("parallel", "arbitrary"),
        name=name,
    )(x, w)


HGRN_GROUP = 4
HGRN_ROWS = HGRN_GROUP * HGRN_CHUNK


def _hgrn_kernel(tab_ref, q_ref, ff_ref, fb_ref, i_ref, g_ref, ng_ref, o_ref,
                 ofw_ref, obw_ref, qef_ref, kef_ref, k2f_ref, qbf_ref, decf_ref,
                 qeb_ref, keb_ref, k2b_ref, qbb_ref, decb_ref, *, layer, seq):
    C, G, R, hd = HGRN_CHUNK, HGRN_GROUP, HGRN_ROWS, A_HEAD_DIM
    n_groups = seq // R
    tab = tab_ref[...]
    e = jnp.exp(tab - jnp.max(tab, axis=0, keepdims=True))
    sm = e / jnp.sum(e, axis=0, keepdims=True)
    lb = jnp.sum(sm[:layer + 1], axis=0, keepdims=True)

    row = lax.broadcasted_iota(jnp.int32, (R, R), 0)
    col = lax.broadcasted_iota(jnp.int32, (R, R), 1)
    same = (row // C) == (col // C)
    lower = same & (col <= row)
    upper = same & (col >= row)
    X = BF16_SUBLANES
    jrow = lax.broadcasted_iota(jnp.int32, (X, R), 0)
    jcol = lax.broadcasted_iota(jnp.int32, (X, R), 1)
    in_chunk = (jcol // C) == jrow
    last_rows = in_chunk.astype(BF16)
    mid_f = (in_chunk & (jcol % C <= C // 2 - 1)).astype(BF16)
    mid_b = (in_chunk & (jcol % C >= C // 2)).astype(BF16)
    cum_f = jnp.concatenate([lower.astype(BF16), mid_f, last_rows], axis=0)
    cum_b = jnp.concatenate([upper.astype(BF16), mid_b, last_rows], axis=0)

    for ref in (k2f_ref, qbf_ref, k2b_ref, qbb_ref):
        ref[...] = jnp.zeros(ref.shape, BF16)

    def per_chunk(extra, j0):
        return jnp.concatenate(
            [jnp.broadcast_to(extra[j0 + c:j0 + c + 1], (C, hd)) for c in range(G)], axis=0)

    def prepare(start, d, slot):
        f_ref, cum = d["f"], d["cum"]
        rows = pl.ds(start, R)
        f = lb + (1.0 - lb) * jax.nn.sigmoid(f_ref[rows, :])
        k = 1.0 - f
        lf = jnp.log(f)
        hi = lf.astype(BF16)
        r1 = lf - hi.astype(F32)
        mid = r1.astype(BF16)
        lo = (r1 - mid.astype(F32)).astype(BF16)
        yield
        ball = jnp.dot(cum, jnp.concatenate([hi, mid, lo], axis=1), preferred_element_type=F32)
        ball = ball[:, 0:hd] + ball[:, hd:2 * hd] + ball[:, 2 * hd:3 * hd]
        yield
        b = ball[0:R]
        b_mid = per_chunk(ball, R)
        b_last = per_chunk(ball, R + X)
        q = q_ref[rows, :].astype(F32)
        d["qe"][slot] = (q * jnp.exp(b - b_mid)).astype(BF16)
        d["ke"][slot] = (k * jnp.exp(b_mid - b)).astype(BF16)
        yield
        qb = (q * jnp.exp(b)).astype(BF16)
        k2 = (k * jnp.exp(b_last - b)).astype(BF16)
        for c in range(G):
            d["k2"][slot, c * C:(c + 1) * C, c * hd:(c + 1) * hd] = k2[c * C:(c + 1) * C]
            d["qb"][slot, c * C:(c + 1) * C, c * hd:(c + 1) * hd] = qb[c * C:(c + 1) * C]
        d["dec"][slot] = jnp.exp(ball[R + X:R + X + 8])
        yield

    def apply(start, d, slot, st):
        rows = pl.ds(start, R)
        v = i_ref[rows, :]
        sc = lax.dot_general(d["qe"][slot], d["ke"][slot], NT_DIMS, preferred_element_type=F32)
        sc = jnp.where(d["mask"], sc, 0.0)
        yield
        o = jnp.dot(sc.astype(BF16), v, preferred_element_type=F32)
        kv_t = jnp.dot(v.astype(F32).T.astype(BF16), d["k2"][slot], preferred_element_type=F32)
        yield
        dec = d["dec"][slot]
        entering = [None] * G
        for c in (reversed(range(G)) if d["reverse"] else range(G)):
            entering[c] = st.astype(BF16)
            st = st * dec[c:c + 1] + kv_t[:, c * hd:(c + 1) * hd]
        yield
        o = o + lax.dot_general(d["qb"][slot], jnp.concatenate(entering, axis=1), NT_DIMS,
                                preferred_element_type=F32)
        d["out"][rows, :] = o
        return st

    def interleave(*gens):
        results = [None] * len(gens)
        active = list(enumerate(gens))
        while active:
            still = []
            for idx, g in active:
                try:
                    next(g)
                    still.append((idx, g))
                except StopIteration as done:
                    results[idx] = done.value
            active = still
        return results

    fw = dict(f=ff_ref, cum=cum_f, mask=lower, reverse=False, out=ofw_ref,
              qe=qef_ref, ke=kef_ref, k2=k2f_ref, qb=qbf_ref, dec=decf_ref)
    bw = dict(f=fb_ref, cum=cum_b, mask=upper, reverse=True, out=obw_ref,
              qe=qeb_ref, ke=keb_ref, k2=k2b_ref, qb=qbb_ref, dec=decb_ref)

    def starts(g):
        g = jnp.minimum(g, n_groups - 1)
        return pl.multiple_of(g * R, R), pl.multiple_of((n_groups - 1 - g) * R, R)

    s_f, s_b = starts(0)
    interleave(prepare(s_f, fw, 0), prepare(s_b, bw, 0))

    ng = ng_ref[...]

    def finish(start):
        rows = pl.ds(start, R)
        o = ofw_ref[rows, :] + obw_ref[rows, :]
        ms = jnp.mean(o * o, axis=-1, keepdims=True)
        yield
        y = o * lax.rsqrt(ms + RMS_EPS) * ng
        g = g_ref[rows, :].astype(F32)
        yield
        o_ref[rows, :] = (y * (g * jax.nn.sigmoid(g))).astype(o_ref.dtype)

    def half(g, carry, slot, done):
        st_f, st_b = carry
        a_f, a_b = starts(g)
        p_f, p_b = starts(g + 1)
        gens = [prepare(p_f, fw, 1 - slot), prepare(p_b, bw, 1 - slot),
                apply(a_f, fw, slot, st_f), apply(a_b, bw, slot, st_b)]
        if done is not None:
            gens += [finish(r) for r in starts(done)]
        res = interleave(*gens)
        return res[2], res[3]

    assert n_groups % 4 == 0
    mid = n_groups // 2

    def early(n, carry):
        for slot in (0, 1):
            carry = half(2 * n + slot, carry, slot, None)
        return carry

    def late(n, carry):
        for slot in (0, 1):
            carry = half(2 * n + slot, carry, slot, 2 * n + slot - 1)
        return carry

    zero = jnp.zeros((hd, hd), F32)
    carry = lax.fori_loop(0, mid // 2, early, (zero, zero))
    carry = half(mid, carry, 0, None)
    carry = half(mid + 1, carry, 1, mid)
    lax.fori_loop(mid // 2 + 1, n_groups // 2, late, carry)
    interleave(*[finish(r) for r in starts(n_groups - 1)])


def _hgrn(h_act, h_gate, lb_table, norm_g, layer, batch, seq):
    hd = A_HEAD_DIM

    def col(part):
        return pl.BlockSpec((None, seq, hd), lambda b, h: (b, 0, part * A_HEADS + h))

    n_tab = lb_table.shape[0]
    return pl.pallas_call(
        functools.partial(_hgrn_kernel, layer=layer, seq=seq),
        grid=(batch, A_HEADS),
        in_specs=[pl.BlockSpec((n_tab, hd), lambda b, h: (0, h)),
                  col(0), col(0), col(1), col(1), col(2),
                  pl.BlockSpec((1, hd), lambda b, h: (0, h))],
        out_specs=pl.BlockSpec((None, seq, hd), lambda b, h: (b, 0, h)),
        out_shape=jax.ShapeDtypeStruct((batch, seq, A_WIDTH), BF16),
        scratch_shapes=[pltpu.VMEM((seq, hd), F32), pltpu.VMEM((seq, hd), F32)]
        + 2 * [pltpu.VMEM((2, HGRN_ROWS, hd), BF16), pltpu.VMEM((2, HGRN_ROWS, hd), BF16),
               pltpu.VMEM((2, HGRN_ROWS, HGRN_GROUP * hd), BF16),
               pltpu.VMEM((2, HGRN_ROWS, HGRN_GROUP * hd), BF16),
               pltpu.VMEM((2, 8, hd), F32)],
        compiler_params=_cparams("parallel", "parallel"),
        name="hgrn2",
    )(lb_table, h_act, h_gate, h_gate, h_act, h_act, norm_g)


def _gmlp_kernel(u_ref, v_ref, lg_ref, lbias_ref, ws_ref, bias_ref, o_ref, *, rows):
    vln = _layer_norm(v_ref[...].astype(F32), lg_ref[...], lbias_ref[...]).astype(BF16)
    for c in range(rows // B_CHUNK):
        r = slice(c * B_CHUNK, (c + 1) * B_CHUNK)
        for g in range(B_GROUPS):
            cs = slice(g * B_GROUP_DIM, (g + 1) * B_GROUP_DIM)
            s = jnp.dot(ws_ref[g], vln[r, cs], preferred_element_type=F32) + bias_ref[g]
            o_ref[r, cs] = (u_ref[r, cs] * s).astype(o_ref.dtype)


def _gmlp(h_act, u_blk, ln_g, ln_b, ws, bias, rows):
    M = h_act.shape[0]
    bias_b = jnp.broadcast_to(bias[:, :, None], (B_GROUPS, B_CHUNK, B_GROUP_DIM))
    return pl.pallas_call(
        functools.partial(_gmlp_kernel, rows=rows),
        grid=(M // rows,),
        in_specs=[pl.BlockSpec((rows, B_WIDTH), lambda i: (i, u_blk)),
                  pl.BlockSpec((rows, B_WIDTH), lambda i: (i, u_blk + 1)),
                  pl.BlockSpec((1, B_WIDTH), lambda i: (0, 0)),
                  pl.BlockSpec((1, B_WIDTH), lambda i: (0, 0)),
                  pl.BlockSpec((B_GROUPS, B_CHUNK, B_CHUNK), lambda i: (0, 0, 0)),
                  pl.BlockSpec((B_GROUPS, B_CHUNK, B_GROUP_DIM), lambda i: (0, 0, 0))],
        out_specs=pl.BlockSpec((rows, B_WIDTH), lambda i: (i, 0)),
        out_shape=jax.ShapeDtypeStruct((M, B_WIDTH), BF16),
        compiler_params=_cparams("parallel"),
        name="gmlp",
    )(h_act, h_act, ln_g.reshape(1, -1), ln_b.reshape(1, -1), ws.astype(BF16), bias_b)


PROJ_STAGE_ROWS = 512


def _proj_ln_kernel(*refs, n_in, n_out, layer):
    ins = refs[:n_in]
    w_hbm, x_ref, g_ref, b_ref = refs[n_in:n_in + 4]
    out_refs = refs[n_in + 4:n_in + 4 + n_out]
    wb_ref, stage_ref, sem = refs[n_in + 4 + n_out:]
    K = wb_ref.shape[0]
    n_stage = K // PROJ_STAGE_ROWS

    @pl.when(pl.program_id(0) == 0)
    def _():
        def chunk_copy(c):
            rows = pl.ds(c * PROJ_STAGE_ROWS, PROJ_STAGE_ROWS)
            return pltpu.make_async_copy(w_hbm.at[layer, rows, :], stage_ref.at[c % 2], sem.at[c % 2])

        chunk_copy(0).start()
        for c in range(n_stage):
            if c + 1 < n_stage:
                chunk_copy(c + 1).start()
            chunk_copy(c).wait()
            wb_ref[c * PROJ_STAGE_ROWS:(c + 1) * PROJ_STAGE_ROWS, :] = stage_ref[c % 2].astype(BF16)

    acc = None
    off = 0
    for r in ins:
        kk = r.shape[1]
        d = jnp.dot(r[...], wb_ref[off:off + kk, :], preferred_element_type=F32)
        acc = d if acc is None else acc + d
        off += kk
    out = _layer_norm(ALPHA * x_ref[...] + acc, g_ref[...], b_ref[...])
    for o_ref in out_refs:
        o_ref[...] = out.astype(o_ref.dtype)


def _proj_ln(parts, w, layer, x, g, b, tm, out_dtypes, name):
    M, D = x.shape
    K = w.shape[1]
    assert K % PROJ_STAGE_ROWS == 0
    in_specs = [pl.BlockSpec((tm, p.shape[1]), lambda i: (i, 0)) for p in parts]
    in_specs += [pl.BlockSpec(memory_space=pl.ANY),
                 pl.BlockSpec((tm, D), lambda i: (i, 0)),
                 pl.BlockSpec((1, D), lambda i: (0, 0)),
                 pl.BlockSpec((1, D), lambda i: (0, 0))]
    return pl.pallas_call(
        functools.partial(_proj_ln_kernel, n_in=len(parts), n_out=len(out_dtypes), layer=layer),
        grid=(M // tm,),
        in_specs=in_specs,
        out_specs=[pl.BlockSpec((tm, D), lambda i: (i, 0)) for _ in out_dtypes],
        out_shape=[jax.ShapeDtypeStruct((M, D), dt) for dt in out_dtypes],
        scratch_shapes=[pltpu.VMEM((K, D), BF16),
                        pltpu.VMEM((2, PROJ_STAGE_ROWS, D), F32),
                        pltpu.SemaphoreType.DMA((2,))],
        compiler_params=_cparams("arbitrary"),
        name=name,
    )(*parts, w, x, g.reshape(1, -1), b.reshape(1, -1))


def _ffn_up_kernel(xm_ref, xp_ref, xn_ref, wg_ref, wv_ref, cwg_ref, cwv_ref, cbg_ref, cbv_ref,
                   o_ref, lhs_ref, wgb_ref, wvb_ref, *, tm, seq):
    H = BF16_SUBLANES
    i = pl.program_id(1)

    @pl.when(i == 0)
    def _():
        wgb_ref[...] = wg_ref[...].astype(BF16)
        wvb_ref[...] = wv_ref[...].astype(BF16)

    t0 = (i * tm) % seq
    zero = jnp.zeros((H, xm_ref.shape[1]), BF16)
    lhs_ref[0:H, :] = jnp.where(t0 != 0, xp_ref[...], zero)
    lhs_ref[H:H + tm, :] = xm_ref[...]
    lhs_ref[H + tm:2 * H + tm, :] = jnp.where(t0 + tm != seq, xn_ref[...], zero)
    lhs = lhs_ref[...]

    def conv(w_ref, cw_ref, cb_ref):
        h = jnp.dot(lhs, w_ref[...], preferred_element_type=F32)
        cw = cw_ref[...]
        n = h.shape[0]
        prev = pltpu.roll(h, 1, 0)[H:H + tm]
        nxt = pltpu.roll(h, n - 1, 0)[H:H + tm]
        return prev * cw[0:1] + h[H:H + tm] * cw[1:2] + nxt * cw[2:3] + cb_ref[...]

    gate = conv(wgb_ref, cwg_ref, cbg_ref)
    val = conv(wvb_ref, cwv_ref, cbv_ref)
    o_ref[...] = (gate * jax.nn.sigmoid(gate) * val).astype(o_ref.dtype)


def _ffn_up(xb, w_up, conv_w, conv_b, layer, seq, tm, tn):
    M, D = xb.shape
    H = BF16_SUBLANES
    nj = D_FF // tn
    hb = tm // H
    n_hblk = M // H
    conv_b = conv_b.reshape(conv_b.shape[0], 1, -1)
    return pl.pallas_call(
        functools.partial(_ffn_up_kernel, tm=tm, seq=seq),
        grid=(nj, M // tm),
        in_specs=[pl.BlockSpec((tm, D), lambda j, i: (i, 0)),
                  pl.BlockSpec((H, D), lambda j, i: (jnp.maximum(i * hb - 1, 0), 0)),
                  pl.BlockSpec((H, D), lambda j, i: (jnp.minimum((i + 1) * hb, n_hblk - 1), 0)),
                  pl.BlockSpec((None, D, tn), lambda j, i: (layer, 0, j)),
                  pl.BlockSpec((None, D, tn), lambda j, i: (layer, 0, j + nj)),
                  pl.BlockSpec((None, 3, tn), lambda j, i: (layer, 0, j)),
                  pl.BlockSpec((None, 3, tn), lambda j, i: (layer, 0, j + nj)),
                  pl.BlockSpec((None, 1, tn), lambda j, i: (layer, 0, j)),
                  pl.BlockSpec((None, 1, tn), lambda j, i: (layer, 0, j + nj))],
        out_specs=pl.BlockSpec((tm, tn), lambda j, i: (i, j)),
        out_shape=jax.ShapeDtypeStruct((M, D_FF), BF16),
        scratch_shapes=[pltpu.VMEM((tm + 2 * H, D), BF16),
                        pltpu.VMEM((D, tn), BF16), pltpu.VMEM((D, tn), BF16)],
        compiler_params=_cparams("parallel", "arbitrary"),
        name="ffn_up_conv",
    )(xb, xb, xb, w_up, w_up, conv_w, conv_w, conv_b, conv_b)


def _rope_tables(seq):
    t = np.arange(seq)
    r = (t // GRID_W).astype(np.float64)
    c = (t % GRID_W).astype(np.float64)
    half = C_HEAD_DIM // 2
    inv_freq = np.exp(-math.log(ROPE_THETA) * np.arange(0, half, 2, dtype=np.float64) / half)
    ang_r = r[:, None] * inv_freq
    ang_c = c[:, None] * inv_freq
    ang = np.concatenate([ang_r, ang_r, ang_c, ang_c], axis=-1)
    sign = np.tile(np.concatenate([-np.ones(half // 2), np.ones(half // 2)]), 2)
    return jnp.asarray(np.cos(ang), F32), jnp.asarray(np.sin(ang) * sign, F32)


def _qkv_kernel(x_ref, w_ref, gain_ref, cos_ref, sin_ref, o_ref, wb_ref, h_ref,
                *, rows_per_col, n_tiles, n_norm_heads):
    s = pl.program_id(0)
    hd = C_HEAD_DIM

    @pl.when(s == 0)
    def _():
        h_ref[1] = jnp.zeros(h_ref.shape[1:], F32)

    @pl.when((s % rows_per_col == 0) & (s < n_tiles))
    def _():
        wb_ref[...] = w_ref[...].astype(BF16)

    heads_per_tile = o_ref.shape[1] // hd
    first_head = (jnp.maximum(s - 1, 0) // rows_per_col) * heads_per_tile

    def step(fill, drain):
        cos = cos_ref[...]
        sin = sin_ref[...]
        lane = lax.broadcasted_iota(jnp.int32, (1, hd), 1)
        low = (lane % (hd // 2)) < (hd // 4)
        for h in range(heads_per_tile):
            cs = slice(h * hd, (h + 1) * hd)
            a = h_ref[drain, :, cs]
            y = a * lax.rsqrt(jnp.mean(a * a, axis=-1, keepdims=True) + RMS_EPS) * gain_ref[:, cs]
            partner = jnp.where(low, pltpu.roll(y, hd - hd // 4, 1), pltpu.roll(y, hd // 4, 1))
            plain = first_head + h >= n_norm_heads
            o_ref[:, cs] = jnp.where(plain, a, y * cos + partner * sin).astype(o_ref.dtype)
        h_ref[fill] = jnp.dot(x_ref[...], wb_ref[...], preferred_element_type=F32)

    @pl.when(s % 2 == 0)
    def _():
        step(0, 1)

    @pl.when(s % 2 == 1)
    def _():
        step(1, 0)


def _qkv_proj(xb, w, layer, gain, cos, sin, seq, tm, tn):
    M, D = xb.shape
    N = w.shape[2]
    n_pos_blk = seq // tm
    ni = M // tm
    n_tiles = (N // tn) * ni

    def fill(s):
        t = jnp.minimum(s, n_tiles - 1)
        return t // ni, t % ni

    def drain(s):
        t = jnp.maximum(s - 1, 0)
        return t // ni, t % ni

    return pl.pallas_call(
        functools.partial(_qkv_kernel, rows_per_col=ni, n_tiles=n_tiles,
                          n_norm_heads=C_Q_HEADS + C_KV_HEADS),
        grid=(n_tiles + 1,),
        in_specs=[pl.BlockSpec((tm, D), lambda s: (fill(s)[1], 0)),
                  pl.BlockSpec((None, D, tn), lambda s: (layer, 0, fill(s)[0])),
                  pl.BlockSpec((1, tn), lambda s: (0, drain(s)[0])),
                  pl.BlockSpec((tm, C_HEAD_DIM), lambda s: (drain(s)[1] % n_pos_blk, 0)),
                  pl.BlockSpec((tm, C_HEAD_DIM), lambda s: (drain(s)[1] % n_pos_blk, 0))],
        out_specs=pl.BlockSpec((tm, tn), lambda s: (drain(s)[1], drain(s)[0])),
        out_shape=jax.ShapeDtypeStruct((M, N), BF16),
        scratch_shapes=[pltpu.VMEM((D, tn), BF16), pltpu.VMEM((2, tm, tn), F32)],
        compiler_params=_cparams("arbitrary"),
        name="qkv_norm_rope",
    )(xb, w, gain, cos, sin)


def _attn_kernel(q_ref, k_ref, v_ref, o_ref, vt_ref, st_ref, m_ref, *, tq, n_sub, blocks_per_head, n_chunks):
    hd = C_HEAD_DIM
    s = pl.program_id(0)
    seq = k_ref.shape[0]
    ck = seq // n_chunks

    @pl.when(s == 0)
    def _():
        st_ref[1] = jnp.zeros(st_ref.shape[1:], F32)
        m_ref[1] = jnp.zeros(m_ref.shape[1:], F32)

    @pl.when(jnp.maximum(s - 1, 0) % blocks_per_head == 0)
    def _():
        vt_ref[0:hd, :] = v_ref[...].astype(F32).T.astype(BF16)
        vt_ref[hd:, :] = jnp.ones((vt_ref.shape[0] - hd, seq), BF16)

    sub = tq // n_sub

    def write_out(a, acc):
        o = acc[0:hd] * (1.0 / acc[hd:hd + 1])
        for g in range(C_GROUP):
            o_ref[a * sub:(a + 1) * sub, g * hd:(g + 1) * hd] = (
                o[:, g * sub:(g + 1) * sub].T.astype(o_ref.dtype))

    def step(fill, drain):
        qs = [jnp.concatenate([q_ref[a * sub:(a + 1) * sub, g * hd:(g + 1) * hd] for g in range(C_GROUP)],
                              axis=0) for a in range(n_sub)]
        pending = None
        for a in range(n_sub):
            m_drain = m_ref[drain, a]
            m_fill = None
            acc = None
            for c in range(n_chunks):
                rows = slice(c * ck, (c + 1) * ck)
                sc = lax.dot_general(k_ref[rows, :], qs[a], NT_DIMS, preferred_element_type=F32)
                st_ref[fill, a, rows, :] = sc
                mc = jnp.max(sc, axis=0, keepdims=True)
                m_fill = mc if c == 0 else jnp.maximum(m_fill, mc)
                if c == 1 and pending is not None:
                    write_out(*pending)
                    pending = None
                p = jnp.exp2(st_ref[drain, a, rows, :] - m_drain).astype(BF16)
                pv = jnp.dot(vt_ref[:, rows], p, preferred_element_type=F32)
                acc = pv if c == 0 else acc + pv
            m_ref[fill, a] = m_fill
            pending = (a, acc)
        write_out(*pending)

    @pl.when(s % 2 == 0)
    def _():
        step(0, 1)

    @pl.when(s % 2 == 1)
    def _():
        step(1, 0)


def _attention(qkv, batch, seq, tq, n_sub, n_chunks):
    hd = C_HEAD_DIM
    gw = C_GROUP * hd
    bph = seq // tq
    n_blocks = batch * C_KV_HEADS * bph

    def coords(blk):
        return blk // (C_KV_HEADS * bph), (blk // bph) % C_KV_HEADS, blk % bph

    def q_map(s):
        b, h, i = coords(jnp.minimum(s, n_blocks - 1))
        return b, i, h

    def k_map(s):
        b, h, _ = coords(jnp.minimum(s, n_blocks - 1))
        return b, 0, C_Q_HEADS + h

    def v_map(s):
        b, h, _ = coords(jnp.maximum(s - 1, 0))
        return b, 0, C_Q_HEADS + C_KV_HEADS + h

    def o_map(s):
        b, h, i = coords(jnp.maximum(s - 1, 0))
        return b, i, h

    return pl.pallas_call(
        functools.partial(_attn_kernel, tq=tq, n_sub=n_sub, blocks_per_head=bph, n_chunks=n_chunks),
        grid=(n_blocks + 1,),
        in_specs=[pl.BlockSpec((None, tq, gw), q_map),
                  pl.BlockSpec((None, seq, hd), k_map),
                  pl.BlockSpec((None, seq, hd), v_map)],
        out_specs=pl.BlockSpec((None, tq, gw), o_map),
        out_shape=jax.ShapeDtypeStruct((batch, seq, C_Q_HEADS * hd), BF16),
        scratch_shapes=[pltpu.VMEM((hd + BF16_SUBLANES, seq), BF16),
                        pltpu.VMEM((2, n_sub, seq, C_GROUP * tq // n_sub), F32),
                        pltpu.VMEM((2, n_sub, 1, C_GROUP * tq // n_sub), F32)],
        compiler_params=_cparams("arbitrary"),
        name="gqa_attention",
    )(qkv, qkv, qkv)


def _conv_ffn_ln(x, xb, w_up, conv_w, conv_b, w_down, g, b, layer, seq, out_dtypes):
    act = _ffn_up(xb, w_up, conv_w, conv_b, layer, seq, tm=1024, tn=512)
    return _proj_ln([act], w_down, layer, x, g[layer], b[layer], 256, out_dtypes, "ffn_down_ln")


def kernel(x, w_in_ab, hgrn_lb_table, hgrn_norm_g, gmlp_ln_g, gmlp_ln_b, gmlp_ws, gmlp_bias,
           w_out_ab, w_in_attn, q_norm_g, k_norm_g, w_out_attn, ffn_up, ffn_conv_w, ffn_conv_b,
           ffn_down, ln1_g, ln1_b, ln2_g, ln2_b):
    batch, seq, d = x.shape
    M = batch * seq
    x = x.reshape(M, d)
    xb = x.astype(BF16)

    h_gate = _matmul(xb, w_in_ab, 0, 2, lambda j: j + 1, 1024, 1024, F32, "inproj_gates")
    h_act = _matmul(xb, w_in_ab, 0, 5, lambda j: j + 2 * jnp.minimum(j, 1), 1024, 1024, BF16, "inproj_acts")
    o_a = _hgrn(h_act.reshape(batch, seq, -1), h_gate.reshape(batch, seq, -1), hgrn_lb_table,
                hgrn_norm_g[0].reshape(1, -1), 0, batch, seq)
    o_b = _gmlp(h_act, 3, gmlp_ln_g[0], gmlp_ln_b[0], gmlp_ws[0], gmlp_bias[0], rows=512)
    both = (F32, BF16)
    x, xb = _proj_ln([o_a.reshape(M, -1), o_b], w_out_ab, 0, x, ln1_g[0], ln1_b[0], 512, both, "outproj_ln")
    x, xb = _conv_ffn_ln(x, xb, ffn_up, ffn_conv_w, ffn_conv_b, ffn_down, ln2_g, ln2_b, 0, seq, both)

    scale = C_HEAD_DIM ** -0.5 * math.log2(math.e)
    gain = jnp.concatenate([jnp.tile(q_norm_g[0] * scale, C_Q_HEADS), jnp.tile(k_norm_g[0], C_KV_HEADS),
                            jnp.ones((C_KV_HEADS * C_HEAD_DIM,), F32)])
    cos, sin = _rope_tables(seq)
    qkv = _qkv_proj(xb, w_in_attn, 0, gain.reshape(1, -1), cos, sin, seq, tm=1024, tn=1024)
    att = _attention(qkv.reshape(batch, seq, -1), batch, seq, tq=256, n_sub=2, n_chunks=8)
    x, xb = _proj_ln([att.reshape(M, -1)], w_out_attn, 0, x, ln1_g[1], ln1_b[1], 512, both, "outproj_ln")
    (x,) = _conv_ffn_ln(x, xb, ffn_up, ffn_conv_w, ffn_conv_b, ffn_down, ln2_g, ln2_b, 1, seq, (F32,))
    return x.reshape(batch, seq, d)
```

```python
import functools
import math

import jax
import jax.numpy as jnp
import numpy as np
from jax import lax
from jax.experimental import pallas as pl
from jax.experimental.pallas import tpu as pltpu

F32 = jnp.float32
BF16 = jnp.bfloat16

D_MODEL = 2048
GRID_W = 64
A_HEAD_DIM = 128
A_WIDTH = D_MODEL // 2
A_HEADS = A_WIDTH // A_HEAD_DIM
HGRN_CHUNK = 64
B_WIDTH = D_MODEL // 2
B_GROUP_DIM = 128
B_GROUPS = B_WIDTH // B_GROUP_DIM
B_CHUNK = 128
C_HEAD_DIM = 128
C_Q_HEADS = D_MODEL // C_HEAD_DIM
C_KV_HEADS = C_Q_HEADS // 4
C_GROUP = C_Q_HEADS // C_KV_HEADS
ROPE_THETA = 10000.0
D_FF = 5632
DEPTH = 2
ALPHA = (2.0 * DEPTH) ** 0.25
LN_EPS = 1e-5
RMS_EPS = 1e-6

V7X_VMEM_LIMIT_BYTES = 56 * 1024 * 1024
BF16_SUBLANES = 16

NT_DIMS = (((1,), (1,)), ((), ()))


def _cparams(*sem):
    return pltpu.CompilerParams(dimension_semantics=sem, vmem_limit_bytes=V7X_VMEM_LIMIT_BYTES)


def _layer_norm(y, g, b):
    mu = jnp.mean(y, axis=-1, keepdims=True)
    yc = y - mu
    var = jnp.mean(yc * yc, axis=-1, keepdims=True)
    return yc * lax.rsqrt(var + LN_EPS) * g + b


def _mm_kernel(x_ref, w_ref, o_ref, wb_ref):
    @pl.when(pl.program_id(1) == 0)
    def _():
        wb_ref[...] = w_ref[...].astype(BF16)

    o_ref[...] = jnp.dot(x_ref[...], wb_ref[...], preferred_element_type=F32).astype(o_ref.dtype)


def _matmul(x, w, layer, n_col_tiles, col_tile_of, tm, tn, out_dtype, name):
    M, K = x.shape
    return pl.pallas_call(
        _mm_kernel,
        grid=(n_col_tiles, M // tm),
        in_specs=[pl.BlockSpec((tm, K), lambda j, i: (i, 0)),
                  pl.BlockSpec((None, K, tn), lambda j, i: (layer, 0, col_tile_of(j)))],
        out_specs=pl.BlockSpec((tm, tn), lambda j, i: (i, j)),
        out_shape=jax.ShapeDtypeStruct((M, n_col_tiles * tn), out_dtype),
        scratch_shapes=[pltpu.VMEM((K, tn), BF16)],
        compiler_params=_cparams("parallel", "arbitrary"),
        name=name,
    )(x, w)


HGRN_GROUP = 4
HGRN_ROWS = HGRN_GROUP * HGRN_CHUNK


def _hgrn_kernel(tab_ref, q_ref, ff_ref, fb_ref, i_ref, g_ref, ng_ref, o_ref,
                 ofw_ref, obw_ref, qef_ref, kef_ref, k2f_ref, qbf_ref, decf_ref,
                 qeb_ref, keb_ref, k2b_ref, qbb_ref, decb_ref, *, layer, seq):
    C, G, R, hd = HGRN_CHUNK, HGRN_GROUP, HGRN_ROWS, A_HEAD_DIM
    n_groups = seq // R
    tab = tab_ref[...]
    e = jnp.exp(tab - jnp.max(tab, axis=0, keepdims=True))
    sm = e / jnp.sum(e, axis=0, keepdims=True)
    lb = jnp.sum(sm[:layer + 1], axis=0, keepdims=True)

    row = lax.broadcasted_iota(jnp.int32, (R, R), 0)
    col = lax.broadcasted_iota(jnp.int32, (R, R), 1)
    same = (row // C) == (col // C)
    lower = same & (col <= row)
    upper = same & (col >= row)
    X = BF16_SUBLANES
    jrow = lax.broadcasted_iota(jnp.int32, (X, R), 0)
    jcol = lax.broadcasted_iota(jnp.int32, (X, R), 1)
    in_chunk = (jcol // C) == jrow
    last_rows = in_chunk.astype(BF16)
    mid_f = (in_chunk & (jcol % C <= C // 2 - 1)).astype(BF16)
    mid_b = (in_chunk & (jcol % C >= C // 2)).astype(BF16)
    cum_f = jnp.concatenate([lower.astype(BF16), mid_f, last_rows], axis=0)
    cum_b = jnp.concatenate([upper.astype(BF16), mid_b, last_rows], axis=0)

    for ref in (k2f_ref, qbf_ref, k2b_ref, qbb_ref):
        ref[...] = jnp.zeros(ref.shape, BF16)

    def per_chunk(extra, j0):
        return jnp.concatenate(
            [jnp.broadcast_to(extra[j0 + c:j0 + c + 1], (C, hd)) for c in range(G)], axis=0)

    def prepare(start, d, slot):
        f_ref, cum = d["f"], d["cum"]
        rows = pl.ds(start, R)
        f = lb + (1.0 - lb) * jax.nn.sigmoid(f_ref[rows, :])
        k = 1.0 - f
        lf = jnp.log(f)
        hi = lf.astype(BF16)
        r1 = lf - hi.astype(F32)
        mid = r1.astype(BF16)
        lo = (r1 - mid.astype(F32)).astype(BF16)
        yield
        ball = jnp.dot(cum, jnp.concatenate([hi, mid, lo], axis=1), preferred_element_type=F32)
        ball = ball[:, 0:hd] + ball[:, hd:2 * hd] + ball[:, 2 * hd:3 * hd]
        yield
        b = ball[0:R]
        b_mid = per_chunk(ball, R)
        b_last = per_chunk(ball, R + X)
        q = q_ref[rows, :].astype(F32)
        d["qe"][slot] = (q * jnp.exp(b - b_mid)).astype(BF16)
        d["ke"][slot] = (k * jnp.exp(b_mid - b)).astype(BF16)
        yield
        qb = (q * jnp.exp(b)).astype(BF16)
        k2 = (k * jnp.exp(b_last - b)).astype(BF16)
        for c in range(G):
            d["k2"][slot, c * C:(c + 1) * C, c * hd:(c + 1) * hd] = k2[c * C:(c + 1) * C]
            d["qb"][slot, c * C:(c + 1) * C, c * hd:(c + 1) * hd] = qb[c * C:(c + 1) * C]
        d["dec"][slot] = jnp.exp(ball[R + X:R + X + 8])
        yield

    def apply(start, d, slot, st):
        rows = pl.ds(start, R)
        v = i_ref[rows, :]
        sc = lax.dot_general(d["qe"][slot], d["ke"][slot], NT_DIMS, preferred_element_type=F32)
        sc = jnp.where(d["mask"], sc, 0.0)
        yield
        o = jnp.dot(sc.astype(BF16), v, preferred_element_type=F32)
        kv_t = jnp.dot(v.astype(F32).T.astype(BF16), d["k2"][slot], preferred_element_type=F32)
        yield
        dec = d["dec"][slot]
        entering = [None] * G
        for c in (reversed(range(G)) if d["reverse"] else range(G)):
            entering[c] = st.astype(BF16)
            st = st * dec[c:c + 1] + kv_t[:, c * hd:(c + 1) * hd]
        yield
        o = o + lax.dot_general(d["qb"][slot], jnp.concatenate(entering, axis=1), NT_DIMS,
                                preferred_element_type=F32)
        d["out"][rows, :] = o
        return st

    def interleave(*gens):
        results = [None] * len(gens)
        active = list(enumerate(gens))
        while active:
            still = []
            for idx, g in active:
                try:
                    next(g)
                    still.append((idx, g))
                except StopIteration as done:
                    results[idx] = done.value
            active = still
        return results

    fw = dict(f=ff_ref, cum=cum_f, mask=lower, reverse=False, out=ofw_ref,
              qe=qef_ref, ke=kef_ref, k2=k2f_ref, qb=qbf_ref, dec=decf_ref)
    bw = dict(f=fb_ref, cum=cum_b, mask=upper, reverse=True, out=obw_ref,
              qe=qeb_ref, ke=keb_ref, k2=k2b_ref, qb=qbb_ref, dec=decb_ref)

    def starts(g):
        g = jnp.minimum(g, n_groups - 1)
        return pl.multiple_of(g * R, R), pl.multiple_of((n_groups - 1 - g) * R, R)

    s_f, s_b = starts(0)
    interleave(prepare(s_f, fw, 0), prepare(s_b, bw, 0))

    ng = ng_ref[...]

    def finish(start):
        rows = pl.ds(start, R)
        o = ofw_ref[rows, :] + obw_ref[rows, :]
        ms = jnp.mean(o * o, axis=-1, keepdims=True)
        yield
        y = o * lax.rsqrt(ms + RMS_EPS) * ng
        g = g_ref[rows, :].astype(F32)
        yield
        o_ref[rows, :] = (y * (g * jax.nn.sigmoid(g))).astype(o_ref.dtype)

    def half(g, carry, slot, done):
        st_f, st_b = carry
        a_f, a_b = starts(g)
        p_f, p_b = starts(g + 1)
        gens = [prepare(p_f, fw, 1 - slot), prepare(p_b, bw, 1 - slot),
                apply(a_f, fw, slot, st_f), apply(a_b, bw, slot, st_b)]
        if done is not None:
            gens += [finish(r) for r in starts(done)]
        res = interleave(*gens)
        return res[2], res[3]

    assert n_groups % 4 == 0
    mid = n_groups // 2

    def early(n, carry):
        for slot in (0, 1):
            carry = half(2 * n + slot, carry, slot, None)
        return carry

    def late(n, carry):
        for slot in (0, 1):
            carry = half(2 * n + slot, carry, slot, 2 * n + slot - 1)
        return carry

    zero = jnp.zeros((hd, hd), F32)
    carry = lax.fori_loop(0, mid // 2, early, (zero, zero))
    carry = half(mid, carry, 0, None)
    carry = half(mid + 1, carry, 1, mid)
    lax.fori_loop(mid // 2 + 1, n_groups // 2, late, carry)
    interleave(*[finish(r) for r in starts(n_groups - 1)])


def _hgrn(h_act, h_gate, lb_table, norm_g, layer, batch, seq):
    hd = A_HEAD_DIM

    def col(part):
        return pl.BlockSpec((None, seq, hd), lambda b, h: (b, 0, part * A_HEADS + h))

    n_tab = lb_table.shape[0]
    return pl.pallas_call(
        functools.partial(_hgrn_kernel, layer=layer, seq=seq),
        grid=(batch, A_HEADS),
        in_specs=[pl.BlockSpec((n_tab, hd), lambda b, h: (0, h)),
                  col(0), col(0), col(1), col(1), col(2),
                  pl.BlockSpec((1, hd), lambda b, h: (0, h))],
        out_specs=pl.BlockSpec((None, seq, hd), lambda b, h: (b, 0, h)),
        out_shape=jax.ShapeDtypeStruct((batch, seq, A_WIDTH), BF16),
        scratch_shapes=[pltpu.VMEM((seq, hd), F32), pltpu.VMEM((seq, hd), F32)]
        + 2 * [pltpu.VMEM((2, HGRN_ROWS, hd), BF16), pltpu.VMEM((2, HGRN_ROWS, hd), BF16),
               pltpu.VMEM((2, HGRN_ROWS, HGRN_GROUP * hd), BF16),
               pltpu.VMEM((2, HGRN_ROWS, HGRN_GROUP * hd), BF16),
               pltpu.VMEM((2, 8, hd), F32)],
        compiler_params=_cparams("parallel", "parallel"),
        name="hgrn2",
    )(lb_table, h_act, h_gate, h_gate, h_act, h_act, norm_g)


def _gmlp_kernel(u_ref, v_ref, lg_ref, lbias_ref, ws_ref, bias_ref, o_ref, *, rows):
    vln = _layer_norm(v_ref[...].astype(F32), lg_ref[...], lbias_ref[...]).astype(BF16)
    for c in range(rows // B_CHUNK):
        r = slice(c * B_CHUNK, (c + 1) * B_CHUNK)
        for g in range(B_GROUPS):
            cs = slice(g * B_GROUP_DIM, (g + 1) * B_GROUP_DIM)
            s = jnp.dot(ws_ref[g], vln[r, cs], preferred_element_type=F32) + bias_ref[g]
            o_ref[r, cs] = (u_ref[r, cs] * s).astype(o_ref.dtype)


def _gmlp(h_act, u_blk, ln_g, ln_b, ws, bias, rows):
    M = h_act.shape[0]
    bias_b = jnp.broadcast_to(bias[:, :, None], (B_GROUPS, B_CHUNK, B_GROUP_DIM))
    return pl.pallas_call(
        functools.partial(_gmlp_kernel, rows=rows),
        grid=(M // rows,),
        in_specs=[pl.BlockSpec((rows, B_WIDTH), lambda i: (i, u_blk)),
                  pl.BlockSpec((rows, B_WIDTH), lambda i: (i, u_blk + 1)),
                  pl.BlockSpec((1, B_WIDTH), lambda i: (0, 0)),
                  pl.BlockSpec((1, B_WIDTH), lambda i: (0, 0)),
                  pl.BlockSpec((B_GROUPS, B_CHUNK, B_CHUNK), lambda i: (0, 0, 0)),
                  pl.BlockSpec((B_GROUPS, B_CHUNK, B_GROUP_DIM), lambda i: (0, 0, 0))],
        out_specs=pl.BlockSpec((rows, B_WIDTH), lambda i: (i, 0)),
        out_shape=jax.ShapeDtypeStruct((M, B_WIDTH), BF16),
        compiler_params=_cparams("parallel"),
        name="gmlp",
    )(h_act, h_act, ln_g.reshape(1, -1), ln_b.reshape(1, -1), ws.astype(BF16), bias_b)


PROJ_STAGE_ROWS = 512


def _proj_ln_kernel(*refs, n_in, n_out, layer):
    ins = refs[:n_in]
    w_hbm, x_ref, g_ref, b_ref = refs[n_in:n_in + 4]
    out_refs = refs[n_in + 4:n_in + 4 + n_out]
    wb_ref, stage_ref, sem = refs[n_in + 4 + n_out:]
    K = wb_ref.shape[0]
    n_stage = K // PROJ_STAGE_ROWS

    @pl.when(pl.program_id(0) == 0)
    def _():
        def chunk_copy(c):
            rows = pl.ds(c * PROJ_STAGE_ROWS, PROJ_STAGE_ROWS)
            return pltpu.make_async_copy(w_hbm.at[layer, rows, :], stage_ref.at[c % 2], sem.at[c % 2])

        chunk_copy(0).start()
        for c in range(n_stage):
            if c + 1 < n_stage:
                chunk_copy(c + 1).start()
            chunk_copy(c).wait()
            wb_ref[c * PROJ_STAGE_ROWS:(c + 1) * PROJ_STAGE_ROWS, :] = stage_ref[c % 2].astype(BF16)

    acc = None
    off = 0
    for r in ins:
        kk = r.shape[1]
        d = jnp.dot(r[...], wb_ref[off:off + kk, :], preferred_element_type=F32)
        acc = d if acc is None else acc + d
        off += kk
    out = _layer_norm(ALPHA * x_ref[...] + acc, g_ref[...], b_ref[...])
    for o_ref in out_refs:
        o_ref[...] = out.astype(o_ref.dtype)


def _proj_ln(parts, w, layer, x, g, b, tm, out_dtypes, name):
    M, D = x.shape
    K = w.shape[1]
    assert K % PROJ_STAGE_ROWS == 0
    in_specs = [pl.BlockSpec((tm, p.shape[1]), lambda i: (i, 0)) for p in parts]
    in_specs += [pl.BlockSpec(memory_space=pl.ANY),
                 pl.BlockSpec((tm, D), lambda i: (i, 0)),
                 pl.BlockSpec((1, D), lambda i: (0, 0)),
                 pl.BlockSpec((1, D), lambda i: (0, 0))]
    return pl.pallas_call(
        functools.partial(_proj_ln_kernel, n_in=len(parts), n_out=len(out_dtypes), layer=layer),
        grid=(M // tm,),
        in_specs=in_specs,
        out_specs=[pl.BlockSpec((tm, D), lambda i: (i, 0)) for _ in out_dtypes],
        out_shape=[jax.ShapeDtypeStruct((M, D), dt) for dt in out_dtypes],
        scratch_shapes=[pltpu.VMEM((K, D), BF16),
                        pltpu.VMEM((2, PROJ_STAGE_ROWS, D), F32),
                        pltpu.SemaphoreType.DMA((2,))],
        compiler_params=_cparams("arbitrary"),
        name=name,
    )(*parts, w, x, g.reshape(1, -1), b.reshape(1, -1))


def _ffn_up_kernel(xm_ref, xp_ref, xn_ref, wg_ref, wv_ref, cwg_ref, cwv_ref, cbg_ref, cbv_ref,
                   o_ref, lhs_ref, wgb_ref, wvb_ref, *, tm, seq):
    H = BF16_SUBLANES
    i = pl.program_id(1)

    @pl.when(i == 0)
    def _():
        wgb_ref[...] = wg_ref[...].astype(BF16)
        wvb_ref[...] = wv_ref[...].astype(BF16)

    t0 = (i * tm) % seq
    zero = jnp.zeros((H, xm_ref.shape[1]), BF16)
    lhs_ref[0:H, :] = jnp.where(t0 != 0, xp_ref[...], zero)
    lhs_ref[H:H + tm, :] = xm_ref[...]
    lhs_ref[H + tm:2 * H + tm, :] = jnp.where(t0 + tm != seq, xn_ref[...], zero)
    lhs = lhs_ref[...]

    def conv(w_ref, cw_ref, cb_ref):
        h = jnp.dot(lhs, w_ref[...], preferred_element_type=F32)
        cw = cw_ref[...]
        n = h.shape[0]
        prev = pltpu.roll(h, 1, 0)[H:H + tm]
        nxt = pltpu.roll(h, n - 1, 0)[H:H + tm]
        return prev * cw[0:1] + h[H:H + tm] * cw[1:2] + nxt * cw[2:3] + cb_ref[...]

    gate = conv(wgb_ref, cwg_ref, cbg_ref)
    val = conv(wvb_ref, cwv_ref, cbv_ref)
    o_ref[...] = (gate * jax.nn.sigmoid(gate) * val).astype(o_ref.dtype)


def _ffn_up(xb, w_up, conv_w, conv_b, layer, seq, tm, tn):
    M, D = xb.shape
    H = BF16_SUBLANES
    nj = D_FF // tn
    hb = tm // H
    n_hblk = M // H
    conv_b = conv_b.reshape(conv_b.shape[0], 1, -1)
    return pl.pallas_call(
        functools.partial(_ffn_up_kernel, tm=tm, seq=seq),
        grid=(nj, M // tm),
        in_specs=[pl.BlockSpec((tm, D), lambda j, i: (i, 0)),
                  pl.BlockSpec((H, D), lambda j, i: (jnp.maximum(i * hb - 1, 0), 0)),
                  pl.BlockSpec((H, D), lambda j, i: (jnp.minimum((i + 1) * hb, n_hblk - 1), 0)),
                  pl.BlockSpec((None, D, tn), lambda j, i: (layer, 0, j)),
                  pl.BlockSpec((None, D, tn), lambda j, i: (layer, 0, j + nj)),
                  pl.BlockSpec((None, 3, tn), lambda j, i: (layer, 0, j)),
                  pl.BlockSpec((None, 3, tn), lambda j, i: (layer, 0, j + nj)),
                  pl.BlockSpec((None, 1, tn), lambda j, i: (layer, 0, j)),
                  pl.BlockSpec((None, 1, tn), lambda j, i: (layer, 0, j + nj))],
        out_specs=pl.BlockSpec((tm, tn), lambda j, i: (i, j)),
        out_shape=jax.ShapeDtypeStruct((M, D_FF), BF16),
        scratch_shapes=[pltpu.VMEM((tm + 2 * H, D), BF16),
                        pltpu.VMEM((D, tn), BF16), pltpu.VMEM((D, tn), BF16)],
        compiler_params=_cparams("parallel", "arbitrary"),
        name="ffn_up_conv",
    )(xb, xb, xb, w_up, w_up, conv_w, conv_w, conv_b, conv_b)


def _rope_tables(seq):
    t = np.arange(seq)
    r = (t // GRID_W).astype(np.float64)
    c = (t % GRID_W).astype(np.float64)
    half = C_HEAD_DIM // 2
    inv_freq = np.exp(-math.log(ROPE_THETA) * np.arange(0, half, 2, dtype=np.float64) / half)
    ang_r = r[:, None] * inv_freq
    ang_c = c[:, None] * inv_freq
    ang = np.concatenate([ang_r, ang_r, ang_c, ang_c], axis=-1)
    sign = np.tile(np.concatenate([-np.ones(half // 2), np.ones(half // 2)]), 2)
    return jnp.asarray(np.cos(ang), F32), jnp.asarray(np.sin(ang) * sign, F32)


def _qkv_kernel(x_ref, w_ref, gain_ref, cos_ref, sin_ref, o_ref, wb_ref, h_ref,
                *, rows_per_col, n_tiles, n_norm_heads):
    s = pl.program_id(0)
    hd = C_HEAD_DIM

    @pl.when(s == 0)
    def _():
        h_ref[1] = jnp.zeros(h_ref.shape[1:], F32)

    @pl.when((s % rows_per_col == 0) & (s < n_tiles))
    def _():
        wb_ref[...] = w_ref[...].astype(BF16)

    heads_per_tile = o_ref.shape[1] // hd
    first_head = (jnp.maximum(s - 1, 0) // rows_per_col) * heads_per_tile

    def step(fill, drain):
        cos = cos_ref[...]
        sin = sin_ref[...]
        lane = lax.broadcasted_iota(jnp.int32, (1, hd), 1)
        low = (lane % (hd // 2)) < (hd // 4)
        for h in range(heads_per_tile):
            cs = slice(h * hd, (h + 1) * hd)
            a = h_ref[drain, :, cs]
            y = a * lax.rsqrt(jnp.mean(a * a, axis=-1, keepdims=True) + RMS_EPS) * gain_ref[:, cs]
            partner = jnp.where(low, pltpu.roll(y, hd - hd // 4, 1), pltpu.roll(y, hd // 4, 1))
            plain = first_head + h >= n_norm_heads
            o_ref[:, cs] = jnp.where(plain, a, y * cos + partner * sin).astype(o_ref.dtype)
        h_ref[fill] = jnp.dot(x_ref[...], wb_ref[...], preferred_element_type=F32)

    @pl.when(s % 2 == 0)
    def _():
        step(0, 1)

    @pl.when(s % 2 == 1)
    def _():
        step(1, 0)


def _qkv_proj(xb, w, layer, gain, cos, sin, seq, tm, tn):
    M, D = xb.shape
    N = w.shape[2]
    n_pos_blk = seq // tm
    ni = M // tm
    n_tiles = (N // tn) * ni

    def fill(s):
        t = jnp.minimum(s, n_tiles - 1)
        return t // ni, t % ni

    def drain(s):
        t = jnp.maximum(s - 1, 0)
        return t // ni, t % ni

    return pl.pallas_call(
        functools.partial(_qkv_kernel, rows_per_col=ni, n_tiles=n_tiles,
                          n_norm_heads=C_Q_HEADS + C_KV_HEADS),
        grid=(n_tiles + 1,),
        in_specs=[pl.BlockSpec((tm, D), lambda s: (fill(s)[1], 0)),
                  pl.BlockSpec((None, D, tn), lambda s: (layer, 0, fill(s)[0])),
                  pl.BlockSpec((1, tn), lambda s: (0, drain(s)[0])),
                  pl.BlockSpec((tm, C_HEAD_DIM), lambda s: (drain(s)[1] % n_pos_blk, 0)),
                  pl.BlockSpec((tm, C_HEAD_DIM), lambda s: (drain(s)[1] % n_pos_blk, 0))],
        out_specs=pl.BlockSpec((tm, tn), lambda s: (drain(s)[1], drain(s)[0])),
        out_shape=jax.ShapeDtypeStruct((M, N), BF16),
        scratch_shapes=[pltpu.VMEM((D, tn), BF16), pltpu.VMEM((2, tm, tn), F32)],
        compiler_params=_cparams("arbitrary"),
        name="qkv_norm_rope",
    )(xb, w, gain, cos, sin)


def _attn_kernel(q_ref, k_ref, v_ref, o_ref, vt_ref, st_ref, m_ref, *, tq, n_sub, blocks_per_head, n_chunks):
    hd = C_HEAD_DIM
    s = pl.program_id(0)
    seq = k_ref.shape[0]
    ck = seq // n_chunks

    @pl.when(s == 0)
    def _():
        st_ref[1] = jnp.zeros(st_ref.shape[1:], F32)
        m_ref[1] = jnp.zeros(m_ref.shape[1:], F32)

    @pl.when(jnp.maximum(s - 1, 0) % blocks_per_head == 0)
    def _():
        vt_ref[0:hd, :] = v_ref[...].astype(F32).T.astype(BF16)
        vt_ref[hd:, :] = jnp.ones((vt_ref.shape[0] - hd, seq), BF16)

    sub = tq // n_sub

    def write_out(a, acc):
        o = acc[0:hd] * (1.0 / acc[hd:hd + 1])
        for g in range(C_GROUP):
            o_ref[a * sub:(a + 1) * sub, g * hd:(g + 1) * hd] = (
                o[:, g * sub:(g + 1) * sub].T.astype(o_ref.dtype))

    def step(fill, drain):
        qs = [jnp.concatenate([q_ref[a * sub:(a + 1) * sub, g * hd:(g + 1) * hd] for g in range(C_GROUP)],
                              axis=0) for a in range(n_sub)]
        pending = None
        for a in range(n_sub):
            m_drain = m_ref[drain, a]
            m_fill = None
            acc = None
            for c in range(n_chunks):
                rows = slice(c * ck, (c + 1) * ck)
                sc = lax.dot_general(k_ref[rows, :], qs[a], NT_DIMS, preferred_element_type=F32)
                st_ref[fill, a, rows, :] = sc
                mc = jnp.max(sc, axis=0, keepdims=True)
                m_fill = mc if c == 0 else jnp.maximum(m_fill, mc)
                if c == 1 and pending is not None:
                    write_out(*pending)
                    pending = None
                p = jnp.exp2(st_ref[drain, a, rows, :] - m_drain).astype(BF16)
                pv = jnp.dot(vt_ref[:, rows], p, preferred_element_type=F32)
                acc = pv if c == 0 else acc + pv
            m_ref[fill, a] = m_fill
            pending = (a, acc)
        write_out(*pending)

    @pl.when(s % 2 == 0)
    def _():
        step(0, 1)

    @pl.when(s % 2 == 1)
    def _():
        step(1, 0)


def _attention(qkv, batch, seq, tq, n_sub, n_chunks):
    hd = C_HEAD_DIM
    gw = C_GROUP * hd
    bph = seq // tq
    n_blocks = batch * C_KV_HEADS * bph

    def coords(blk):
        return blk // (C_KV_HEADS * bph), (blk // bph) % C_KV_HEADS, blk % bph

    def q_map(s):
        b, h, i = coords(jnp.minimum(s, n_blocks - 1))
        return b, i, h

    def k_map(s):
        b, h, _ = coords(jnp.minimum(s, n_blocks - 1))
        return b, 0, C_Q_HEADS + h

    def v_map(s):
        b, h, _ = coords(jnp.maximum(s - 1, 0))
        return b, 0, C_Q_HEADS + C_KV_HEADS + h

    def o_map(s):
        b, h, i = coords(jnp.maximum(s - 1, 0))
        return b, i, h

    return pl.pallas_call(
        functools.partial(_attn_kernel, tq=tq, n_sub=n_sub, blocks_per_head=bph, n_chunks=n_chunks),
        grid=(n_blocks + 1,),
        in_specs=[pl.BlockSpec((None, tq, gw), q_map),
                  pl.BlockSpec((None, seq, hd), k_map),
                  pl.BlockSpec((None, seq, hd), v_map)],
        out_specs=pl.BlockSpec((None, tq, gw), o_map),
        out_shape=jax.ShapeDtypeStruct((batch, seq, C_Q_HEADS * hd), BF16),
        scratch_shapes=[pltpu.VMEM((hd + BF16_SUBLANES, seq), BF16),
                        pltpu.VMEM((2, n_sub, seq, C_GROUP * tq // n_sub), F32),
                        pltpu.VMEM((2, n_sub, 1, C_GROUP * tq // n_sub), F32)],
        compiler_params=_cparams("arbitrary"),
        name="gqa_attention",
    )(qkv, qkv, qkv)


def _conv_ffn_ln(x, xb, w_up, conv_w, conv_b, w_down, g, b, layer, seq, out_dtypes):
    act = _ffn_up(xb, w_up, conv_w, conv_b, layer, seq, tm=1024, tn=512)
    return _proj_ln([act], w_down, layer, x, g[layer], b[layer], 256, out_dtypes, "ffn_down_ln")


def kernel(x, w_in_ab, hgrn_lb_table, hgrn_norm_g, gmlp_ln_g, gmlp_ln_b, gmlp_ws, gmlp_bias,
           w_out_ab, w_in_attn, q_norm_g, k_norm_g, w_out_attn, ffn_up, ffn_conv_w, ffn_conv_b,
           ffn_down, ln1_g, ln1_b, ln2_g, ln2_b):
    batch, seq, d = x.shape
    M = batch * seq
    x = x.reshape(M, d)
    xb = x.astype(BF16)

    h_gate = _matmul(xb, w_in_ab, 0, 2, lambda j: j + 1, 1024, 1024, F32, "inproj_gates")
    h_act = _matmul(xb, w_in_ab, 0, 5, lambda j: j + 2 * jnp.minimum(j, 1), 2048, 1024, BF16, "inproj_acts")
    o_a = _hgrn(h_act.reshape(batch, seq, -1), h_gate.reshape(batch, seq, -1), hgrn_lb_table,
                hgrn_norm_g[0].reshape(1, -1), 0, batch, seq)
    o_b = _gmlp(h_act, 3, gmlp_ln_g[0], gmlp_ln_b[0], gmlp_ws[0], gmlp_bias[0], rows=512)
    both = (F32, BF16)
    x, xb = _proj_ln([o_a.reshape(M, -1), o_b], w_out_ab, 0, x, ln1_g[0], ln1_b[0], 512, both, "outproj_ln")
    x, xb = _conv_ffn_ln(x, xb, ffn_up, ffn_conv_w, ffn_conv_b, ffn_down, ln2_g, ln2_b, 0, seq, both)

    scale = C_HEAD_DIM ** -0.5 * math.log2(math.e)
    gain = jnp.concatenate([jnp.tile(q_norm_g[0] * scale, C_Q_HEADS), jnp.tile(k_norm_g[0], C_KV_HEADS),
                            jnp.ones((C_KV_HEADS * C_HEAD_DIM,), F32)])
    cos, sin = _rope_tables(seq)
    qkv = _qkv_proj(xb, w_in_attn, 0, gain.reshape(1, -1), cos, sin, seq, tm=1024, tn=1024)
    att = _attention(qkv.reshape(batch, seq, -1), batch, seq, tq=256, n_sub=2, n_chunks=8)
    x, xb = _proj_ln([att.reshape(M, -1)], w_out_attn, 0, x, ln1_g[1], ln1_b[1], 512, both, "outproj_ln")
    (x,) = _conv_ffn_ln(x, xb, ffn_up, ffn_conv_w, ffn_conv_b, ffn_down, ln2_g, ln2_b, 1, seq, (F32,))
    return x.reshape(batch, seq, d)
```

```python
import functools
import math

import jax
import jax.numpy as jnp
import numpy as np
from jax import lax
from jax.experimental import pallas as pl
from jax.experimental.pallas import tpu as pltpu

F32 = jnp.float32
BF16 = jnp.bfloat16

D_MODEL = 2048
GRID_W = 64
A_HEAD_DIM = 128
A_WIDTH = D_MODEL // 2
A_HEADS = A_WIDTH // A_HEAD_DIM
HGRN_CHUNK = 64
B_WIDTH = D_MODEL // 2
B_GROUP_DIM = 128
B_GROUPS = B_WIDTH // B_GROUP_DIM
B_CHUNK = 128
C_HEAD_DIM = 128
C_Q_HEADS = D_MODEL // C_HEAD_DIM
C_KV_HEADS = C_Q_HEADS // 4
C_GROUP = C_Q_HEADS // C_KV_HEADS
ROPE_THETA = 10000.0
D_FF = 5632
DEPTH = 2
ALPHA = (2.0 * DEPTH) ** 0.25
LN_EPS = 1e-5
RMS_EPS = 1e-6

V7X_VMEM_LIMIT_BYTES = 56 * 1024 * 1024
BF16_SUBLANES = 16

NT_DIMS = (((1,), (1,)), ((), ()))


def _cparams(*sem):
    return pltpu.CompilerParams(dimension_semantics=sem, vmem_limit_bytes=V7X_VMEM_LIMIT_BYTES)


def _layer_norm(y, g, b):
    mu = jnp.mean(y, axis=-1, keepdims=True)
    yc = y - mu
    var = jnp.mean(yc * yc, axis=-1, keepdims=True)
    return yc * lax.rsqrt(var + LN_EPS) * g + b


def _mm_kernel(x_ref, w_ref, o_ref, wb_ref):
    @pl.when(pl.program_id(1) == 0)
    def _():
        wb_ref[...] = w_ref[...].astype(BF16)

    o_ref[...] = jnp.dot(x_ref[...], wb_ref[...], preferred_element_type=F32).astype(o_ref.dtype)


def _matmul(x, w, layer, n_col_tiles, col_tile_of, tm, tn, out_dtype, name):
    M, K = x.shape
    return pl.pallas_call(
        _mm_kernel,
        grid=(n_col_tiles, M // tm),
        in_specs=[pl.BlockSpec((tm, K), lambda j, i: (i, 0)),
                  pl.BlockSpec((None, K, tn), lambda j, i: (layer, 0, col_tile_of(j)))],
        out_specs=pl.BlockSpec((tm, tn), lambda j, i: (i, j)),
        out_shape=jax.ShapeDtypeStruct((M, n_col_tiles * tn), out_dtype),
        scratch_shapes=[pltpu.VMEM((K, tn), BF16)],
        compiler_params=_cparams("parallel", "arbitrary"),
        name=name,
    )(x, w)


HGRN_GROUP = 4
HGRN_ROWS = HGRN_GROUP * HGRN_CHUNK


def _hgrn_kernel(tab_ref, q_ref, ff_ref, fb_ref, i_ref, g_ref, ng_ref, o_ref,
                 ofw_ref, obw_ref, qef_ref, kef_ref, k2f_ref, qbf_ref, decf_ref,
                 qeb_ref, keb_ref, k2b_ref, qbb_ref, decb_ref, *, layer, seq):
    C, G, R, hd = HGRN_CHUNK, HGRN_GROUP, HGRN_ROWS, A_HEAD_DIM
    n_groups = seq // R
    tab = tab_ref[...]
    e = jnp.exp(tab - jnp.max(tab, axis=0, keepdims=True))
    sm = e / jnp.sum(e, axis=0, keepdims=True)
    lb = jnp.sum(sm[:layer + 1], axis=0, keepdims=True)

    row = lax.broadcasted_iota(jnp.int32, (R, R), 0)
    col = lax.broadcasted_iota(jnp.int32, (R, R), 1)
    same = (row // C) == (col // C)
    lower = same & (col <= row)
    upper = same & (col >= row)
    X = BF16_SUBLANES
    jrow = lax.broadcasted_iota(jnp.int32, (X, R), 0)
    jcol = lax.broadcasted_iota(jnp.int32, (X, R), 1)
    in_chunk = (jcol // C) == jrow
    last_rows = in_chunk.astype(BF16)
    mid_f = (in_chunk & (jcol % C <= C // 2 - 1)).astype(BF16)
    mid_b = (in_chunk & (jcol % C >= C // 2)).astype(BF16)
    cum_f = jnp.concatenate([lower.astype(BF16), mid_f, last_rows], axis=0)
    cum_b = jnp.concatenate([upper.astype(BF16), mid_b, last_rows], axis=0)

    for ref in (k2f_ref, qbf_ref, k2b_ref, qbb_ref):
        ref[...] = jnp.zeros(ref.shape, BF16)

    def per_chunk(extra, j0):
        return jnp.concatenate(
            [jnp.broadcast_to(extra[j0 + c:j0 + c + 1], (C, hd)) for c in range(G)], axis=0)

    def prepare(start, d, slot):
        f_ref, cum = d["f"], d["cum"]
        rows = pl.ds(start, R)
        f = lb + (1.0 - lb) * jax.nn.sigmoid(f_ref[rows, :])
        k = 1.0 - f
        lf = jnp.log(f)
        hi = lf.astype(BF16)
        r1 = lf - hi.astype(F32)
        mid = r1.astype(BF16)
        lo = (r1 - mid.astype(F32)).astype(BF16)
        yield
        ball = jnp.dot(cum, jnp.concatenate([hi, mid, lo], axis=1), preferred_element_type=F32)
        ball = ball[:, 0:hd] + ball[:, hd:2 * hd] + ball[:, 2 * hd:3 * hd]
        yield
        b = ball[0:R]
        b_mid = per_chunk(ball, R)
        b_last = per_chunk(ball, R + X)
        q = q_ref[rows, :].astype(F32)
        d["qe"][slot] = (q * jnp.exp(b - b_mid)).astype(BF16)
        d["ke"][slot] = (k * jnp.exp(b_mid - b)).astype(BF16)
        yield
        qb = (q * jnp.exp(b)).astype(BF16)
        k2 = (k * jnp.exp(b_last - b)).astype(BF16)
        for c in range(G):
            d["k2"][slot, c * C:(c + 1) * C, c * hd:(c + 1) * hd] = k2[c * C:(c + 1) * C]
            d["qb"][slot, c * C:(c + 1) * C, c * hd:(c + 1) * hd] = qb[c * C:(c + 1) * C]
        d["dec"][slot] = jnp.exp(ball[R + X:R + X + 8])
        yield

    def apply(start, d, slot, st):
        rows = pl.ds(start, R)
        v = i_ref[rows, :]
        sc = lax.dot_general(d["qe"][slot], d["ke"][slot], NT_DIMS, preferred_element_type=F32)
        sc = jnp.where(d["mask"], sc, 0.0)
        yield
        o = jnp.dot(sc.astype(BF16), v, preferred_element_type=F32)
        kv_t = jnp.dot(v.astype(F32).T.astype(BF16), d["k2"][slot], preferred_element_type=F32)
        yield
        dec = d["dec"][slot]
        entering = [None] * G
        for c in (reversed(range(G)) if d["reverse"] else range(G)):
            entering[c] = st.astype(BF16)
            st = st * dec[c:c + 1] + kv_t[:, c * hd:(c + 1) * hd]
        yield
        o = o + lax.dot_general(d["qb"][slot], jnp.concatenate(entering, axis=1), NT_DIMS,
                                preferred_element_type=F32)
        d["out"][rows, :] = o
        return st

    def interleave(*gens):
        results = [None] * len(gens)
        active = list(enumerate(gens))
        while active:
            still = []
            for idx, g in active:
                try:
                    next(g)
                    still.append((idx, g))
                except StopIteration as done:
                    results[idx] = done.value
            active = still
        return results

    fw = dict(f=ff_ref, cum=cum_f, mask=lower, reverse=False, out=ofw_ref,
              qe=qef_ref, ke=kef_ref, k2=k2f_ref, qb=qbf_ref, dec=decf_ref)
    bw = dict(f=fb_ref, cum=cum_b, mask=upper, reverse=True, out=obw_ref,
              qe=qeb_ref, ke=keb_ref, k2=k2b_ref, qb=qbb_ref, dec=decb_ref)

    def starts(g):
        g = jnp.minimum(g, n_groups - 1)
        return pl.multiple_of(g * R, R), pl.multiple_of((n_groups - 1 - g) * R, R)

    s_f, s_b = starts(0)
    interleave(prepare(s_f, fw, 0), prepare(s_b, bw, 0))

    ng = ng_ref[...]

    def finish(start):
        rows = pl.ds(start, R)
        o = ofw_ref[rows, :] + obw_ref[rows, :]
        ms = jnp.mean(o * o, axis=-1, keepdims=True)
        yield
        y = o * lax.rsqrt(ms + RMS_EPS) * ng
        g = g_ref[rows, :].astype(F32)
        yield
        o_ref[rows, :] = (y * (g * jax.nn.sigmoid(g))).astype(o_ref.dtype)

    def half(g, carry, slot, done):
        st_f, st_b = carry
        a_f, a_b = starts(g)
        p_f, p_b = starts(g + 1)
        gens = [prepare(p_f, fw, 1 - slot), prepare(p_b, bw, 1 - slot),
                apply(a_f, fw, slot, st_f), apply(a_b, bw, slot, st_b)]
        if done is not None:
            gens += [finish(r) for r in starts(done)]
        res = interleave(*gens)
        return res[2], res[3]

    assert n_groups % 4 == 0
    mid = n_groups // 2

    def early(n, carry):
        for slot in (0, 1):
            carry = half(2 * n + slot, carry, slot, None)
        return carry

    def late(n, carry):
        for slot in (0, 1):
            carry = half(2 * n + slot, carry, slot, 2 * n + slot - 1)
        return carry

    zero = jnp.zeros((hd, hd), F32)
    carry = lax.fori_loop(0, mid // 2, early, (zero, zero))
    carry = half(mid, carry, 0, None)
    carry = half(mid + 1, carry, 1, mid)
    lax.fori_loop(mid // 2 + 1, n_groups // 2, late, carry)
    interleave(*[finish(r) for r in starts(n_groups - 1)])


def _hgrn(h_act, h_gate, lb_table, norm_g, layer, batch, seq):
    hd = A_HEAD_DIM

    def col(part):
        return pl.BlockSpec((None, seq, hd), lambda b, h: (b, 0, part * A_HEADS + h))

    n_tab = lb_table.shape[0]
    return pl.pallas_call(
        functools.partial(_hgrn_kernel, layer=layer, seq=seq),
        grid=(batch, A_HEADS),
        in_specs=[pl.BlockSpec((n_tab, hd), lambda b, h: (0, h)),
                  col(0), col(0), col(1), col(1), col(2),
                  pl.BlockSpec((1, hd), lambda b, h: (0, h))],
        out_specs=pl.BlockSpec((None, seq, hd), lambda b, h: (b, 0, h)),
        out_shape=jax.ShapeDtypeStruct((batch, seq, A_WIDTH), BF16),
        scratch_shapes=[pltpu.VMEM((seq, hd), F32), pltpu.VMEM((seq, hd), F32)]
        + 2 * [pltpu.VMEM((2, HGRN_ROWS, hd), BF16), pltpu.VMEM((2, HGRN_ROWS, hd), BF16),
               pltpu.VMEM((2, HGRN_ROWS, HGRN_GROUP * hd), BF16),
               pltpu.VMEM((2, HGRN_ROWS, HGRN_GROUP * hd), BF16),
               pltpu.VMEM((2, 8, hd), F32)],
        compiler_params=_cparams("parallel", "parallel"),
        name="hgrn2",
    )(lb_table, h_act, h_gate, h_gate, h_act, h_act, norm_g)


def _gmlp_kernel(u_ref, v_ref, lg_ref, lbias_ref, ws_ref, bias_ref, o_ref, *, rows):
    vln = _layer_norm(v_ref[...].astype(F32), lg_ref[...], lbias_ref[...]).astype(BF16)
    for c in range(rows // B_CHUNK):
        r = slice(c * B_CHUNK, (c + 1) * B_CHUNK)
        for g in range(B_GROUPS):
            cs = slice(g * B_GROUP_DIM, (g + 1) * B_GROUP_DIM)
            s = jnp.dot(ws_ref[g], vln[r, cs], preferred_element_type=F32) + bias_ref[g]
            o_ref[r, cs] = (u_ref[r, cs] * s).astype(o_ref.dtype)


def _gmlp(h_act, u_blk, ln_g, ln_b, ws, bias, rows):
    M = h_act.shape[0]
    bias_b = jnp.broadcast_to(bias[:, :, None], (B_GROUPS, B_CHUNK, B_GROUP_DIM))
    return pl.pallas_call(
        functools.partial(_gmlp_kernel, rows=rows),
        grid=(M // rows,),
        in_specs=[pl.BlockSpec((rows, B_WIDTH), lambda i: (i, u_blk)),
                  pl.BlockSpec((rows, B_WIDTH), lambda i: (i, u_blk + 1)),
                  pl.BlockSpec((1, B_WIDTH), lambda i: (0, 0)),
                  pl.BlockSpec((1, B_WIDTH), lambda i: (0, 0)),
                  pl.BlockSpec((B_GROUPS, B_CHUNK, B_CHUNK), lambda i: (0, 0, 0)),
                  pl.BlockSpec((B_GROUPS, B_CHUNK, B_GROUP_DIM), lambda i: (0, 0, 0))],
        out_specs=pl.BlockSpec((rows, B_WIDTH), lambda i: (i, 0)),
        out_shape=jax.ShapeDtypeStruct((M, B_WIDTH), BF16),
        compiler_params=_cparams("parallel"),
        name="gmlp",
    )(h_act, h_act, ln_g.reshape(1, -1), ln_b.reshape(1, -1), ws.astype(BF16), bias_b)


PROJ_STAGE_ROWS = 512


def _proj_ln_kernel(*refs, n_in, n_out, layer):
    ins = refs[:n_in]
    w_hbm, x_ref, g_ref, b_ref = refs[n_in:n_in + 4]
    out_refs = refs[n_in + 4:n_in + 4 + n_out]
    wb_ref, stage_ref, sem = refs[n_in + 4 + n_out:]
    K = wb_ref.shape[0]
    n_stage = K // PROJ_STAGE_ROWS

    @pl.when(pl.program_id(0) == 0)
    def _():
        def chunk_copy(c):
            rows = pl.ds(c * PROJ_STAGE_ROWS, PROJ_STAGE_ROWS)
            return pltpu.make_async_copy(w_hbm.at[layer, rows, :], stage_ref.at[c % 2], sem.at[c % 2])

        chunk_copy(0).start()
        for c in range(n_stage):
            if c + 1 < n_stage:
                chunk_copy(c + 1).start()
            chunk_copy(c).wait()
            wb_ref[c * PROJ_STAGE_ROWS:(c + 1) * PROJ_STAGE_ROWS, :] = stage_ref[c % 2].astype(BF16)

    acc = None
    off = 0
    for r in ins:
        kk = r.shape[1]
        d = jnp.dot(r[...], wb_ref[off:off + kk, :], preferred_element_type=F32)
        acc = d if acc is None else acc + d
        off += kk
    out = _layer_norm(ALPHA * x_ref[...] + acc, g_ref[...], b_ref[...])
    for o_ref in out_refs:
        o_ref[...] = out.astype(o_ref.dtype)


def _proj_ln(parts, w, layer, x, g, b, tm, out_dtypes, name):
    M, D = x.shape
    K = w.shape[1]
    assert K % PROJ_STAGE_ROWS == 0
    in_specs = [pl.BlockSpec((tm, p.shape[1]), lambda i: (i, 0)) for p in parts]
    in_specs += [pl.BlockSpec(memory_space=pl.ANY),
                 pl.BlockSpec((tm, D), lambda i: (i, 0)),
                 pl.BlockSpec((1, D), lambda i: (0, 0)),
                 pl.BlockSpec((1, D), lambda i: (0, 0))]
    return pl.pallas_call(
        functools.partial(_proj_ln_kernel, n_in=len(parts), n_out=len(out_dtypes), layer=layer),
        grid=(M // tm,),
        in_specs=in_specs,
        out_specs=[pl.BlockSpec((tm, D), lambda i: (i, 0)) for _ in out_dtypes],
        out_shape=[jax.ShapeDtypeStruct((M, D), dt) for dt in out_dtypes],
        scratch_shapes=[pltpu.VMEM((K, D), BF16),
                        pltpu.VMEM((2, PROJ_STAGE_ROWS, D), F32),
                        pltpu.SemaphoreType.DMA((2,))],
        compiler_params=_cparams("arbitrary"),
        name=name,
    )(*parts, w, x, g.reshape(1, -1), b.reshape(1, -1))


def _ffn_up_kernel(xm_ref, xp_ref, xn_ref, wg_ref, wv_ref, cwg_ref, cwv_ref, cbg_ref, cbv_ref,
                   o_ref, lhs_ref, wgb_ref, wvb_ref, *, tm, seq):
    H = BF16_SUBLANES
    i = pl.program_id(1)

    @pl.when(i == 0)
    def _():
        wgb_ref[...] = wg_ref[...].astype(BF16)
        wvb_ref[...] = wv_ref[...].astype(BF16)

    t0 = (i * tm) % seq
    zero = jnp.zeros((H, xm_ref.shape[1]), BF16)
    lhs_ref[0:H, :] = jnp.where(t0 != 0, xp_ref[...], zero)
    lhs_ref[H:H + tm, :] = xm_ref[...]
    lhs_ref[H + tm:2 * H + tm, :] = jnp.where(t0 + tm != seq, xn_ref[...], zero)
    lhs = lhs_ref[...]

    def conv(w_ref, cw_ref, cb_ref):
        h = jnp.dot(lhs, w_ref[...], preferred_element_type=F32)
        cw = cw_ref[...]
        n = h.shape[0]
        prev = pltpu.roll(h, 1, 0)[H:H + tm]
        nxt = pltpu.roll(h, n - 1, 0)[H:H + tm]
        return prev * cw[0:1] + h[H:H + tm] * cw[1:2] + nxt * cw[2:3] + cb_ref[...]

    gate = conv(wgb_ref, cwg_ref, cbg_ref)
    val = conv(wvb_ref, cwv_ref, cbv_ref)
    o_ref[...] = (gate * jax.nn.sigmoid(gate) * val).astype(o_ref.dtype)


def _ffn_up(xb, w_up, conv_w, conv_b, layer, seq, tm, tn):
    M, D = xb.shape
    H = BF16_SUBLANES
    nj = D_FF // tn
    hb = tm // H
    n_hblk = M // H
    conv_b = conv_b.reshape(conv_b.shape[0], 1, -1)
    return pl.pallas_call(
        functools.partial(_ffn_up_kernel, tm=tm, seq=seq),
        grid=(nj, M // tm),
        in_specs=[pl.BlockSpec((tm, D), lambda j, i: (i, 0)),
                  pl.BlockSpec((H, D), lambda j, i: (jnp.maximum(i * hb - 1, 0), 0)),
                  pl.BlockSpec((H, D), lambda j, i: (jnp.minimum((i + 1) * hb, n_hblk - 1), 0)),
                  pl.BlockSpec((None, D, tn), lambda j, i: (layer, 0, j)),
                  pl.BlockSpec((None, D, tn), lambda j, i: (layer, 0, j + nj)),
                  pl.BlockSpec((None, 3, tn), lambda j, i: (layer, 0, j)),
                  pl.BlockSpec((None, 3, tn), lambda j, i: (layer, 0, j + nj)),
                  pl.BlockSpec((None, 1, tn), lambda j, i: (layer, 0, j)),
                  pl.BlockSpec((None, 1, tn), lambda j, i: (layer, 0, j + nj))],
        out_specs=pl.BlockSpec((tm, tn), lambda j, i: (i, j)),
        out_shape=jax.ShapeDtypeStruct((M, D_FF), BF16),
        scratch_shapes=[pltpu.VMEM((tm + 2 * H, D), BF16),
                        pltpu.VMEM((D, tn), BF16), pltpu.VMEM((D, tn), BF16)],
        compiler_params=_cparams("parallel", "arbitrary"),
        name="ffn_up_conv",
    )(xb, xb, xb, w_up, w_up, conv_w, conv_w, conv_b, conv_b)


def _rope_tables(seq):
    t = np.arange(seq)
    r = (t // GRID_W).astype(np.float64)
    c = (t % GRID_W).astype(np.float64)
    half = C_HEAD_DIM // 2
    inv_freq = np.exp(-math.log(ROPE_THETA) * np.arange(0, half, 2, dtype=np.float64) / half)
    ang_r = r[:, None] * inv_freq
    ang_c = c[:, None] * inv_freq
    ang = np.concatenate([ang_r, ang_r, ang_c, ang_c], axis=-1)
    sign = np.tile(np.concatenate([-np.ones(half // 2), np.ones(half // 2)]), 2)
    return jnp.asarray(np.cos(ang), F32), jnp.asarray(np.sin(ang) * sign, F32)


def _qkv_kernel(x_ref, w_ref, gain_ref, cos_ref, sin_ref, o_ref, wb_ref, h_ref,
                *, rows_per_col, n_tiles, n_norm_heads):
    s = pl.program_id(0)
    hd = C_HEAD_DIM

    @pl.when(s == 0)
    def _():
        h_ref[1] = jnp.zeros(h_ref.shape[1:], F32)

    @pl.when((s % rows_per_col == 0) & (s < n_tiles))
    def _():
        wb_ref[...] = w_ref[...].astype(BF16)

    heads_per_tile = o_ref.shape[1] // hd
    first_head = (jnp.maximum(s - 1, 0) // rows_per_col) * heads_per_tile

    def step(fill, drain):
        cos = cos_ref[...]
        sin = sin_ref[...]
        lane = lax.broadcasted_iota(jnp.int32, (1, hd), 1)
        low = (lane % (hd // 2)) < (hd // 4)
        for h in range(heads_per_tile):
            cs = slice(h * hd, (h + 1) * hd)
            a = h_ref[drain, :, cs]
            y = a * lax.rsqrt(jnp.mean(a * a, axis=-1, keepdims=True) + RMS_EPS) * gain_ref[:, cs]
            partner = jnp.where(low, pltpu.roll(y, hd - hd // 4, 1), pltpu.roll(y, hd // 4, 1))
            plain = first_head + h >= n_norm_heads
            o_ref[:, cs] = jnp.where(plain, a, y * cos + partner * sin).astype(o_ref.dtype)
        h_ref[fill] = jnp.dot(x_ref[...], wb_ref[...], preferred_element_type=F32)

    @pl.when(s % 2 == 0)
    def _():
        step(0, 1)

    @pl.when(s % 2 == 1)
    def _():
        step(1, 0)


def _qkv_proj(xb, w, layer, gain, cos, sin, seq, tm, tn):
    M, D = xb.shape
    N = w.shape[2]
    n_pos_blk = seq // tm
    ni = M // tm
    n_tiles = (N // tn) * ni

    def fill(s):
        t = jnp.minimum(s, n_tiles - 1)
        return t // ni, t % ni

    def drain(s):
        t = jnp.maximum(s - 1, 0)
        return t // ni, t % ni

    return pl.pallas_call(
        functools.partial(_qkv_kernel, rows_per_col=ni, n_tiles=n_tiles,
                          n_norm_heads=C_Q_HEADS + C_KV_HEADS),
        grid=(n_tiles + 1,),
        in_specs=[pl.BlockSpec((tm, D), lambda s: (fill(s)[1], 0)),
                  pl.BlockSpec((None, D, tn), lambda s: (layer, 0, fill(s)[0])),
                  pl.BlockSpec((1, tn), lambda s: (0, drain(s)[0])),
                  pl.BlockSpec((tm, C_HEAD_DIM), lambda s: (drain(s)[1] % n_pos_blk, 0)),
                  pl.BlockSpec((tm, C_HEAD_DIM), lambda s: (drain(s)[1] % n_pos_blk, 0))],
        out_specs=pl.BlockSpec((tm, tn), lambda s: (drain(s)[1], drain(s)[0])),
        out_shape=jax.ShapeDtypeStruct((M, N), BF16),
        scratch_shapes=[pltpu.VMEM((D, tn), BF16), pltpu.VMEM((2, tm, tn), F32)],
        compiler_params=_cparams("arbitrary"),
        name="qkv_norm_rope",
    )(xb, w, gain, cos, sin)


def _attn_kernel(q_ref, k_ref, v_ref, o_ref, vt_ref, st_ref, m_ref, *, tq, n_sub, blocks_per_head, n_chunks):
    hd = C_HEAD_DIM
    s = pl.program_id(0)
    seq = k_ref.shape[0]
    ck = seq // n_chunks

    @pl.when(s == 0)
    def _():
        st_ref[1] = jnp.zeros(st_ref.shape[1:], F32)
        m_ref[1] = jnp.zeros(m_ref.shape[1:], F32)

    @pl.when(jnp.maximum(s - 1, 0) % blocks_per_head == 0)
    def _():
        vt_ref[0:hd, :] = v_ref[...].astype(F32).T.astype(BF16)
        vt_ref[hd:, :] = jnp.ones((vt_ref.shape[0] - hd, seq), BF16)

    sub = tq // n_sub

    def write_out(a, acc):
        o = acc[0:hd] * (1.0 / acc[hd:hd + 1])
        for g in range(C_GROUP):
            o_ref[a * sub:(a + 1) * sub, g * hd:(g + 1) * hd] = (
                o[:, g * sub:(g + 1) * sub].T.astype(o_ref.dtype))

    def step(fill, drain):
        qs = [jnp.concatenate([q_ref[a * sub:(a + 1) * sub, g * hd:(g + 1) * hd] for g in range(C_GROUP)],
                              axis=0) for a in range(n_sub)]
        pending = None
        for a in range(n_sub):
            m_drain = m_ref[drain, a]
            m_fill = None
            acc = None
            for c in range(n_chunks):
                rows = slice(c * ck, (c + 1) * ck)
                sc = lax.dot_general(k_ref[rows, :], qs[a], NT_DIMS, preferred_element_type=F32)
                st_ref[fill, a, rows, :] = sc
                mc = jnp.max(sc, axis=0, keepdims=True)
                m_fill = mc if c == 0 else jnp.maximum(m_fill, mc)
                if c == 1 and pending is not None:
                    write_out(*pending)
                    pending = None
                p = jnp.exp2(st_ref[drain, a, rows, :] - m_drain).astype(BF16)
                pv = jnp.dot(vt_ref[:, rows], p, preferred_element_type=F32)
                acc = pv if c == 0 else acc + pv
            m_ref[fill, a] = m_fill
            pending = (a, acc)
        write_out(*pending)

    @pl.when(s % 2 == 0)
    def _():
        step(0, 1)

    @pl.when(s % 2 == 1)
    def _():
        step(1, 0)


def _attention(qkv, batch, seq, tq, n_sub, n_chunks):
    hd = C_HEAD_DIM
    gw = C_GROUP * hd
    bph = seq // tq
    n_blocks = batch * C_KV_HEADS * bph

    def coords(blk):
        return blk // (C_KV_HEADS * bph), (blk // bph) % C_KV_HEADS, blk % bph

    def q_map(s):
        b, h, i = coords(jnp.minimum(s, n_blocks - 1))
        return b, i, h

    def k_map(s):
        b, h, _ = coords(jnp.minimum(s, n_blocks - 1))
        return b, 0, C_Q_HEADS + h

    def v_map(s):
        b, h, _ = coords(jnp.maximum(s - 1, 0))
        return b, 0, C_Q_HEADS + C_KV_HEADS + h

    def o_map(s):
        b, h, i = coords(jnp.maximum(s - 1, 0))
        return b, i, h

    return pl.pallas_call(
        functools.partial(_attn_kernel, tq=tq, n_sub=n_sub, blocks_per_head=bph, n_chunks=n_chunks),
        grid=(n_blocks + 1,),
        in_specs=[pl.BlockSpec((None, tq, gw), q_map),
                  pl.BlockSpec((None, seq, hd), k_map),
                  pl.BlockSpec((None, seq, hd), v_map)],
        out_specs=pl.BlockSpec((None, tq, gw), o_map),
        out_shape=jax.ShapeDtypeStruct((batch, seq, C_Q_HEADS * hd), BF16),
        scratch_shapes=[pltpu.VMEM((hd + BF16_SUBLANES, seq), BF16),
                        pltpu.VMEM((2, n_sub, seq, C_GROUP * tq // n_sub), F32),
                        pltpu.VMEM((2, n_sub, 1, C_GROUP * tq // n_sub), F32)],
        compiler_params=_cparams("arbitrary"),
        name="gqa_attention",
    )(qkv, qkv, qkv)


INPROJ_GATE_ROWS, INPROJ_ACT_ROWS, INPROJ_COLS = 1024, 2048, 1024
GMLP_ROWS = 512
OUTPROJ_ROWS, DOWNPROJ_ROWS = 512, 256
FFN_UP_ROWS, FFN_UP_COLS = 1024, 512
QKV_ROWS, QKV_COLS = 1024, 1024
ATTN_QUERIES, ATTN_SUB_BLOCKS, ATTN_KEY_CHUNKS = 256, 2, 8


def _conv_ffn_ln(x, xb, w_up, conv_w, conv_b, w_down, g, b, layer, seq, out_dtypes):
    act = _ffn_up(xb, w_up, conv_w, conv_b, layer, seq, tm=FFN_UP_ROWS, tn=FFN_UP_COLS)
    return _proj_ln([act], w_down, layer, x, g[layer], b[layer], DOWNPROJ_ROWS, out_dtypes, "ffn_down_ln")


def kernel(x, w_in_ab, hgrn_lb_table, hgrn_norm_g, gmlp_ln_g, gmlp_ln_b, gmlp_ws, gmlp_bias,
           w_out_ab, w_in_attn, q_norm_g, k_norm_g, w_out_attn, ffn_up, ffn_conv_w, ffn_conv_b,
           ffn_down, ln1_g, ln1_b, ln2_g, ln2_b):
    batch, seq, d = x.shape
    M = batch * seq
    x = x.reshape(M, d)
    xb = x.astype(BF16)

    assert INPROJ_COLS == A_WIDTH == B_WIDTH
    h_gate = _matmul(xb, w_in_ab, 0, 2, lambda j: j + 1, INPROJ_GATE_ROWS, INPROJ_COLS, F32, "inproj_gates")
    h_act = _matmul(xb, w_in_ab, 0, 5, lambda j: j + 2 * jnp.minimum(j, 1), INPROJ_ACT_ROWS, INPROJ_COLS,
                    BF16, "inproj_acts")
    o_a = _hgrn(h_act.reshape(batch, seq, -1), h_gate.reshape(batch, seq, -1), hgrn_lb_table,
                hgrn_norm_g[0].reshape(1, -1), 0, batch, seq)
    o_b = _gmlp(h_act, 3, gmlp_ln_g[0], gmlp_ln_b[0], gmlp_ws[0], gmlp_bias[0], rows=GMLP_ROWS)
    both = (F32, BF16)
    x, xb = _proj_ln([o_a.reshape(M, -1), o_b], w_out_ab, 0, x, ln1_g[0], ln1_b[0], OUTPROJ_ROWS, both,
                     "outproj_ln")
    x, xb = _conv_ffn_ln(x, xb, ffn_up, ffn_conv_w, ffn_conv_b, ffn_down, ln2_g, ln2_b, 0, seq, both)

    scale = C_HEAD_DIM ** -0.5 * math.log2(math.e)
    gain = jnp.concatenate([jnp.tile(q_norm_g[0] * scale, C_Q_HEADS), jnp.tile(k_norm_g[0], C_KV_HEADS),
                            jnp.ones((C_KV_HEADS * C_HEAD_DIM,), F32)])
    cos, sin = _rope_tables(seq)
    qkv = _qkv_proj(xb, w_in_attn, 0, gain.reshape(1, -1), cos, sin, seq, tm=QKV_ROWS, tn=QKV_COLS)
    att = _attention(qkv.reshape(batch, seq, -1), batch, seq, tq=ATTN_QUERIES, n_sub=ATTN_SUB_BLOCKS,
                     n_chunks=ATTN_KEY_CHUNKS)
    x, xb = _proj_ln([att.reshape(M, -1)], w_out_attn, 0, x, ln1_g[1], ln1_b[1], OUTPROJ_ROWS, both,
                     "outproj_ln")
    (x,) = _conv_ffn_ln(x, xb, ffn_up, ffn_conv_w, ffn_conv_b, ffn_down, ln2_g, ln2_b, 1, seq, (F32,))
    return x.reshape(batch, seq, d)
```

```python
import functools
import math

import jax
import jax.numpy as jnp
import numpy as np
from jax import lax
from jax.experimental import pallas as pl
from jax.experimental.pallas import tpu as pltpu

F32 = jnp.float32
BF16 = jnp.bfloat16

D_MODEL = 2048
GRID_W = 64
A_HEAD_DIM = 128
A_WIDTH = D_MODEL // 2
A_HEADS = A_WIDTH // A_HEAD_DIM
HGRN_CHUNK = 64
B_WIDTH = D_MODEL // 2
B_GROUP_DIM = 128
B_GROUPS = B_WIDTH // B_GROUP_DIM
B_CHUNK = 128
C_HEAD_DIM = 128
C_Q_HEADS = D_MODEL // C_HEAD_DIM
C_KV_HEADS = C_Q_HEADS // 4
C_GROUP = C_Q_HEADS // C_KV_HEADS
ROPE_THETA = 10000.0
D_FF = 5632
DEPTH = 2
ALPHA = (2.0 * DEPTH) ** 0.25
LN_EPS = 1e-5
RMS_EPS = 1e-6

V7X_VMEM_LIMIT_BYTES = 56 * 1024 * 1024
BF16_SUBLANES = 16

NT_DIMS = (((1,), (1,)), ((), ()))


def _cparams(*sem):
    return pltpu.CompilerParams(dimension_semantics=sem, vmem_limit_bytes=V7X_VMEM_LIMIT_BYTES)


def _layer_norm(y, g, b):
    mu = jnp.mean(y, axis=-1, keepdims=True)
    yc = y - mu
    var = jnp.mean(yc * yc, axis=-1, keepdims=True)
    return yc * lax.rsqrt(var + LN_EPS) * g + b


def _mm_kernel(x_ref, w_ref, o_ref, wb_ref):
    @pl.when(pl.program_id(1) == 0)
    def _():
        wb_ref[...] = w_ref[...].astype(BF16)

    o_ref[...] = jnp.dot(x_ref[...], wb_ref[...], preferred_element_type=F32).astype(o_ref.dtype)


def _matmul(x, w, layer, n_col_tiles, col_tile_of, tm, tn, out_dtype, name):
    M, K = x.shape
    return pl.pallas_call(
        _mm_kernel,
        grid=(n_col_tiles, M // tm),
        in_specs=[pl.BlockSpec((tm, K), lambda j, i: (i, 0)),
                  pl.BlockSpec((None, K, tn), lambda j, i: (layer, 0, col_tile_of(j)))],
        out_specs=pl.BlockSpec((tm, tn), lambda j, i: (i, j)),
        out_shape=jax.ShapeDtypeStruct((M, n_col_tiles * tn), out_dtype),
        scratch_shapes=[pltpu.VMEM((K, tn), BF16)],
        compiler_params=_cparams("parallel", "arbitrary"),
        name=name,
    )(x, w)


HGRN_GROUP = 4
HGRN_ROWS = HGRN_GROUP * HGRN_CHUNK


def _hgrn_kernel(tab_ref, q_ref, ff_ref, fb_ref, i_ref, g_ref, ng_ref, o_ref,
                 acc_ref, qe_ref, ke_ref, k2_ref, qb_ref, dec_ref, *, layer, seq, n_heads):
    C, G, R, hd = HGRN_CHUNK, HGRN_GROUP, HGRN_ROWS, A_HEAD_DIM
    n_groups = seq // R
    tab = tab_ref[...]
    e = jnp.exp(tab - jnp.max(tab, axis=0, keepdims=True))
    sm = e / jnp.sum(e, axis=0, keepdims=True)
    lb_all = jnp.sum(sm[:layer + 1], axis=0, keepdims=True)

    row = lax.broadcasted_iota(jnp.int32, (R, R), 0)
    col = lax.broadcasted_iota(jnp.int32, (R, R), 1)
    same = (row // C) == (col // C)
    lower = same & (col <= row)
    upper = same & (col >= row)
    X = BF16_SUBLANES
    jrow = lax.broadcasted_iota(jnp.int32, (X, R), 0)
    jcol = lax.broadcasted_iota(jnp.int32, (X, R), 1)
    in_chunk = (jcol // C) == jrow
    last_rows = in_chunk.astype(BF16)
    mid_f = (in_chunk & (jcol % C <= C // 2 - 1)).astype(BF16)
    mid_b = (in_chunk & (jcol % C >= C // 2)).astype(BF16)
    cum_f = jnp.concatenate([lower.astype(BF16), mid_f, last_rows], axis=0)
    cum_b = jnp.concatenate([upper.astype(BF16), mid_b, last_rows], axis=0)

    for ref in (k2_ref, qb_ref):
        ref[...] = jnp.zeros(ref.shape, BF16)

    def per_chunk(extra, j0):
        return jnp.concatenate(
            [jnp.broadcast_to(extra[j0 + c:j0 + c + 1], (C, hd)) for c in range(G)], axis=0)

    def prepare(start, d, slot):
        f_ref, cum, lb = d["f"], d["cum"], d["lb"]
        rows = pl.ds(start, R)
        f = lb + (1.0 - lb) * jax.nn.sigmoid(f_ref[rows, d["cs"]])
        k = 1.0 - f
        lf = jnp.log(f)
        hi = lf.astype(BF16)
        r1 = lf - hi.astype(F32)
        mid = r1.astype(BF16)
        lo = (r1 - mid.astype(F32)).astype(BF16)
        yield
        ball = jnp.dot(cum, jnp.concatenate([hi, mid, lo], axis=1), preferred_element_type=F32)
        ball = ball[:, 0:hd] + ball[:, hd:2 * hd] + ball[:, 2 * hd:3 * hd]
        yield
        b = ball[0:R]
        b_mid = per_chunk(ball, R)
        b_last = per_chunk(ball, R + X)
        q = q_ref[rows, d["cs"]].astype(F32)
        d["qe"][slot] = (q * jnp.exp(b - b_mid)).astype(BF16)
        d["ke"][slot] = (k * jnp.exp(b_mid - b)).astype(BF16)
        yield
        qb = (q * jnp.exp(b)).astype(BF16)
        k2 = (k * jnp.exp(b_last - b)).astype(BF16)
        for c in range(G):
            d["k2"][slot, c * C:(c + 1) * C, c * hd:(c + 1) * hd] = k2[c * C:(c + 1) * C]
            d["qb"][slot, c * C:(c + 1) * C, c * hd:(c + 1) * hd] = qb[c * C:(c + 1) * C]
        d["dec"][slot] = jnp.exp(ball[R + X:R + X + 8])
        yield

    def apply(start, d, slot, st):
        rows = pl.ds(start, R)
        v = i_ref[rows, d["cs"]]
        sc = lax.dot_general(d["qe"][slot], d["ke"][slot], NT_DIMS, preferred_element_type=F32)
        sc = jnp.where(d["mask"], sc, 0.0)
        yield
        o = jnp.dot(sc.astype(BF16), v, preferred_element_type=F32)
        kv_t = jnp.dot(v.astype(F32).T.astype(BF16), d["k2"][slot], preferred_element_type=F32)
        yield
        dec = d["dec"][slot]
        entering = [None] * G
        for c in (reversed(range(G)) if d["reverse"] else range(G)):
            entering[c] = st.astype(BF16)
            st = st * dec[c:c + 1] + kv_t[:, c * hd:(c + 1) * hd]
        yield
        o = o + lax.dot_general(d["qb"][slot], jnp.concatenate(entering, axis=1), NT_DIMS,
                                preferred_element_type=F32)
        d["out"][rows, :] = o
        return st

    def interleave(*gens):
        results = [None] * len(gens)
        active = list(enumerate(gens))
        while active:
            still = []
            for idx, g in active:
                try:
                    next(g)
                    still.append((idx, g))
                except StopIteration as done:
                    results[idx] = done.value
            active = still
        return results

    streams = []
    for h in range(n_heads):
        cs = slice(h * hd, (h + 1) * hd)
        for direction, (f_ref, cum, mask) in enumerate(((ff_ref, cum_f, lower), (fb_ref, cum_b, upper))):
            i = 2 * h + direction
            streams.append(dict(f=f_ref, cs=cs, lb=lb_all[:, cs], cum=cum, mask=mask, reverse=direction == 1,
                                out=acc_ref.at[i], qe=qe_ref.at[i], ke=ke_ref.at[i], k2=k2_ref.at[i],
                                qb=qb_ref.at[i], dec=dec_ref.at[i]))
    n_streams = len(streams)

    def starts(g):
        g = jnp.minimum(g, n_groups - 1)
        return pl.multiple_of(g * R, R), pl.multiple_of((n_groups - 1 - g) * R, R)

    interleave(*[prepare(starts(0)[i % 2], d, 0) for i, d in enumerate(streams)])

    def finish(start, h):
        rows = pl.ds(start, R)
        cs = slice(h * hd, (h + 1) * hd)
        o = acc_ref[2 * h, rows, :] + acc_ref[2 * h + 1, rows, :]
        ms = jnp.mean(o * o, axis=-1, keepdims=True)
        yield
        y = o * lax.rsqrt(ms + RMS_EPS) * ng_ref[:, cs]
        g = g_ref[rows, cs].astype(F32)
        yield
        o_ref[rows, cs] = (y * (g * jax.nn.sigmoid(g))).astype(o_ref.dtype)

    def half(g, carry, slot, done):
        a, p = starts(g), starts(g + 1)
        gens = [apply(a[i % 2], d, slot, carry[i]) for i, d in enumerate(streams)]
        gens += [prepare(p[i % 2], d, 1 - slot) for i, d in enumerate(streams)]
        if done is not None:
            gens += [finish(r, h) for h in range(n_heads) for r in starts(done)]
        return tuple(interleave(*gens)[:n_streams])

    assert n_groups % 4 == 0
    mid = n_groups // 2

    def early(n, carry):
        for slot in (0, 1):
            carry = half(2 * n + slot, carry, slot, None)
        return carry

    def late(n, carry):
        for slot in (0, 1):
            carry = half(2 * n + slot, carry, slot, 2 * n + slot - 1)
        return carry

    zero = jnp.zeros((hd, hd), F32)
    carry = lax.fori_loop(0, mid // 2, early, (zero,) * n_streams)
    carry = half(mid, carry, 0, None)
    carry = half(mid + 1, carry, 1, mid)
    lax.fori_loop(mid // 2 + 1, n_groups // 2, late, carry)
    interleave(*[finish(r, h) for h in range(n_heads) for r in starts(n_groups - 1)])


def _hgrn(h_act, h_gate, lb_table, norm_g, layer, batch, seq, n_heads):
    hd = A_HEAD_DIM
    w = n_heads * hd
    blocks = A_HEADS // n_heads
    n_streams = 2 * n_heads

    def col(part):
        return pl.BlockSpec((None, seq, w), lambda b, h: (b, 0, part * blocks + h))

    n_tab = lb_table.shape[0]
    return pl.pallas_call(
        functools.partial(_hgrn_kernel, layer=layer, seq=seq, n_heads=n_heads),
        grid=(batch, blocks),
        in_specs=[pl.BlockSpec((n_tab, w), lambda b, h: (0, h)),
                  col(0), col(0), col(1), col(1), col(2),
                  pl.BlockSpec((1, w), lambda b, h: (0, h))],
        out_specs=pl.BlockSpec((None, seq, w), lambda b, h: (b, 0, h)),
        out_shape=jax.ShapeDtypeStruct((batch, seq, A_WIDTH), BF16),
        scratch_shapes=[pltpu.VMEM((n_streams, seq, hd), F32),
                        pltpu.VMEM((n_streams, 2, HGRN_ROWS, hd), BF16),
                        pltpu.VMEM((n_streams, 2, HGRN_ROWS, hd), BF16),
                        pltpu.VMEM((n_streams, 2, HGRN_ROWS, HGRN_GROUP * hd), BF16),
                        pltpu.VMEM((n_streams, 2, HGRN_ROWS, HGRN_GROUP * hd), BF16),
                        pltpu.VMEM((n_streams, 2, 8, hd), F32)],
        compiler_params=_cparams("parallel", "parallel"),
        name="hgrn2",
    )(lb_table, h_act, h_gate, h_gate, h_act, h_act, norm_g)


def _gmlp_kernel(u_ref, v_ref, lg_ref, lbias_ref, ws_ref, bias_ref, o_ref, *, rows):
    vln = _layer_norm(v_ref[...].astype(F32), lg_ref[...], lbias_ref[...]).astype(BF16)
    for c in range(rows // B_CHUNK):
        r = slice(c * B_CHUNK, (c + 1) * B_CHUNK)
        for g in range(B_GROUPS):
            cs = slice(g * B_GROUP_DIM, (g + 1) * B_GROUP_DIM)
            s = jnp.dot(ws_ref[g], vln[r, cs], preferred_element_type=F32) + bias_ref[g]
            o_ref[r, cs] = (u_ref[r, cs] * s).astype(o_ref.dtype)


def _gmlp(h_act, u_blk, ln_g, ln_b, ws, bias, rows):
    M = h_act.shape[0]
    bias_b = jnp.broadcast_to(bias[:, :, None], (B_GROUPS, B_CHUNK, B_GROUP_DIM))
    return pl.pallas_call(
        functools.partial(_gmlp_kernel, rows=rows),
        grid=(M // rows,),
        in_specs=[pl.BlockSpec((rows, B_WIDTH), lambda i: (i, u_blk)),
                  pl.BlockSpec((rows, B_WIDTH), lambda i: (i, u_blk + 1)),
                  pl.BlockSpec((1, B_WIDTH), lambda i: (0, 0)),
                  pl.BlockSpec((1, B_WIDTH), lambda i: (0, 0)),
                  pl.BlockSpec((B_GROUPS, B_CHUNK, B_CHUNK), lambda i: (0, 0, 0)),
                  pl.BlockSpec((B_GROUPS, B_CHUNK, B_GROUP_DIM), lambda i: (0, 0, 0))],
        out_specs=pl.BlockSpec((rows, B_WIDTH), lambda i: (i, 0)),
        out_shape=jax.ShapeDtypeStruct((M, B_WIDTH), BF16),
        compiler_params=_cparams("parallel"),
        name="gmlp",
    )(h_act, h_act, ln_g.reshape(1, -1), ln_b.reshape(1, -1), ws.astype(BF16), bias_b)


PROJ_STAGE_ROWS = 512


def _proj_ln_kernel(*refs, n_in, n_out, layer):
    ins = refs[:n_in]
    w_hbm, x_ref, g_ref, b_ref = refs[n_in:n_in + 4]
    out_refs = refs[n_in + 4:n_in + 4 + n_out]
    wb_ref, stage_ref, sem = refs[n_in + 4 + n_out:]
    K = wb_ref.shape[0]
    n_stage = K // PROJ_STAGE_ROWS

    @pl.when(pl.program_id(0) == 0)
    def _():
        def chunk_copy(c):
            rows = pl.ds(c * PROJ_STAGE_ROWS, PROJ_STAGE_ROWS)
            return pltpu.make_async_copy(w_hbm.at[layer, rows, :], stage_ref.at[c % 2], sem.at[c % 2])

        chunk_copy(0).start()
        for c in range(n_stage):
            if c + 1 < n_stage:
                chunk_copy(c + 1).start()
            chunk_copy(c).wait()
            wb_ref[c * PROJ_STAGE_ROWS:(c + 1) * PROJ_STAGE_ROWS, :] = stage_ref[c % 2].astype(BF16)

    acc = None
    off = 0
    for r in ins:
        kk = r.shape[1]
        d = jnp.dot(r[...], wb_ref[off:off + kk, :], preferred_element_type=F32)
        acc = d if acc is None else acc + d
        off += kk
    out = _layer_norm(ALPHA * x_ref[...] + acc, g_ref[...], b_ref[...])
    for o_ref in out_refs:
        o_ref[...] = out.astype(o_ref.dtype)


def _proj_ln(parts, w, layer, x, g, b, tm, out_dtypes, name):
    M, D = x.shape
    K = w.shape[1]
    assert K % PROJ_STAGE_ROWS == 0
    in_specs = [pl.BlockSpec((tm, p.shape[1]), lambda i: (i, 0)) for p in parts]
    in_specs += [pl.BlockSpec(memory_space=pl.ANY),
                 pl.BlockSpec((tm, D), lambda i: (i, 0)),
                 pl.BlockSpec((1, D), lambda i: (0, 0)),
                 pl.BlockSpec((1, D), lambda i: (0, 0))]
    return pl.pallas_call(
        functools.partial(_proj_ln_kernel, n_in=len(parts), n_out=len(out_dtypes), layer=layer),
        grid=(M // tm,),
        in_specs=in_specs,
        out_specs=[pl.BlockSpec((tm, D), lambda i: (i, 0)) for _ in out_dtypes],
        out_shape=[jax.ShapeDtypeStruct((M, D), dt) for dt in out_dtypes],
        scratch_shapes=[pltpu.VMEM((K, D), BF16),
                        pltpu.VMEM((2, PROJ_STAGE_ROWS, D), F32),
                        pltpu.SemaphoreType.DMA((2,))],
        compiler_params=_cparams("arbitrary"),
        name=name,
    )(*parts, w, x, g.reshape(1, -1), b.reshape(1, -1))


def _ffn_up_kernel(xm_ref, xp_ref, xn_ref, wg_ref, wv_ref, cwg_ref, cwv_ref, cbg_ref, cbv_ref,
                   o_ref, lhs_ref, wgb_ref, wvb_ref, *, tm, seq):
    H = BF16_SUBLANES
    i = pl.program_id(1)

    @pl.when(i == 0)
    def _():
        wgb_ref[...] = wg_ref[...].astype(BF16)
        wvb_ref[...] = wv_ref[...].astype(BF16)

    t0 = (i * tm) % seq
    zero = jnp.zeros((H, xm_ref.shape[1]), BF16)
    lhs_ref[0:H, :] = jnp.where(t0 != 0, xp_ref[...], zero)
    lhs_ref[H:H + tm, :] = xm_ref[...]
    lhs_ref[H + tm:2 * H + tm, :] = jnp.where(t0 + tm != seq, xn_ref[...], zero)
    lhs = lhs_ref[...]

    def conv(w_ref, cw_ref, cb_ref):
        h = jnp.dot(lhs, w_ref[...], preferred_element_type=F32)
        cw = cw_ref[...]
        n = h.shape[0]
        prev = pltpu.roll(h, 1, 0)[H:H + tm]
        nxt = pltpu.roll(h, n - 1, 0)[H:H + tm]
        return prev * cw[0:1] + h[H:H + tm] * cw[1:2] + nxt * cw[2:3] + cb_ref[...]

    gate = conv(wgb_ref, cwg_ref, cbg_ref)
    val = conv(wvb_ref, cwv_ref, cbv_ref)
    o_ref[...] = (gate * jax.nn.sigmoid(gate) * val).astype(o_ref.dtype)


def _ffn_up(xb, w_up, conv_w, conv_b, layer, seq, tm, tn):
    M, D = xb.shape
    H = BF16_SUBLANES
    nj = D_FF // tn
    hb = tm // H
    n_hblk = M // H
    conv_b = conv_b.reshape(conv_b.shape[0], 1, -1)
    return pl.pallas_call(
        functools.partial(_ffn_up_kernel, tm=tm, seq=seq),
        grid=(nj, M // tm),
        in_specs=[pl.BlockSpec((tm, D), lambda j, i: (i, 0)),
                  pl.BlockSpec((H, D), lambda j, i: (jnp.maximum(i * hb - 1, 0), 0)),
                  pl.BlockSpec((H, D), lambda j, i: (jnp.minimum((i + 1) * hb, n_hblk - 1), 0)),
                  pl.BlockSpec((None, D, tn), lambda j, i: (layer, 0, j)),
                  pl.BlockSpec((None, D, tn), lambda j, i: (layer, 0, j + nj)),
                  pl.BlockSpec((None, 3, tn), lambda j, i: (layer, 0, j)),
                  pl.BlockSpec((None, 3, tn), lambda j, i: (layer, 0, j + nj)),
                  pl.BlockSpec((None, 1, tn), lambda j, i: (layer, 0, j)),
                  pl.BlockSpec((None, 1, tn), lambda j, i: (layer, 0, j + nj))],
        out_specs=pl.BlockSpec((tm, tn), lambda j, i: (i, j)),
        out_shape=jax.ShapeDtypeStruct((M, D_FF), BF16),
        scratch_shapes=[pltpu.VMEM((tm + 2 * H, D), BF16),
                        pltpu.VMEM((D, tn), BF16), pltpu.VMEM((D, tn), BF16)],
        compiler_params=_cparams("parallel", "arbitrary"),
        name="ffn_up_conv",
    )(xb, xb, xb, w_up, w_up, conv_w, conv_w, conv_b, conv_b)


def _rope_tables(seq):
    t = np.arange(seq)
    r = (t // GRID_W).astype(np.float64)
    c = (t % GRID_W).astype(np.float64)
    half = C_HEAD_DIM // 2
    inv_freq = np.exp(-math.log(ROPE_THETA) * np.arange(0, half, 2, dtype=np.float64) / half)
    ang_r = r[:, None] * inv_freq
    ang_c = c[:, None] * inv_freq
    ang = np.concatenate([ang_r, ang_r, ang_c, ang_c], axis=-1)
    sign = np.tile(np.concatenate([-np.ones(half // 2), np.ones(half // 2)]), 2)
    return jnp.asarray(np.cos(ang), F32), jnp.asarray(np.sin(ang) * sign, F32)


def _qkv_kernel(x_ref, w_ref, gain_ref, cos_ref, sin_ref, o_ref, wb_ref, h_ref,
                *, rows_per_col, n_tiles, n_norm_heads):
    s = pl.program_id(0)
    hd = C_HEAD_DIM

    @pl.when(s == 0)
    def _():
        h_ref[1] = jnp.zeros(h_ref.shape[1:], F32)

    @pl.when((s % rows_per_col == 0) & (s < n_tiles))
    def _():
        wb_ref[...] = w_ref[...].astype(BF16)

    heads_per_tile = o_ref.shape[1] // hd
    first_head = (jnp.maximum(s - 1, 0) // rows_per_col) * heads_per_tile

    def step(fill, drain):
        cos = cos_ref[...]
        sin = sin_ref[...]
        lane = lax.broadcasted_iota(jnp.int32, (1, hd), 1)
        low = (lane % (hd // 2)) < (hd // 4)
        for h in range(heads_per_tile):
            cs = slice(h * hd, (h + 1) * hd)
            a = h_ref[drain, :, cs]
            y = a * lax.rsqrt(jnp.mean(a * a, axis=-1, keepdims=True) + RMS_EPS) * gain_ref[:, cs]
            partner = jnp.where(low, pltpu.roll(y, hd - hd // 4, 1), pltpu.roll(y, hd // 4, 1))
            plain = first_head + h >= n_norm_heads
            o_ref[:, cs] = jnp.where(plain, a, y * cos + partner * sin).astype(o_ref.dtype)
        h_ref[fill] = jnp.dot(x_ref[...], wb_ref[...], preferred_element_type=F32)

    @pl.when(s % 2 == 0)
    def _():
        step(0, 1)

    @pl.when(s % 2 == 1)
    def _():
        step(1, 0)


def _qkv_proj(xb, w, layer, gain, cos, sin, seq, tm, tn):
    M, D = xb.shape
    N = w.shape[2]
    n_pos_blk = seq // tm
    ni = M // tm
    n_tiles = (N // tn) * ni

    def fill(s):
        t = jnp.minimum(s, n_tiles - 1)
        return t // ni, t % ni

    def drain(s):
        t = jnp.maximum(s - 1, 0)
        return t // ni, t % ni

    return pl.pallas_call(
        functools.partial(_qkv_kernel, rows_per_col=ni, n_tiles=n_tiles,
                          n_norm_heads=C_Q_HEADS + C_KV_HEADS),
        grid=(n_tiles + 1,),
        in_specs=[pl.BlockSpec((tm, D), lambda s: (fill(s)[1], 0)),
                  pl.BlockSpec((None, D, tn), lambda s: (layer, 0, fill(s)[0])),
                  pl.BlockSpec((1, tn), lambda s: (0, drain(s)[0])),
                  pl.BlockSpec((tm, C_HEAD_DIM), lambda s: (drain(s)[1] % n_pos_blk, 0)),
                  pl.BlockSpec((tm, C_HEAD_DIM), lambda s: (drain(s)[1] % n_pos_blk, 0))],
        out_specs=pl.BlockSpec((tm, tn), lambda s: (drain(s)[1], drain(s)[0])),
        out_shape=jax.ShapeDtypeStruct((M, N), BF16),
        scratch_shapes=[pltpu.VMEM((D, tn), BF16), pltpu.VMEM((2, tm, tn), F32)],
        compiler_params=_cparams("arbitrary"),
        name="qkv_norm_rope",
    )(xb, w, gain, cos, sin)


def _attn_kernel(q_ref, k_ref, v_ref, o_ref, vt_ref, st_ref, m_ref, *, tq, n_sub, blocks_per_head, n_chunks):
    hd = C_HEAD_DIM
    s = pl.program_id(0)
    seq = k_ref.shape[0]
    ck = seq // n_chunks

    @pl.when(s == 0)
    def _():
        st_ref[1] = jnp.zeros(st_ref.shape[1:], F32)
        m_ref[1] = jnp.zeros(m_ref.shape[1:], F32)

    @pl.when(jnp.maximum(s - 1, 0) % blocks_per_head == 0)
    def _():
        vt_ref[0:hd, :] = v_ref[...].astype(F32).T.astype(BF16)
        vt_ref[hd:, :] = jnp.ones((vt_ref.shape[0] - hd, seq), BF16)

    sub = tq // n_sub

    def write_out(a, acc):
        o = acc[0:hd] * (1.0 / acc[hd:hd + 1])
        for g in range(C_GROUP):
            o_ref[a * sub:(a + 1) * sub, g * hd:(g + 1) * hd] = (
                o[:, g * sub:(g + 1) * sub].T.astype(o_ref.dtype))

    def step(fill, drain):
        qs = [jnp.concatenate([q_ref[a * sub:(a + 1) * sub, g * hd:(g + 1) * hd] for g in range(C_GROUP)],
                              axis=0) for a in range(n_sub)]
        pending = None
        for a in range(n_sub):
            m_drain = m_ref[drain, a]
            m_fill = None
            acc = None
            for c in range(n_chunks):
                rows = slice(c * ck, (c + 1) * ck)
                sc = lax.dot_general(k_ref[rows, :], qs[a], NT_DIMS, preferred_element_type=F32)
                st_ref[fill, a, rows, :] = sc
                mc = jnp.max(sc, axis=0, keepdims=True)
                m_fill = mc if c == 0 else jnp.maximum(m_fill, mc)
                if c == 1 and pending is not None:
                    write_out(*pending)
                    pending = None
                p = jnp.exp2(st_ref[drain, a, rows, :] - m_drain).astype(BF16)
                pv = jnp.dot(vt_ref[:, rows], p, preferred_element_type=F32)
                acc = pv if c == 0 else acc + pv
            m_ref[fill, a] = m_fill
            pending = (a, acc)
        write_out(*pending)

    @pl.when(s % 2 == 0)
    def _():
        step(0, 1)

    @pl.when(s % 2 == 1)
    def _():
        step(1, 0)


def _attention(qkv, batch, seq, tq, n_sub, n_chunks):
    hd = C_HEAD_DIM
    gw = C_GROUP * hd
    bph = seq // tq
    n_blocks = batch * C_KV_HEADS * bph

    def coords(blk):
        return blk // (C_KV_HEADS * bph), (blk // bph) % C_KV_HEADS, blk % bph

    def q_map(s):
        b, h, i = coords(jnp.minimum(s, n_blocks - 1))
        return b, i, h

    def k_map(s):
        b, h, _ = coords(jnp.minimum(s, n_blocks - 1))
        return b, 0, C_Q_HEADS + h

    def v_map(s):
        b, h, _ = coords(jnp.maximum(s - 1, 0))
        return b, 0, C_Q_HEADS + C_KV_HEADS + h

    def o_map(s):
        b, h, i = coords(jnp.maximum(s - 1, 0))
        return b, i, h

    return pl.pallas_call(
        functools.partial(_attn_kernel, tq=tq, n_sub=n_sub, blocks_per_head=bph, n_chunks=n_chunks),
        grid=(n_blocks + 1,),
        in_specs=[pl.BlockSpec((None, tq, gw), q_map),
                  pl.BlockSpec((None, seq, hd), k_map),
                  pl.BlockSpec((None, seq, hd), v_map)],
        out_specs=pl.BlockSpec((None, tq, gw), o_map),
        out_shape=jax.ShapeDtypeStruct((batch, seq, C_Q_HEADS * hd), BF16),
        scratch_shapes=[pltpu.VMEM((hd + BF16_SUBLANES, seq), BF16),
                        pltpu.VMEM((2, n_sub, seq, C_GROUP * tq // n_sub), F32),
                        pltpu.VMEM((2, n_sub, 1, C_GROUP * tq // n_sub), F32)],
        compiler_params=_cparams("arbitrary"),
        name="gqa_attention",
    )(qkv, qkv, qkv)


INPROJ_GATE_ROWS, INPROJ_ACT_ROWS, INPROJ_COLS = 1024, 2048, 1024
GMLP_ROWS = 512
HGRN_HEADS_PER_STEP = 2
OUTPROJ_ROWS, DOWNPROJ_ROWS = 512, 256
FFN_UP_ROWS, FFN_UP_COLS = 1024, 512
QKV_ROWS, QKV_COLS = 1024, 1024
ATTN_QUERIES, ATTN_SUB_BLOCKS, ATTN_KEY_CHUNKS = 256, 2, 8


def _conv_ffn_ln(x, xb, w_up, conv_w, conv_b, w_down, g, b, layer, seq, out_dtypes):
    act = _ffn_up(xb, w_up, conv_w, conv_b, layer, seq, tm=FFN_UP_ROWS, tn=FFN_UP_COLS)
    return _proj_ln([act], w_down, layer, x, g[layer], b[layer], DOWNPROJ_ROWS, out_dtypes, "ffn_down_ln")


def kernel(x, w_in_ab, hgrn_lb_table, hgrn_norm_g, gmlp_ln_g, gmlp_ln_b, gmlp_ws, gmlp_bias,
           w_out_ab, w_in_attn, q_norm_g, k_norm_g, w_out_attn, ffn_up, ffn_conv_w, ffn_conv_b,
           ffn_down, ln1_g, ln1_b, ln2_g, ln2_b):
    batch, seq, d = x.shape
    M = batch * seq
    x = x.reshape(M, d)
    xb = x.astype(BF16)

    assert INPROJ_COLS == A_WIDTH == B_WIDTH
    h_gate = _matmul(xb, w_in_ab, 0, 2, lambda j: j + 1, INPROJ_GATE_ROWS, INPROJ_COLS, F32, "inproj_gates")
    h_act = _matmul(xb, w_in_ab, 0, 5, lambda j: j + 2 * jnp.minimum(j, 1), INPROJ_ACT_ROWS, INPROJ_COLS,
                    BF16, "inproj_acts")
    o_a = _hgrn(h_act.reshape(batch, seq, -1), h_gate.reshape(batch, seq, -1), hgrn_lb_table,
                hgrn_norm_g[0].reshape(1, -1), 0, batch, seq, HGRN_HEADS_PER_STEP)
    o_b = _gmlp(h_act, 3, gmlp_ln_g[0], gmlp_ln_b[0], gmlp_ws[0], gmlp_bias[0], rows=GMLP_ROWS)
    both = (F32, BF16)
    x, xb = _proj_ln([o_a.reshape(M, -1), o_b], w_out_ab, 0, x, ln1_g[0], ln1_b[0], OUTPROJ_ROWS, both,
                     "outproj_ln")
    x, xb = _conv_ffn_ln(x, xb, ffn_up, ffn_conv_w, ffn_conv_b, ffn_down, ln2_g, ln2_b, 0, seq, both)

    scale = C_HEAD_DIM ** -0.5 * math.log2(math.e)
    gain = jnp.concatenate([jnp.tile(q_norm_g[0] * scale, C_Q_HEADS), jnp.tile(k_norm_g[0], C_KV_HEADS),
                            jnp.ones((C_KV_HEADS * C_HEAD_DIM,), F32)])
    cos, sin = _rope_tables(seq)
    qkv = _qkv_proj(xb, w_in_attn, 0, gain.reshape(1, -1), cos, sin, seq, tm=QKV_ROWS, tn=QKV_COLS)
    att = _attention(qkv.reshape(batch, seq, -1), batch, seq, tq=ATTN_QUERIES, n_sub=ATTN_SUB_BLOCKS,
                     n_chunks=ATTN_KEY_CHUNKS)
    x, xb = _proj_ln([att.reshape(M, -1)], w_out_attn, 0, x, ln1_g[1], ln1_b[1], OUTPROJ_ROWS, both,
                     "outproj_ln")
    (x,) = _conv_ffn_ln(x, xb, ffn_up, ffn_conv_w, ffn_conv_b, ffn_down, ln2_g, ln2_b, 1, seq, (F32,))
    return x.reshape(batch, seq, d)
```

```python
import functools
import math

import jax
import jax.numpy as jnp
import numpy as np
from jax import lax
from jax.experimental import pallas as pl
from jax.experimental.pallas import tpu as pltpu

F32 = jnp.float32
BF16 = jnp.bfloat16

D_MODEL = 2048
GRID_W = 64
A_HEAD_DIM = 128
A_WIDTH = D_MODEL // 2
A_HEADS = A_WIDTH // A_HEAD_DIM
HGRN_CHUNK = 64
B_WIDTH = D_MODEL // 2
B_GROUP_DIM = 128
B_GROUPS = B_WIDTH // B_GROUP_DIM
B_CHUNK = 128
C_HEAD_DIM = 128
C_Q_HEADS = D_MODEL // C_HEAD_DIM
C_KV_HEADS = C_Q_HEADS // 4
C_GROUP = C_Q_HEADS // C_KV_HEADS
ROPE_THETA = 10000.0
D_FF = 5632
DEPTH = 2
ALPHA = (2.0 * DEPTH) ** 0.25
LN_EPS = 1e-5
RMS_EPS = 1e-6

V7X_VMEM_LIMIT_BYTES = 56 * 1024 * 1024
BF16_SUBLANES = 16

NT_DIMS = (((1,), (1,)), ((), ()))


def _cparams(*sem):
    return pltpu.CompilerParams(dimension_semantics=sem, vmem_limit_bytes=V7X_VMEM_LIMIT_BYTES)


def _layer_norm(y, g, b):
    mu = jnp.mean(y, axis=-1, keepdims=True)
    yc = y - mu
    var = jnp.mean(yc * yc, axis=-1, keepdims=True)
    return yc * lax.rsqrt(var + LN_EPS) * g + b


def _mm_kernel(x_ref, w_ref, o_ref, wb_ref):
    @pl.when(pl.program_id(1) == 0)
    def _():
        wb_ref[...] = w_ref[...].astype(BF16)

    o_ref[...] = jnp.dot(x_ref[...], wb_ref[...], preferred_element_type=F32).astype(o_ref.dtype)


def _matmul(x, w, layer, n_col_tiles, col_tile_of, tm, tn, out_dtype, name):
    M, K = x.shape
    return pl.pallas_call(
        _mm_kernel,
        grid=(n_col_tiles, M // tm),
        in_specs=[pl.BlockSpec((tm, K), lambda j, i: (i, 0)),
                  pl.BlockSpec((None, K, tn), lambda j, i: (layer, 0, col_tile_of(j)))],
        out_specs=pl.BlockSpec((tm, tn), lambda j, i: (i, j)),
        out_shape=jax.ShapeDtypeStruct((M, n_col_tiles * tn), out_dtype),
        scratch_shapes=[pltpu.VMEM((K, tn), BF16)],
        compiler_params=_cparams("parallel", "arbitrary"),
        name=name,
    )(x, w)


HGRN_GROUP = 4
HGRN_ROWS = HGRN_GROUP * HGRN_CHUNK


def _hgrn_kernel(tab_ref, q_ref, ff_ref, fb_ref, i_ref, g_ref, ng_ref, o_ref,
                 acc_ref, qe_ref, ke_ref, k2_ref, qb_ref, dec_ref, *, layer, seq, n_heads):
    C, G, R, hd = HGRN_CHUNK, HGRN_GROUP, HGRN_ROWS, A_HEAD_DIM
    n_groups = seq // R
    tab = tab_ref[...]
    e = jnp.exp(tab - jnp.max(tab, axis=0, keepdims=True))
    sm = e / jnp.sum(e, axis=0, keepdims=True)
    lb_all = jnp.sum(sm[:layer + 1], axis=0, keepdims=True)

    row = lax.broadcasted_iota(jnp.int32, (R, R), 0)
    col = lax.broadcasted_iota(jnp.int32, (R, R), 1)
    same = (row // C) == (col // C)
    lower = same & (col <= row)
    upper = same & (col >= row)
    X = BF16_SUBLANES
    jrow = lax.broadcasted_iota(jnp.int32, (X, R), 0)
    jcol = lax.broadcasted_iota(jnp.int32, (X, R), 1)
    in_chunk = (jcol // C) == jrow
    last_rows = in_chunk.astype(BF16)
    mid_f = (in_chunk & (jcol % C <= C // 2 - 1)).astype(BF16)
    mid_b = (in_chunk & (jcol % C >= C // 2)).astype(BF16)
    cum_f = jnp.concatenate([lower.astype(BF16), mid_f, last_rows], axis=0)
    cum_b = jnp.concatenate([upper.astype(BF16), mid_b, last_rows], axis=0)

    for ref in (k2_ref, qb_ref):
        ref[...] = jnp.zeros(ref.shape, BF16)

    def per_chunk(extra, j0):
        return jnp.concatenate(
            [jnp.broadcast_to(extra[j0 + c:j0 + c + 1], (C, hd)) for c in range(G)], axis=0)

    def prepare(start, d, slot):
        f_ref, cum, lb = d["f"], d["cum"], d["lb"]
        rows = pl.ds(start, R)
        f = lb + (1.0 - lb) * jax.nn.sigmoid(f_ref[rows, d["cs"]])
        k = 1.0 - f
        lf = jnp.log(f)
        hi = lf.astype(BF16)
        r1 = lf - hi.astype(F32)
        mid = r1.astype(BF16)
        lo = (r1 - mid.astype(F32)).astype(BF16)
        yield
        ball = jnp.dot(cum, jnp.concatenate([hi, mid, lo], axis=1), preferred_element_type=F32)
        ball = ball[:, 0:hd] + ball[:, hd:2 * hd] + ball[:, 2 * hd:3 * hd]
        yield
        b = ball[0:R]
        b_mid = per_chunk(ball, R)
        b_last = per_chunk(ball, R + X)
        q = q_ref[rows, d["cs"]].astype(F32)
        d["qe"][slot] = (q * jnp.exp(b - b_mid)).astype(BF16)
        d["ke"][slot] = (k * jnp.exp(b_mid - b)).astype(BF16)
        yield
        qb = (q * jnp.exp(b)).astype(BF16)
        k2 = (k * jnp.exp(b_last - b)).astype(BF16)
        for c in range(G):
            d["k2"][slot, c * C:(c + 1) * C, c * hd:(c + 1) * hd] = k2[c * C:(c + 1) * C]
            d["qb"][slot, c * C:(c + 1) * C, c * hd:(c + 1) * hd] = qb[c * C:(c + 1) * C]
        d["dec"][slot] = jnp.exp(ball[R + X:R + X + 8])
        yield

    def apply(start, d, slot, st):
        rows = pl.ds(start, R)
        v = i_ref[rows, d["cs"]]
        sc = lax.dot_general(d["qe"][slot], d["ke"][slot], NT_DIMS, preferred_element_type=F32)
        sc = jnp.where(d["mask"], sc, 0.0)
        yield
        o = jnp.dot(sc.astype(BF16), v, preferred_element_type=F32)
        kv_t = jnp.dot(v.astype(F32).T.astype(BF16), d["k2"][slot], preferred_element_type=F32)
        yield
        dec = d["dec"][slot]
        entering = [None] * G
        for c in (reversed(range(G)) if d["reverse"] else range(G)):
            entering[c] = st.astype(BF16)
            st = st * dec[c:c + 1] + kv_t[:, c * hd:(c + 1) * hd]
        yield
        o = o + lax.dot_general(d["qb"][slot], jnp.concatenate(entering, axis=1), NT_DIMS,
                                preferred_element_type=F32)
        d["out"][rows, :] = o
        return st

    def interleave(*gens):
        results = [None] * len(gens)
        active = list(enumerate(gens))
        while active:
            still = []
            for idx, g in active:
                try:
                    next(g)
                    still.append((idx, g))
                except StopIteration as done:
                    results[idx] = done.value
            active = still
        return results

    streams = []
    for h in range(n_heads):
        cs = slice(h * hd, (h + 1) * hd)
        for direction, (f_ref, cum, mask) in enumerate(((ff_ref, cum_f, lower), (fb_ref, cum_b, upper))):
            i = 2 * h + direction
            streams.append(dict(f=f_ref, cs=cs, lb=lb_all[:, cs], cum=cum, mask=mask, reverse=direction == 1,
                                out=acc_ref.at[i], qe=qe_ref.at[i], ke=ke_ref.at[i], k2=k2_ref.at[i],
                                qb=qb_ref.at[i], dec=dec_ref.at[i]))
    n_streams = len(streams)

    def starts(g):
        g = jnp.minimum(g, n_groups - 1)
        return pl.multiple_of(g * R, R), pl.multiple_of((n_groups - 1 - g) * R, R)

    interleave(*[prepare(starts(0)[i % 2], d, 0) for i, d in enumerate(streams)])

    def finish(start, h):
        rows = pl.ds(start, R)
        cs = slice(h * hd, (h + 1) * hd)
        o = acc_ref[2 * h, rows, :] + acc_ref[2 * h + 1, rows, :]
        ms = jnp.mean(o * o, axis=-1, keepdims=True)
        yield
        y = o * lax.rsqrt(ms + RMS_EPS) * ng_ref[:, cs]
        g = g_ref[rows, cs].astype(F32)
        yield
        o_ref[rows, cs] = (y * (g * jax.nn.sigmoid(g))).astype(o_ref.dtype)

    def half(g, carry, slot, done):
        a, p = starts(g), starts(g + 1)
        gens = [apply(a[i % 2], d, slot, carry[i]) for i, d in enumerate(streams)]
        gens += [prepare(p[i % 2], d, 1 - slot) for i, d in enumerate(streams)]
        if done is not None:
            gens += [finish(r, h) for h in range(n_heads) for r in starts(done)]
        return tuple(interleave(*gens)[:n_streams])

    assert n_groups % 4 == 0
    mid = n_groups // 2

    def early(n, carry):
        for slot in (0, 1):
            carry = half(2 * n + slot, carry, slot, None)
        return carry

    def late(n, carry):
        for slot in (0, 1):
            carry = half(2 * n + slot, carry, slot, 2 * n + slot - 1)
        return carry

    zero = jnp.zeros((hd, hd), F32)
    carry = lax.fori_loop(0, mid // 2, early, (zero,) * n_streams)
    carry = half(mid, carry, 0, None)
    carry = half(mid + 1, carry, 1, mid)
    lax.fori_loop(mid // 2 + 1, n_groups // 2, late, carry)
    interleave(*[finish(r, h) for h in range(n_heads) for r in starts(n_groups - 1)])


def _hgrn(h_act, h_gate, lb_table, norm_g, layer, batch, seq, n_heads):
    hd = A_HEAD_DIM
    w = n_heads * hd
    blocks = A_HEADS // n_heads
    n_streams = 2 * n_heads

    def col(part):
        return pl.BlockSpec((None, seq, w), lambda b, h: (b, 0, part * blocks + h))

    n_tab = lb_table.shape[0]
    return pl.pallas_call(
        functools.partial(_hgrn_kernel, layer=layer, seq=seq, n_heads=n_heads),
        grid=(batch, blocks),
        in_specs=[pl.BlockSpec((n_tab, w), lambda b, h: (0, h)),
                  col(0), col(0), col(1), col(1), col(2),
                  pl.BlockSpec((1, w), lambda b, h: (0, h))],
        out_specs=pl.BlockSpec((None, seq, w), lambda b, h: (b, 0, h)),
        out_shape=jax.ShapeDtypeStruct((batch, seq, A_WIDTH), BF16),
        scratch_shapes=[pltpu.VMEM((n_streams, seq, hd), F32),
                        pltpu.VMEM((n_streams, 2, HGRN_ROWS, hd), BF16),
                        pltpu.VMEM((n_streams, 2, HGRN_ROWS, hd), BF16),
                        pltpu.VMEM((n_streams, 2, HGRN_ROWS, HGRN_GROUP * hd), BF16),
                        pltpu.VMEM((n_streams, 2, HGRN_ROWS, HGRN_GROUP * hd), BF16),
                        pltpu.VMEM((n_streams, 2, 8, hd), F32)],
        compiler_params=_cparams("parallel", "parallel"),
        name="hgrn2",
    )(lb_table, h_act, h_gate, h_gate, h_act, h_act, norm_g)


def _gmlp_kernel(u_ref, v_ref, lg_ref, lbias_ref, ws_ref, bias_ref, o_ref, *, rows):
    vln = _layer_norm(v_ref[...].astype(F32), lg_ref[...], lbias_ref[...]).astype(BF16)
    for c in range(rows // B_CHUNK):
        r = slice(c * B_CHUNK, (c + 1) * B_CHUNK)
        for g in range(B_GROUPS):
            cs = slice(g * B_GROUP_DIM, (g + 1) * B_GROUP_DIM)
            s = jnp.dot(ws_ref[g], vln[r, cs], preferred_element_type=F32) + bias_ref[g]
            o_ref[r, cs] = (u_ref[r, cs] * s).astype(o_ref.dtype)


def _gmlp(h_act, u_blk, ln_g, ln_b, ws, bias, rows):
    M = h_act.shape[0]
    bias_b = jnp.broadcast_to(bias[:, :, None], (B_GROUPS, B_CHUNK, B_GROUP_DIM))
    return pl.pallas_call(
        functools.partial(_gmlp_kernel, rows=rows),
        grid=(M // rows,),
        in_specs=[pl.BlockSpec((rows, B_WIDTH), lambda i: (i, u_blk)),
                  pl.BlockSpec((rows, B_WIDTH), lambda i: (i, u_blk + 1)),
                  pl.BlockSpec((1, B_WIDTH), lambda i: (0, 0)),
                  pl.BlockSpec((1, B_WIDTH), lambda i: (0, 0)),
                  pl.BlockSpec((B_GROUPS, B_CHUNK, B_CHUNK), lambda i: (0, 0, 0)),
                  pl.BlockSpec((B_GROUPS, B_CHUNK, B_GROUP_DIM), lambda i: (0, 0, 0))],
        out_specs=pl.BlockSpec((rows, B_WIDTH), lambda i: (i, 0)),
        out_shape=jax.ShapeDtypeStruct((M, B_WIDTH), BF16),
        compiler_params=_cparams("parallel"),
        name="gmlp",
    )(h_act, h_act, ln_g.reshape(1, -1), ln_b.reshape(1, -1), ws.astype(BF16), bias_b)


PROJ_STAGE_ROWS = 512


def _proj_ln_kernel(*refs, n_in, n_out, layer):
    ins = refs[:n_in]
    w_hbm, x_ref, g_ref, b_ref = refs[n_in:n_in + 4]
    out_refs = refs[n_in + 4:n_in + 4 + n_out]
    wb_ref, stage_ref, sem = refs[n_in + 4 + n_out:]
    K = wb_ref.shape[0]
    n_stage = K // PROJ_STAGE_ROWS

    @pl.when(pl.program_id(0) == 0)
    def _():
        def chunk_copy(c):
            rows = pl.ds(c * PROJ_STAGE_ROWS, PROJ_STAGE_ROWS)
            return pltpu.make_async_copy(w_hbm.at[layer, rows, :], stage_ref.at[c % 2], sem.at[c % 2])

        chunk_copy(0).start()
        for c in range(n_stage):
            if c + 1 < n_stage:
                chunk_copy(c + 1).start()
            chunk_copy(c).wait()
            wb_ref[c * PROJ_STAGE_ROWS:(c + 1) * PROJ_STAGE_ROWS, :] = stage_ref[c % 2].astype(BF16)

    acc = None
    off = 0
    for r in ins:
        kk = r.shape[1]
        d = jnp.dot(r[...], wb_ref[off:off + kk, :], preferred_element_type=F32)
        acc = d if acc is None else acc + d
        off += kk
    out = _layer_norm(ALPHA * x_ref[...] + acc, g_ref[...], b_ref[...])
    for o_ref in out_refs:
        o_ref[...] = out.astype(o_ref.dtype)


def _proj_ln(parts, w, layer, x, g, b, tm, out_dtypes, name):
    M, D = x.shape
    K = w.shape[1]
    assert K % PROJ_STAGE_ROWS == 0
    in_specs = [pl.BlockSpec((tm, p.shape[1]), lambda i: (i, 0)) for p in parts]
    in_specs += [pl.BlockSpec(memory_space=pl.ANY),
                 pl.BlockSpec((tm, D), lambda i: (i, 0)),
                 pl.BlockSpec((1, D), lambda i: (0, 0)),
                 pl.BlockSpec((1, D), lambda i: (0, 0))]
    return pl.pallas_call(
        functools.partial(_proj_ln_kernel, n_in=len(parts), n_out=len(out_dtypes), layer=layer),
        grid=(M // tm,),
        in_specs=in_specs,
        out_specs=[pl.BlockSpec((tm, D), lambda i: (i, 0)) for _ in out_dtypes],
        out_shape=[jax.ShapeDtypeStruct((M, D), dt) for dt in out_dtypes],
        scratch_shapes=[pltpu.VMEM((K, D), BF16),
                        pltpu.VMEM((2, PROJ_STAGE_ROWS, D), F32),
                        pltpu.SemaphoreType.DMA((2,))],
        compiler_params=_cparams("arbitrary"),
        name=name,
    )(*parts, w, x, g.reshape(1, -1), b.reshape(1, -1))


def _ffn_up_kernel(xm_ref, xp_ref, xn_ref, wg_ref, wv_ref, cwg_ref, cwv_ref, cbg_ref, cbv_ref,
                   o_ref, lhs_ref, wgb_ref, wvb_ref, *, tm, seq):
    H = BF16_SUBLANES
    i = pl.program_id(1)

    @pl.when(i == 0)
    def _():
        wgb_ref[...] = wg_ref[...].astype(BF16)
        wvb_ref[...] = wv_ref[...].astype(BF16)

    t0 = (i * tm) % seq
    zero = jnp.zeros((H, xm_ref.shape[1]), BF16)
    lhs_ref[0:H, :] = jnp.where(t0 != 0, xp_ref[...], zero)
    lhs_ref[H:H + tm, :] = xm_ref[...]
    lhs_ref[H + tm:2 * H + tm, :] = jnp.where(t0 + tm != seq, xn_ref[...], zero)
    lhs = lhs_ref[...]

    def conv(w_ref, cw_ref, cb_ref):
        h = jnp.dot(lhs, w_ref[...], preferred_element_type=F32)
        cw = cw_ref[...]
        n = h.shape[0]
        prev = pltpu.roll(h, 1, 0)[H:H + tm]
        nxt = pltpu.roll(h, n - 1, 0)[H:H + tm]
        return prev * cw[0:1] + h[H:H + tm] * cw[1:2] + nxt * cw[2:3] + cb_ref[...]

    gate = conv(wgb_ref, cwg_ref, cbg_ref)
    val = conv(wvb_ref, cwv_ref, cbv_ref)
    o_ref[...] = (gate * jax.nn.sigmoid(gate) * val).astype(o_ref.dtype)


def _ffn_up(xb, w_up, conv_w, conv_b, layer, seq, tm, tn):
    M, D = xb.shape
    H = BF16_SUBLANES
    nj = D_FF // tn
    hb = tm // H
    n_hblk = M // H
    conv_b = conv_b.reshape(conv_b.shape[0], 1, -1)
    return pl.pallas_call(
        functools.partial(_ffn_up_kernel, tm=tm, seq=seq),
        grid=(nj, M // tm),
        in_specs=[pl.BlockSpec((tm, D), lambda j, i: (i, 0)),
                  pl.BlockSpec((H, D), lambda j, i: (jnp.maximum(i * hb - 1, 0), 0)),
                  pl.BlockSpec((H, D), lambda j, i: (jnp.minimum((i + 1) * hb, n_hblk - 1), 0)),
                  pl.BlockSpec((None, D, tn), lambda j, i: (layer, 0, j)),
                  pl.BlockSpec((None, D, tn), lambda j, i: (layer, 0, j + nj)),
                  pl.BlockSpec((None, 3, tn), lambda j, i: (layer, 0, j)),
                  pl.BlockSpec((None, 3, tn), lambda j, i: (layer, 0, j + nj)),
                  pl.BlockSpec((None, 1, tn), lambda j, i: (layer, 0, j)),
                  pl.BlockSpec((None, 1, tn), lambda j, i: (layer, 0, j + nj))],
        out_specs=pl.BlockSpec((tm, tn), lambda j, i: (i, j)),
        out_shape=jax.ShapeDtypeStruct((M, D_FF), BF16),
        scratch_shapes=[pltpu.VMEM((tm + 2 * H, D), BF16),
                        pltpu.VMEM((D, tn), BF16), pltpu.VMEM((D, tn), BF16)],
        compiler_params=_cparams("parallel", "arbitrary"),
        name="ffn_up_conv",
    )(xb, xb, xb, w_up, w_up, conv_w, conv_w, conv_b, conv_b)


def _rope_tables(seq):
    t = np.arange(seq)
    r = (t // GRID_W).astype(np.float64)
    c = (t % GRID_W).astype(np.float64)
    half = C_HEAD_DIM // 2
    inv_freq = np.exp(-math.log(ROPE_THETA) * np.arange(0, half, 2, dtype=np.float64) / half)
    ang_r = r[:, None] * inv_freq
    ang_c = c[:, None] * inv_freq
    ang = np.concatenate([ang_r, ang_r, ang_c, ang_c], axis=-1)
    sign = np.tile(np.concatenate([-np.ones(half // 2), np.ones(half // 2)]), 2)
    return jnp.asarray(np.cos(ang), F32), jnp.asarray(np.sin(ang) * sign, F32)


def _qkv_kernel(x_ref, w_ref, gain_ref, cos_ref, sin_ref, o_ref, wb_ref, h_ref,
                *, rows_per_col, n_tiles, n_norm_heads):
    s = pl.program_id(0)
    hd = C_HEAD_DIM

    @pl.when(s == 0)
    def _():
        h_ref[1] = jnp.zeros(h_ref.shape[1:], F32)

    @pl.when((s % rows_per_col == 0) & (s < n_tiles))
    def _():
        wb_ref[...] = w_ref[...].astype(BF16)

    heads_per_tile = o_ref.shape[1] // hd
    first_head = (jnp.maximum(s - 1, 0) // rows_per_col) * heads_per_tile

    def step(fill, drain):
        cos = cos_ref[...]
        sin = sin_ref[...]
        lane = lax.broadcasted_iota(jnp.int32, (1, hd), 1)
        low = (lane % (hd // 2)) < (hd // 4)
        for h in range(heads_per_tile):
            cs = slice(h * hd, (h + 1) * hd)
            a = h_ref[drain, :, cs]
            y = a * lax.rsqrt(jnp.mean(a * a, axis=-1, keepdims=True) + RMS_EPS) * gain_ref[:, cs]
            partner = jnp.where(low, pltpu.roll(y, hd - hd // 4, 1), pltpu.roll(y, hd // 4, 1))
            plain = first_head + h >= n_norm_heads
            o_ref[:, cs] = jnp.where(plain, a, y * cos + partner * sin).astype(o_ref.dtype)
        h_ref[fill] = jnp.dot(x_ref[...], wb_ref[...], preferred_element_type=F32)

    @pl.when(s % 2 == 0)
    def _():
        step(0, 1)

    @pl.when(s % 2 == 1)
    def _():
        step(1, 0)


def _qkv_proj(xb, w, layer, gain, cos, sin, seq, tm, tn):
    M, D = xb.shape
    N = w.shape[2]
    n_pos_blk = seq // tm
    ni = M // tm
    n_tiles = (N // tn) * ni

    def fill(s):
        t = jnp.minimum(s, n_tiles - 1)
        return t // ni, t % ni

    def drain(s):
        t = jnp.maximum(s - 1, 0)
        return t // ni, t % ni

    return pl.pallas_call(
        functools.partial(_qkv_kernel, rows_per_col=ni, n_tiles=n_tiles,
                          n_norm_heads=C_Q_HEADS + C_KV_HEADS),
        grid=(n_tiles + 1,),
        in_specs=[pl.BlockSpec((tm, D), lambda s: (fill(s)[1], 0)),
                  pl.BlockSpec((None, D, tn), lambda s: (layer, 0, fill(s)[0])),
                  pl.BlockSpec((1, tn), lambda s: (0, drain(s)[0])),
                  pl.BlockSpec((tm, C_HEAD_DIM), lambda s: (drain(s)[1] % n_pos_blk, 0)),
                  pl.BlockSpec((tm, C_HEAD_DIM), lambda s: (drain(s)[1] % n_pos_blk, 0))],
        out_specs=pl.BlockSpec((tm, tn), lambda s: (drain(s)[1], drain(s)[0])),
        out_shape=jax.ShapeDtypeStruct((M, N), BF16),
        scratch_shapes=[pltpu.VMEM((D, tn), BF16), pltpu.VMEM((2, tm, tn), F32)],
        compiler_params=_cparams("arbitrary"),
        name="qkv_norm_rope",
    )(xb, w, gain, cos, sin)


def _attn_kernel(q_ref, k_ref, v_ref, o_ref, vt_ref, st_ref, m_ref, *, tq, n_sub, blocks_per_head, n_chunks):
    hd = C_HEAD_DIM
    s = pl.program_id(0)
    seq = k_ref.shape[0]
    ck = seq // n_chunks

    @pl.when(s == 0)
    def _():
        st_ref[1] = jnp.zeros(st_ref.shape[1:], F32)
        m_ref[1] = jnp.zeros(m_ref.shape[1:], F32)

    @pl.when(jnp.maximum(s - 1, 0) % blocks_per_head == 0)
    def _():
        vt_ref[0:hd, :] = v_ref[...].astype(F32).T.astype(BF16)
        vt_ref[hd:, :] = jnp.ones((vt_ref.shape[0] - hd, seq), BF16)

    sub = tq // n_sub

    def write_out(a, acc):
        o = acc[0:hd] * (1.0 / acc[hd:hd + 1])
        for g in range(C_GROUP):
            o_ref[a * sub:(a + 1) * sub, g * hd:(g + 1) * hd] = (
                o[:, g * sub:(g + 1) * sub].T.astype(o_ref.dtype))

    def step(fill, drain):
        qs = [jnp.concatenate([q_ref[a * sub:(a + 1) * sub, g * hd:(g + 1) * hd] for g in range(C_GROUP)],
                              axis=0) for a in range(n_sub)]
        pending = None
        for a in range(n_sub):
            m_drain = m_ref[drain, a]
            m_fill = None
            acc = None
            for c in range(n_chunks):
                rows = slice(c * ck, (c + 1) * ck)
                sc = lax.dot_general(k_ref[rows, :], qs[a], NT_DIMS, preferred_element_type=F32)
                st_ref[fill, a, rows, :] = sc
                mc = jnp.max(sc, axis=0, keepdims=True)
                m_fill = mc if c == 0 else jnp.maximum(m_fill, mc)
                if c == 1 and pending is not None:
                    write_out(*pending)
                    pending = None
                p = jnp.exp2(st_ref[drain, a, rows, :] - m_drain).astype(BF16)
                pv = jnp.dot(vt_ref[:, rows], p, preferred_element_type=F32)
                acc = pv if c == 0 else acc + pv
            m_ref[fill, a] = m_fill
            pending = (a, acc)
        write_out(*pending)

    @pl.when(s % 2 == 0)
    def _():
        step(0, 1)

    @pl.when(s % 2 == 1)
    def _():
        step(1, 0)


def _attention(qkv, batch, seq, tq, n_sub, n_chunks):
    hd = C_HEAD_DIM
    gw = C_GROUP * hd
    bph = seq // tq
    n_blocks = batch * C_KV_HEADS * bph

    def coords(blk):
        return blk // (C_KV_HEADS * bph), (blk // bph) % C_KV_HEADS, blk % bph

    def q_map(s):
        b, h, i = coords(jnp.minimum(s, n_blocks - 1))
        return b, i, h

    def k_map(s):
        b, h, _ = coords(jnp.minimum(s, n_blocks - 1))
        return b, 0, C_Q_HEADS + h

    def v_map(s):
        b, h, _ = coords(jnp.maximum(s - 1, 0))
        return b, 0, C_Q_HEADS + C_KV_HEADS + h

    def o_map(s):
        b, h, i = coords(jnp.maximum(s - 1, 0))
        return b, i, h

    return pl.pallas_call(
        functools.partial(_attn_kernel, tq=tq, n_sub=n_sub, blocks_per_head=bph, n_chunks=n_chunks),
        grid=(n_blocks + 1,),
        in_specs=[pl.BlockSpec((None, tq, gw), q_map),
                  pl.BlockSpec((None, seq, hd), k_map),
                  pl.BlockSpec((None, seq, hd), v_map)],
        out_specs=pl.BlockSpec((None, tq, gw), o_map),
        out_shape=jax.ShapeDtypeStruct((batch, seq, C_Q_HEADS * hd), BF16),
        scratch_shapes=[pltpu.VMEM((hd + BF16_SUBLANES, seq), BF16),
                        pltpu.VMEM((2, n_sub, seq, C_GROUP * tq // n_sub), F32),
                        pltpu.VMEM((2, n_sub, 1, C_GROUP * tq // n_sub), F32)],
        compiler_params=_cparams("arbitrary"),
        name="gqa_attention",
    )(qkv, qkv, qkv)


INPROJ_GATE_ROWS, INPROJ_ACT_ROWS, INPROJ_COLS = 1024, 2048, 1024
GMLP_ROWS = 1024
HGRN_HEADS_PER_STEP = 2
OUTPROJ_ROWS, DOWNPROJ_ROWS = 512, 256
FFN_UP_ROWS, FFN_UP_COLS = 1024, 512
QKV_ROWS, QKV_COLS = 1024, 1024
ATTN_QUERIES, ATTN_SUB_BLOCKS, ATTN_KEY_CHUNKS = 256, 2, 16


def _conv_ffn_ln(x, xb, w_up, conv_w, conv_b, w_down, g, b, layer, seq, out_dtypes):
    act = _ffn_up(xb, w_up, conv_w, conv_b, layer, seq, tm=FFN_UP_ROWS, tn=FFN_UP_COLS)
    return _proj_ln([act], w_down, layer, x, g[layer], b[layer], DOWNPROJ_ROWS, out_dtypes, "ffn_down_ln")


def kernel(x, w_in_ab, hgrn_lb_table, hgrn_norm_g, gmlp_ln_g, gmlp_ln_b, gmlp_ws, gmlp_bias,
           w_out_ab, w_in_attn, q_norm_g, k_norm_g, w_out_attn, ffn_up, ffn_conv_w, ffn_conv_b,
           ffn_down, ln1_g, ln1_b, ln2_g, ln2_b):
    batch, seq, d = x.shape
    M = batch * seq
    x = x.reshape(M, d)
    xb = x.astype(BF16)

    assert INPROJ_COLS == A_WIDTH == B_WIDTH
    h_gate = _matmul(xb, w_in_ab, 0, 2, lambda j: j + 1, INPROJ_GATE_ROWS, INPROJ_COLS, F32, "inproj_gates")
    h_act = _matmul(xb, w_in_ab, 0, 5, lambda j: j + 2 * jnp.minimum(j, 1), INPROJ_ACT_ROWS, INPROJ_COLS,
                    BF16, "inproj_acts")
    o_a = _hgrn(h_act.reshape(batch, seq, -1), h_gate.reshape(batch, seq, -1), hgrn_lb_table,
                hgrn_norm_g[0].reshape(1, -1), 0, batch, seq, HGRN_HEADS_PER_STEP)
    o_b = _gmlp(h_act, 3, gmlp_ln_g[0], gmlp_ln_b[0], gmlp_ws[0], gmlp_bias[0], rows=GMLP_ROWS)
    both = (F32, BF16)
    x, xb = _proj_ln([o_a.reshape(M, -1), o_b], w_out_ab, 0, x, ln1_g[0], ln1_b[0], OUTPROJ_ROWS, both,
                     "outproj_ln")
    x, xb = _conv_ffn_ln(x, xb, ffn_up, ffn_conv_w, ffn_conv_b, ffn_down, ln2_g, ln2_b, 0, seq, both)

    scale = C_HEAD_DIM ** -0.5 * math.log2(math.e)
    gain = jnp.concatenate([jnp.tile(q_norm_g[0] * scale, C_Q_HEADS), jnp.tile(k_norm_g[0], C_KV_HEADS),
                            jnp.ones((C_KV_HEADS * C_HEAD_DIM,), F32)])
    cos, sin = _rope_tables(seq)
    qkv = _qkv_proj(xb, w_in_attn, 0, gain.reshape(1, -1), cos, sin, seq, tm=QKV_ROWS, tn=QKV_COLS)
    att = _attention(qkv.reshape(batch, seq, -1), batch, seq, tq=ATTN_QUERIES, n_sub=ATTN_SUB_BLOCKS,
                     n_chunks=ATTN_KEY_CHUNKS)
    x, xb = _proj_ln([att.reshape(M, -1)], w_out_attn, 0, x, ln1_g[1], ln1_b[1], OUTPROJ_ROWS, both,
                     "outproj_ln")
    (x,) = _conv_ffn_ln(x, xb, ffn_up, ffn_conv_w, ffn_conv_b, ffn_down, ln2_g, ln2_b, 1, seq, (F32,))
    return x.reshape(batch, seq, d)
```

```python
import functools
import math

import jax
import jax.numpy as jnp
import numpy as np
from jax import lax
from jax.experimental import pallas as pl
from jax.experimental.pallas import tpu as pltpu

F32 = jnp.float32
BF16 = jnp.bfloat16

D_MODEL = 2048
GRID_W = 64
A_HEAD_DIM = 128
A_WIDTH = D_MODEL // 2
A_HEADS = A_WIDTH // A_HEAD_DIM
HGRN_CHUNK = 64
B_WIDTH = D_MODEL // 2
B_GROUP_DIM = 128
B_GROUPS = B_WIDTH // B_GROUP_DIM
B_CHUNK = 128
C_HEAD_DIM = 128
C_Q_HEADS = D_MODEL // C_HEAD_DIM
C_KV_HEADS = C_Q_HEADS // 4
C_GROUP = C_Q_HEADS // C_KV_HEADS
ROPE_THETA = 10000.0
D_FF = 5632
DEPTH = 2
ALPHA = (2.0 * DEPTH) ** 0.25
LN_EPS = 1e-5
RMS_EPS = 1e-6

V7X_VMEM_LIMIT_BYTES = 56 * 1024 * 1024
BF16_SUBLANES = 16

NT_DIMS = (((1,), (1,)), ((), ()))


def _cparams(*sem):
    return pltpu.CompilerParams(dimension_semantics=sem, vmem_limit_bytes=V7X_VMEM_LIMIT_BYTES)


def _layer_norm(y, g, b):
    mu = jnp.mean(y, axis=-1, keepdims=True)
    yc = y - mu
    var = jnp.mean(yc * yc, axis=-1, keepdims=True)
    return yc * lax.rsqrt(var + LN_EPS) * g + b


MM_RING = 3


def _mm_kernel(x_hbm, w_ref, o_ref, wb_ref, xbuf_ref, sem, *, n_row, n_steps, tm):
    i = pl.program_id(1)
    t = pl.program_id(0) * n_row + i

    def tile_copy(step):
        slot = step % MM_RING
        rows = pl.ds(pl.multiple_of((step % n_row) * tm, tm), tm)
        return pltpu.make_async_copy(x_hbm.at[rows, :], xbuf_ref.at[slot], sem.at[slot])

    @pl.when(t == 0)
    def _():
        for step in range(min(MM_RING - 1, n_steps)):
            tile_copy(step).start()

    @pl.when(t + MM_RING - 1 < n_steps)
    def _():
        tile_copy(t + MM_RING - 1).start()

    @pl.when(i == 0)
    def _():
        wb_ref[...] = w_ref[...].astype(BF16)

    tile_copy(t).wait()
    o_ref[...] = jnp.dot(xbuf_ref[t % MM_RING], wb_ref[...], preferred_element_type=F32).astype(o_ref.dtype)


def _matmul(x, w, layer, n_col_tiles, col_tile_of, tm, tn, out_dtype, name):
    M, K = x.shape
    n_row = M // tm
    return pl.pallas_call(
        functools.partial(_mm_kernel, n_row=n_row, n_steps=n_col_tiles * n_row, tm=tm),
        grid=(n_col_tiles, n_row),
        in_specs=[pl.BlockSpec(memory_space=pl.ANY),
                  pl.BlockSpec((None, K, tn), lambda j, i: (layer, 0, col_tile_of(j)))],
        out_specs=pl.BlockSpec((tm, tn), lambda j, i: (i, j)),
        out_shape=jax.ShapeDtypeStruct((M, n_col_tiles * tn), out_dtype),
        scratch_shapes=[pltpu.VMEM((K, tn), BF16),
                        pltpu.VMEM((MM_RING, tm, K), x.dtype),
                        pltpu.SemaphoreType.DMA((MM_RING,))],
        compiler_params=_cparams("arbitrary", "arbitrary"),
        name=name,
    )(x, w)


HGRN_GROUP = 4
HGRN_ROWS = HGRN_GROUP * HGRN_CHUNK


def _hgrn_kernel(tab_ref, q_ref, ff_ref, fb_ref, i_ref, g_ref, ng_ref, o_ref,
                 acc_ref, qe_ref, ke_ref, k2_ref, qb_ref, dec_ref, *, layer, seq, n_heads):
    C, G, R, hd = HGRN_CHUNK, HGRN_GROUP, HGRN_ROWS, A_HEAD_DIM
    n_groups = seq // R
    tab = tab_ref[...]
    e = jnp.exp(tab - jnp.max(tab, axis=0, keepdims=True))
    sm = e / jnp.sum(e, axis=0, keepdims=True)
    lb_all = jnp.sum(sm[:layer + 1], axis=0, keepdims=True)

    row = lax.broadcasted_iota(jnp.int32, (R, R), 0)
    col = lax.broadcasted_iota(jnp.int32, (R, R), 1)
    same = (row // C) == (col // C)
    lower = same & (col <= row)
    upper = same & (col >= row)
    X = BF16_SUBLANES
    jrow = lax.broadcasted_iota(jnp.int32, (X, R), 0)
    jcol = lax.broadcasted_iota(jnp.int32, (X, R), 1)
    in_chunk = (jcol // C) == jrow
    last_rows = in_chunk.astype(BF16)
    mid_f = (in_chunk & (jcol % C <= C // 2 - 1)).astype(BF16)
    mid_b = (in_chunk & (jcol % C >= C // 2)).astype(BF16)
    cum_f = jnp.concatenate([lower.astype(BF16), mid_f, last_rows], axis=0)
    cum_b = jnp.concatenate([upper.astype(BF16), mid_b, last_rows], axis=0)

    for ref in (k2_ref, qb_ref):
        ref[...] = jnp.zeros(ref.shape, BF16)

    def per_chunk(extra, j0):
        return jnp.concatenate(
            [jnp.broadcast_to(extra[j0 + c:j0 + c + 1], (C, hd)) for c in range(G)], axis=0)

    def prepare(start, d, slot):
        f_ref, cum, lb = d["f"], d["cum"], d["lb"]
        rows = pl.ds(start, R)
        f = lb + (1.0 - lb) * jax.nn.sigmoid(f_ref[rows, d["cs"]])
        k = 1.0 - f
        lf = jnp.log(f)
        hi = lf.astype(BF16)
        r1 = lf - hi.astype(F32)
        mid = r1.astype(BF16)
        lo = (r1 - mid.astype(F32)).astype(BF16)
        yield
        ball = jnp.dot(cum, jnp.concatenate([hi, mid, lo], axis=1), preferred_element_type=F32)
        ball = ball[:, 0:hd] + ball[:, hd:2 * hd] + ball[:, 2 * hd:3 * hd]
        yield
        b = ball[0:R]
        b_mid = per_chunk(ball, R)
        b_last = per_chunk(ball, R + X)
        q = q_ref[rows, d["cs"]].astype(F32)
        d["qe"][slot] = (q * jnp.exp(b - b_mid)).astype(BF16)
        d["ke"][slot] = (k * jnp.exp(b_mid - b)).astype(BF16)
        yield
        qb = (q * jnp.exp(b)).astype(BF16)
        k2 = (k * jnp.exp(b_last - b)).astype(BF16)
        for c in range(G):
            d["k2"][slot, c * C:(c + 1) * C, c * hd:(c + 1) * hd] = k2[c * C:(c + 1) * C]
            d["qb"][slot, c * C:(c + 1) * C, c * hd:(c + 1) * hd] = qb[c * C:(c + 1) * C]
        d["dec"][slot] = jnp.exp(ball[R + X:R + X + 8])
        yield

    def apply(start, d, slot, st):
        rows = pl.ds(start, R)
        v = i_ref[rows, d["cs"]]
        sc = lax.dot_general(d["qe"][slot], d["ke"][slot], NT_DIMS, preferred_element_type=F32)
        sc = jnp.where(d["mask"], sc, 0.0)
        yield
        o = jnp.dot(sc.astype(BF16), v, preferred_element_type=F32)
        kv_t = jnp.dot(v.astype(F32).T.astype(BF16), d["k2"][slot], preferred_element_type=F32)
        yield
        dec = d["dec"][slot]
        entering = [None] * G
        for c in (reversed(range(G)) if d["reverse"] else range(G)):
            entering[c] = st.astype(BF16)
            st = st * dec[c:c + 1] + kv_t[:, c * hd:(c + 1) * hd]
        yield
        o = o + lax.dot_general(d["qb"][slot], jnp.concatenate(entering, axis=1), NT_DIMS,
                                preferred_element_type=F32)
        d["out"][rows, :] = o
        return st

    def interleave(*gens):
        results = [None] * len(gens)
        active = list(enumerate(gens))
        while active:
            still = []
            for idx, g in active:
                try:
                    next(g)
                    still.append((idx, g))
                except StopIteration as done:
                    results[idx] = done.value
            active = still
        return results

    streams = []
    for h in range(n_heads):
        cs = slice(h * hd, (h + 1) * hd)
        for direction, (f_ref, cum, mask) in enumerate(((ff_ref, cum_f, lower), (fb_ref, cum_b, upper))):
            i = 2 * h + direction
            streams.append(dict(f=f_ref, cs=cs, lb=lb_all[:, cs], cum=cum, mask=mask, reverse=direction == 1,
                                out=acc_ref.at[i], qe=qe_ref.at[i], ke=ke_ref.at[i], k2=k2_ref.at[i],
                                qb=qb_ref.at[i], dec=dec_ref.at[i]))
    n_streams = len(streams)

    def starts(g):
        g = jnp.minimum(g, n_groups - 1)
        return pl.multiple_of(g * R, R), pl.multiple_of((n_groups - 1 - g) * R, R)

    interleave(*[prepare(starts(0)[i % 2], d, 0) for i, d in enumerate(streams)])

    def finish(start, h):
        rows = pl.ds(start, R)
        cs = slice(h * hd, (h + 1) * hd)
        o = acc_ref[2 * h, rows, :] + acc_ref[2 * h + 1, rows, :]
        ms = jnp.mean(o * o, axis=-1, keepdims=True)
        yield
        y = o * lax.rsqrt(ms + RMS_EPS) * ng_ref[:, cs]
        g = g_ref[rows, cs].astype(F32)
        yield
        o_ref[rows, cs] = (y * (g * jax.nn.sigmoid(g))).astype(o_ref.dtype)

    def half(g, carry, slot, done):
        a, p = starts(g), starts(g + 1)
        gens = [apply(a[i % 2], d, slot, carry[i]) for i, d in enumerate(streams)]
        gens += [prepare(p[i % 2], d, 1 - slot) for i, d in enumerate(streams)]
        if done is not None:
            gens += [finish(r, h) for h in range(n_heads) for r in starts(done)]
        return tuple(interleave(*gens)[:n_streams])

    assert n_groups % 4 == 0
    mid = n_groups // 2

    def early(n, carry):
        for slot in (0, 1):
            carry = half(2 * n + slot, carry, slot, None)
        return carry

    def late(n, carry):
        for slot in (0, 1):
            carry = half(2 * n + slot, carry, slot, 2 * n + slot - 1)
        return carry

    zero = jnp.zeros((hd, hd), F32)
    carry = lax.fori_loop(0, mid // 2, early, (zero,) * n_streams)
    carry = half(mid, carry, 0, None)
    carry = half(mid + 1, carry, 1, mid)
    lax.fori_loop(mid // 2 + 1, n_groups // 2, late, carry)
    interleave(*[finish(r, h) for h in range(n_heads) for r in starts(n_groups - 1)])


def _hgrn(h_act, h_gate, lb_table, norm_g, layer, batch, seq, n_heads):
    hd = A_HEAD_DIM
    w = n_heads * hd
    blocks = A_HEADS // n_heads
    n_streams = 2 * n_heads

    def col(part):
        return pl.BlockSpec((None, seq, w), lambda b, h: (b, 0, part * blocks + h))

    n_tab = lb_table.shape[0]
    return pl.pallas_call(
        functools.partial(_hgrn_kernel, layer=layer, seq=seq, n_heads=n_heads),
        grid=(batch, blocks),
        in_specs=[pl.BlockSpec((n_tab, w), lambda b, h: (0, h)),
                  col(0), col(0), col(1), col(1), col(2),
                  pl.BlockSpec((1, w), lambda b, h: (0, h))],
        out_specs=pl.BlockSpec((None, seq, w), lambda b, h: (b, 0, h)),
        out_shape=jax.ShapeDtypeStruct((batch, seq, A_WIDTH), BF16),
        scratch_shapes=[pltpu.VMEM((n_streams, seq, hd), F32),
                        pltpu.VMEM((n_streams, 2, HGRN_ROWS, hd), BF16),
                        pltpu.VMEM((n_streams, 2, HGRN_ROWS, hd), BF16),
                        pltpu.VMEM((n_streams, 2, HGRN_ROWS, HGRN_GROUP * hd), BF16),
                        pltpu.VMEM((n_streams, 2, HGRN_ROWS, HGRN_GROUP * hd), BF16),
                        pltpu.VMEM((n_streams, 2, 8, hd), F32)],
        compiler_params=_cparams("parallel", "parallel"),
        name="hgrn2",
    )(lb_table, h_act, h_gate, h_gate, h_act, h_act, norm_g)


def _gmlp_kernel(u_ref, v_ref, lg_ref, lbias_ref, ws_ref, bias_ref, o_ref, *, rows):
    vln = _layer_norm(v_ref[...].astype(F32), lg_ref[...], lbias_ref[...]).astype(BF16)
    for c in range(rows // B_CHUNK):
        r = slice(c * B_CHUNK, (c + 1) * B_CHUNK)
        for g in range(B_GROUPS):
            cs = slice(g * B_GROUP_DIM, (g + 1) * B_GROUP_DIM)
            s = jnp.dot(ws_ref[g], vln[r, cs], preferred_element_type=F32) + bias_ref[g]
            o_ref[r, cs] = (u_ref[r, cs] * s).astype(o_ref.dtype)


def _gmlp(h_act, u_blk, ln_g, ln_b, ws, bias, rows):
    M = h_act.shape[0]
    bias_b = jnp.broadcast_to(bias[:, :, None], (B_GROUPS, B_CHUNK, B_GROUP_DIM))
    return pl.pallas_call(
        functools.partial(_gmlp_kernel, rows=rows),
        grid=(M // rows,),
        in_specs=[pl.BlockSpec((rows, B_WIDTH), lambda i: (i, u_blk)),
                  pl.BlockSpec((rows, B_WIDTH), lambda i: (i, u_blk + 1)),
                  pl.BlockSpec((1, B_WIDTH), lambda i: (0, 0)),
                  pl.BlockSpec((1, B_WIDTH), lambda i: (0, 0)),
                  pl.BlockSpec((B_GROUPS, B_CHUNK, B_CHUNK), lambda i: (0, 0, 0)),
                  pl.BlockSpec((B_GROUPS, B_CHUNK, B_GROUP_DIM), lambda i: (0, 0, 0))],
        out_specs=pl.BlockSpec((rows, B_WIDTH), lambda i: (i, 0)),
        out_shape=jax.ShapeDtypeStruct((M, B_WIDTH), BF16),
        compiler_params=_cparams("parallel"),
        name="gmlp",
    )(h_act, h_act, ln_g.reshape(1, -1), ln_b.reshape(1, -1), ws.astype(BF16), bias_b)


PROJ_STAGE_ROWS = 512


def _proj_ln_kernel(*refs, n_in, n_out, layer):
    ins = refs[:n_in]
    w_hbm, x_ref, g_ref, b_ref = refs[n_in:n_in + 4]
    out_refs = refs[n_in + 4:n_in + 4 + n_out]
    wb_ref, stage_ref, sem = refs[n_in + 4 + n_out:]
    K = wb_ref.shape[0]
    n_stage = K // PROJ_STAGE_ROWS

    @pl.when(pl.program_id(0) == 0)
    def _():
        def chunk_copy(c):
            rows = pl.ds(c * PROJ_STAGE_ROWS, PROJ_STAGE_ROWS)
            return pltpu.make_async_copy(w_hbm.at[layer, rows, :], stage_ref.at[c % 2], sem.at[c % 2])

        chunk_copy(0).start()
        for c in range(n_stage):
            if c + 1 < n_stage:
                chunk_copy(c + 1).start()
            chunk_copy(c).wait()
            wb_ref[c * PROJ_STAGE_ROWS:(c + 1) * PROJ_STAGE_ROWS, :] = stage_ref[c % 2].astype(BF16)

    acc = None
    off = 0
    for r in ins:
        kk = r.shape[1]
        d = jnp.dot(r[...], wb_ref[off:off + kk, :], preferred_element_type=F32)
        acc = d if acc is None else acc + d
        off += kk
    out = _layer_norm(ALPHA * x_ref[...] + acc, g_ref[...], b_ref[...])
    for o_ref in out_refs:
        o_ref[...] = out.astype(o_ref.dtype)


def _proj_ln(parts, w, layer, x, g, b, tm, out_dtypes, name):
    M, D = x.shape
    K = w.shape[1]
    assert K % PROJ_STAGE_ROWS == 0
    in_specs = [pl.BlockSpec((tm, p.shape[1]), lambda i: (i, 0)) for p in parts]
    in_specs += [pl.BlockSpec(memory_space=pl.ANY),
                 pl.BlockSpec((tm, D), lambda i: (i, 0)),
                 pl.BlockSpec((1, D), lambda i: (0, 0)),
                 pl.BlockSpec((1, D), lambda i: (0, 0))]
    return pl.pallas_call(
        functools.partial(_proj_ln_kernel, n_in=len(parts), n_out=len(out_dtypes), layer=layer),
        grid=(M // tm,),
        in_specs=in_specs,
        out_specs=[pl.BlockSpec((tm, D), lambda i: (i, 0)) for _ in out_dtypes],
        out_shape=[jax.ShapeDtypeStruct((M, D), dt) for dt in out_dtypes],
        scratch_shapes=[pltpu.VMEM((K, D), BF16),
                        pltpu.VMEM((2, PROJ_STAGE_ROWS, D), F32),
                        pltpu.SemaphoreType.DMA((2,))],
        compiler_params=_cparams("arbitrary"),
        name=name,
    )(*parts, w, x, g.reshape(1, -1), b.reshape(1, -1))


def _ffn_up_kernel(xm_ref, xp_ref, xn_ref, wg_ref, wv_ref, cwg_ref, cwv_ref, cbg_ref, cbv_ref,
                   o_ref, lhs_ref, wgb_ref, wvb_ref, *, tm, seq):
    H = BF16_SUBLANES
    i = pl.program_id(1)

    @pl.when(i == 0)
    def _():
        wgb_ref[...] = wg_ref[...].astype(BF16)
        wvb_ref[...] = wv_ref[...].astype(BF16)

    t0 = (i * tm) % seq
    zero = jnp.zeros((H, xm_ref.shape[1]), BF16)
    lhs_ref[0:H, :] = jnp.where(t0 != 0, xp_ref[...], zero)
    lhs_ref[H:H + tm, :] = xm_ref[...]
    lhs_ref[H + tm:2 * H + tm, :] = jnp.where(t0 + tm != seq, xn_ref[...], zero)
    lhs = lhs_ref[...]

    def conv(w_ref, cw_ref, cb_ref):
        h = jnp.dot(lhs, w_ref[...], preferred_element_type=F32)
        cw = cw_ref[...]
        n = h.shape[0]
        prev = pltpu.roll(h, 1, 0)[H:H + tm]
        nxt = pltpu.roll(h, n - 1, 0)[H:H + tm]
        return prev * cw[0:1] + h[H:H + tm] * cw[1:2] + nxt * cw[2:3] + cb_ref[...]

    gate = conv(wgb_ref, cwg_ref, cbg_ref)
    val = conv(wvb_ref, cwv_ref, cbv_ref)
    o_ref[...] = (gate * jax.nn.sigmoid(gate) * val).astype(o_ref.dtype)


def _ffn_up(xb, w_up, conv_w, conv_b, layer, seq, tm, tn):
    M, D = xb.shape
    H = BF16_SUBLANES
    nj = D_FF // tn
    hb = tm // H
    n_hblk = M // H
    conv_b = conv_b.reshape(conv_b.shape[0], 1, -1)
    return pl.pallas_call(
        functools.partial(_ffn_up_kernel, tm=tm, seq=seq),
        grid=(nj, M // tm),
        in_specs=[pl.BlockSpec((tm, D), lambda j, i: (i, 0)),
                  pl.BlockSpec((H, D), lambda j, i: (jnp.maximum(i * hb - 1, 0), 0)),
                  pl.BlockSpec((H, D), lambda j, i: (jnp.minimum((i + 1) * hb, n_hblk - 1), 0)),
                  pl.BlockSpec((None, D, tn), lambda j, i: (layer, 0, j)),
                  pl.BlockSpec((None, D, tn), lambda j, i: (layer, 0, j + nj)),
                  pl.BlockSpec((None, 3, tn), lambda j, i: (layer, 0, j)),
                  pl.BlockSpec((None, 3, tn), lambda j, i: (layer, 0, j + nj)),
                  pl.BlockSpec((None, 1, tn), lambda j, i: (layer, 0, j)),
                  pl.BlockSpec((None, 1, tn), lambda j, i: (layer, 0, j + nj))],
        out_specs=pl.BlockSpec((tm, tn), lambda j, i: (i, j)),
        out_shape=jax.ShapeDtypeStruct((M, D_FF), BF16),
        scratch_shapes=[pltpu.VMEM((tm + 2 * H, D), BF16),
                        pltpu.VMEM((D, tn), BF16), pltpu.VMEM((D, tn), BF16)],
        compiler_params=_cparams("parallel", "arbitrary"),
        name="ffn_up_conv",
    )(xb, xb, xb, w_up, w_up, conv_w, conv_w, conv_b, conv_b)


def _rope_tables(seq):
    t = np.arange(seq)
    r = (t // GRID_W).astype(np.float64)
    c = (t % GRID_W).astype(np.float64)
    half = C_HEAD_DIM // 2
    inv_freq = np.exp(-math.log(ROPE_THETA) * np.arange(0, half, 2, dtype=np.float64) / half)
    ang_r = r[:, None] * inv_freq
    ang_c = c[:, None] * inv_freq
    ang = np.concatenate([ang_r, ang_r, ang_c, ang_c], axis=-1)
    sign = np.tile(np.concatenate([-np.ones(half // 2), np.ones(half // 2)]), 2)
    return jnp.asarray(np.cos(ang), F32), jnp.asarray(np.sin(ang) * sign, F32)


def _qkv_kernel(x_ref, w_ref, gain_ref, cos_ref, sin_ref, o_ref, wb_ref, h_ref,
                *, rows_per_col, n_tiles, n_norm_heads):
    s = pl.program_id(0)
    hd = C_HEAD_DIM

    @pl.when(s == 0)
    def _():
        h_ref[1] = jnp.zeros(h_ref.shape[1:], F32)

    @pl.when((s % rows_per_col == 0) & (s < n_tiles))
    def _():
        wb_ref[...] = w_ref[...].astype(BF16)

    heads_per_tile = o_ref.shape[1] // hd
    first_head = (jnp.maximum(s - 1, 0) // rows_per_col) * heads_per_tile

    def step(fill, drain):
        cos = cos_ref[...]
        sin = sin_ref[...]
        lane = lax.broadcasted_iota(jnp.int32, (1, hd), 1)
        low = (lane % (hd // 2)) < (hd // 4)
        for h in range(heads_per_tile):
            cs = slice(h * hd, (h + 1) * hd)
            a = h_ref[drain, :, cs]
            y = a * lax.rsqrt(jnp.mean(a * a, axis=-1, keepdims=True) + RMS_EPS) * gain_ref[:, cs]
            partner = jnp.where(low, pltpu.roll(y, hd - hd // 4, 1), pltpu.roll(y, hd // 4, 1))
            plain = first_head + h >= n_norm_heads
            o_ref[:, cs] = jnp.where(plain, a, y * cos + partner * sin).astype(o_ref.dtype)
        h_ref[fill] = jnp.dot(x_ref[...], wb_ref[...], preferred_element_type=F32)

    @pl.when(s % 2 == 0)
    def _():
        step(0, 1)

    @pl.when(s % 2 == 1)
    def _():
        step(1, 0)


def _qkv_proj(xb, w, layer, gain, cos, sin, seq, tm, tn):
    M, D = xb.shape
    N = w.shape[2]
    n_pos_blk = seq // tm
    ni = M // tm
    n_tiles = (N // tn) * ni

    def fill(s):
        t = jnp.minimum(s, n_tiles - 1)
        return t // ni, t % ni

    def drain(s):
        t = jnp.maximum(s - 1, 0)
        return t // ni, t % ni

    return pl.pallas_call(
        functools.partial(_qkv_kernel, rows_per_col=ni, n_tiles=n_tiles,
                          n_norm_heads=C_Q_HEADS + C_KV_HEADS),
        grid=(n_tiles + 1,),
        in_specs=[pl.BlockSpec((tm, D), lambda s: (fill(s)[1], 0)),
                  pl.BlockSpec((None, D, tn), lambda s: (layer, 0, fill(s)[0])),
                  pl.BlockSpec((1, tn), lambda s: (0, drain(s)[0])),
                  pl.BlockSpec((tm, C_HEAD_DIM), lambda s: (drain(s)[1] % n_pos_blk, 0)),
                  pl.BlockSpec((tm, C_HEAD_DIM), lambda s: (drain(s)[1] % n_pos_blk, 0))],
        out_specs=pl.BlockSpec((tm, tn), lambda s: (drain(s)[1], drain(s)[0])),
        out_shape=jax.ShapeDtypeStruct((M, N), BF16),
        scratch_shapes=[pltpu.VMEM((D, tn), BF16), pltpu.VMEM((2, tm, tn), F32)],
        compiler_params=_cparams("arbitrary"),
        name="qkv_norm_rope",
    )(xb, w, gain, cos, sin)


def _attn_kernel(q_ref, k_ref, v_ref, o_ref, vt_ref, st_ref, m_ref, *, tq, n_sub, blocks_per_head, n_chunks):
    hd = C_HEAD_DIM
    s = pl.program_id(0)
    seq = k_ref.shape[0]
    ck = seq // n_chunks

    @pl.when(s == 0)
    def _():
        st_ref[1] = jnp.zeros(st_ref.shape[1:], F32)
        m_ref[1] = jnp.zeros(m_ref.shape[1:], F32)

    @pl.when(jnp.maximum(s - 1, 0) % blocks_per_head == 0)
    def _():
        vt_ref[0:hd, :] = v_ref[...].astype(F32).T.astype(BF16)
        vt_ref[hd:, :] = jnp.ones((vt_ref.shape[0] - hd, seq), BF16)

    sub = tq // n_sub

    def write_out(a, acc):
        o = acc[0:hd] * (1.0 / acc[hd:hd + 1])
        for g in range(C_GROUP):
            o_ref[a * sub:(a + 1) * sub, g * hd:(g + 1) * hd] = (
                o[:, g * sub:(g + 1) * sub].T.astype(o_ref.dtype))

    def step(fill, drain):
        qs = [jnp.concatenate([q_ref[a * sub:(a + 1) * sub, g * hd:(g + 1) * hd] for g in range(C_GROUP)],
                              axis=0) for a in range(n_sub)]
        pending = None
        for a in range(n_sub):
            m_drain = m_ref[drain, a]
            m_fill = None
            acc = None
            for c in range(n_chunks):
                rows = slice(c * ck, (c + 1) * ck)
                sc = lax.dot_general(k_ref[rows, :], qs[a], NT_DIMS, preferred_element_type=F32)
                st_ref[fill, a, rows, :] = sc
                mc = jnp.max(sc, axis=0, keepdims=True)
                m_fill = mc if c == 0 else jnp.maximum(m_fill, mc)
                if c == 1 and pending is not None:
                    write_out(*pending)
                    pending = None
                p = jnp.exp2(st_ref[drain, a, rows, :] - m_drain).astype(BF16)
                pv = jnp.dot(vt_ref[:, rows], p, preferred_element_type=F32)
                acc = pv if c == 0 else acc + pv
            m_ref[fill, a] = m_fill
            pending = (a, acc)
        write_out(*pending)

    @pl.when(s % 2 == 0)
    def _():
        step(0, 1)

    @pl.when(s % 2 == 1)
    def _():
        step(1, 0)


def _attention(qkv, batch, seq, tq, n_sub, n_chunks):
    hd = C_HEAD_DIM
    gw = C_GROUP * hd
    bph = seq // tq
    n_blocks = batch * C_KV_HEADS * bph

    def coords(blk):
        return blk // (C_KV_HEADS * bph), (blk // bph) % C_KV_HEADS, blk % bph

    def q_map(s):
        b, h, i = coords(jnp.minimum(s, n_blocks - 1))
        return b, i, h

    def k_map(s):
        b, h, _ = coords(jnp.minimum(s, n_blocks - 1))
        return b, 0, C_Q_HEADS + h

    def v_map(s):
        b, h, _ = coords(jnp.maximum(s - 1, 0))
        return b, 0, C_Q_HEADS + C_KV_HEADS + h

    def o_map(s):
        b, h, i = coords(jnp.maximum(s - 1, 0))
        return b, i, h

    return pl.pallas_call(
        functools.partial(_attn_kernel, tq=tq, n_sub=n_sub, blocks_per_head=bph, n_chunks=n_chunks),
        grid=(n_blocks + 1,),
        in_specs=[pl.BlockSpec((None, tq, gw), q_map),
                  pl.BlockSpec((None, seq, hd), k_map),
                  pl.BlockSpec((None, seq, hd), v_map)],
        out_specs=pl.BlockSpec((None, tq, gw), o_map),
        out_shape=jax.ShapeDtypeStruct((batch, seq, C_Q_HEADS * hd), BF16),
        scratch_shapes=[pltpu.VMEM((hd + BF16_SUBLANES, seq), BF16),
                        pltpu.VMEM((2, n_sub, seq, C_GROUP * tq // n_sub), F32),
                        pltpu.VMEM((2, n_sub, 1, C_GROUP * tq // n_sub), F32)],
        compiler_params=_cparams("arbitrary"),
        name="gqa_attention",
    )(qkv, qkv, qkv)


INPROJ_GATE_ROWS, INPROJ_ACT_ROWS, INPROJ_COLS = 1024, 1024, 1024
GMLP_ROWS = 1024
HGRN_HEADS_PER_STEP = 2
OUTPROJ_ROWS, DOWNPROJ_ROWS = 512, 256
FFN_UP_ROWS, FFN_UP_COLS = 1024, 512
QKV_ROWS, QKV_COLS = 1024, 1024
ATTN_QUERIES, ATTN_SUB_BLOCKS, ATTN_KEY_CHUNKS = 256, 2, 16


def _conv_ffn_ln(x, xb, w_up, conv_w, conv_b, w_down, g, b, layer, seq, out_dtypes):
    act = _ffn_up(xb, w_up, conv_w, conv_b, layer, seq, tm=FFN_UP_ROWS, tn=FFN_UP_COLS)
    return _proj_ln([act], w_down, layer, x, g[layer], b[layer], DOWNPROJ_ROWS, out_dtypes, "ffn_down_ln")


def kernel(x, w_in_ab, hgrn_lb_table, hgrn_norm_g, gmlp_ln_g, gmlp_ln_b, gmlp_ws, gmlp_bias,
           w_out_ab, w_in_attn, q_norm_g, k_norm_g, w_out_attn, ffn_up, ffn_conv_w, ffn_conv_b,
           ffn_down, ln1_g, ln1_b, ln2_g, ln2_b):
    batch, seq, d = x.shape
    M = batch * seq
    x = x.reshape(M, d)
    xb = x.astype(BF16)

    assert INPROJ_COLS == A_WIDTH == B_WIDTH
    h_gate = _matmul(xb, w_in_ab, 0, 2, lambda j: j + 1, INPROJ_GATE_ROWS, INPROJ_COLS, F32, "inproj_gates")
    h_act = _matmul(xb, w_in_ab, 0, 5, lambda j: j + 2 * jnp.minimum(j, 1), INPROJ_ACT_ROWS, INPROJ_COLS,
                    BF16, "inproj_acts")
    o_a = _hgrn(h_act.reshape(batch, seq, -1), h_gate.reshape(batch, seq, -1), hgrn_lb_table,
                hgrn_norm_g[0].reshape(1, -1), 0, batch, seq, HGRN_HEADS_PER_STEP)
    o_b = _gmlp(h_act, 3, gmlp_ln_g[0], gmlp_ln_b[0], gmlp_ws[0], gmlp_bias[0], rows=GMLP_ROWS)
    both = (F32, BF16)
    x, xb = _proj_ln([o_a.reshape(M, -1), o_b], w_out_ab, 0, x, ln1_g[0], ln1_b[0], OUTPROJ_ROWS, both,
                     "outproj_ln")
    x, xb = _conv_ffn_ln(x, xb, ffn_up, ffn_conv_w, ffn_conv_b, ffn_down, ln2_g, ln2_b, 0, seq, both)

    scale = C_HEAD_DIM ** -0.5 * math.log2(math.e)
    gain = jnp.concatenate([jnp.tile(q_norm_g[0] * scale, C_Q_HEADS), jnp.tile(k_norm_g[0], C_KV_HEADS),
                            jnp.ones((C_KV_HEADS * C_HEAD_DIM,), F32)])
    cos, sin = _rope_tables(seq)
    qkv = _qkv_proj(xb, w_in_attn, 0, gain.reshape(1, -1), cos, sin, seq, tm=QKV_ROWS, tn=QKV_COLS)
    att = _attention(qkv.reshape(batch, seq, -1), batch, seq, tq=ATTN_QUERIES, n_sub=ATTN_SUB_BLOCKS,
                     n_chunks=ATTN_KEY_CHUNKS)
    x, xb = _proj_ln([att.reshape(M, -1)], w_out_attn, 0, x, ln1_g[1], ln1_b[1], OUTPROJ_ROWS, both,
                     "outproj_ln")
    (x,) = _conv_ffn_ln(x, xb, ffn_up, ffn_conv_w, ffn_conv_b, ffn_down, ln2_g, ln2_b, 1, seq, (F32,))
    return x.reshape(batch, seq, d)
```
